```python
import jax, jax.numpy as jnp
from jax import lax
import numpy as np

D_MODEL = 1024
BATCH = 4
SEQ = 8192
DEPTH = 2

N_HEADS = 16
N_KV_HEADS = 4
HEAD_DIM = D_MODEL // N_HEADS
GROUP = N_HEADS // N_KV_HEADS
QKV_DIM = (N_HEADS + 2 * N_KV_HEADS) * HEAD_DIM
ROT_DIM = HEAD_DIM // 4
ROPE_THETA = 500000.0
MOBA_BLOCK = 256
MOBA_TOPK = 3
MOBA_QCHUNK = 16
SWA_WINDOW = 128
D_FF = 3584
N_EXPERTS = 8
TOP_K = 2
PLE_DIM = 256
LN_EPS = 1e-5
N_EVEN = (DEPTH + 1) // 2
N_ODD = DEPTH // 2

kernel_name = "moba_swa_sink_deepnorm_moe_hybrid"


def rope_tables(positions):
    inv = 1.0 / (ROPE_THETA ** (jnp.arange(0, ROT_DIM, 2, dtype=jnp.float32) / ROT_DIM))
    ang = positions.astype(jnp.float32)[..., None] * inv
    return jnp.cos(ang), jnp.sin(ang)


def apply_partial_rope(x, cos, sin):
    half = ROT_DIM // 2
    c = cos[:, :, None, :].astype(x.dtype)
    s = sin[:, :, None, :].astype(x.dtype)
    x1, x2, xp = x[..., :half], x[..., half:ROT_DIM], x[..., ROT_DIM:]
    return jnp.concatenate([x1 * c - x2 * s, x2 * c + x1 * s, xp], axis=-1)


def layer_norm(x, g, b):
    xf = x.astype(jnp.float32)
    mu = jnp.mean(xf, axis=-1, keepdims=True)
    var = jnp.mean(jnp.square(xf - mu), axis=-1, keepdims=True)
    return ((xf - mu) * lax.rsqrt(var + LN_EPS) * g.astype(jnp.float32) + b.astype(jnp.float32)).astype(x.dtype)


def project_qkv(x, w_qkv, cos, sin):
    B, S, _ = x.shape
    qkv = x @ w_qkv
    nq = N_HEADS * HEAD_DIM
    nk = N_KV_HEADS * HEAD_DIM
    q = qkv[..., :nq].reshape(B, S, N_HEADS, HEAD_DIM)
    k = qkv[..., nq:nq + nk].reshape(B, S, N_KV_HEADS, HEAD_DIM)
    v = qkv[..., nq + nk:].reshape(B, S, N_KV_HEADS, HEAD_DIM)
    return apply_partial_rope(q, cos, sin), apply_partial_rope(k, cos, sin), v


def moba_attention(q, k, v):
    B, S, H, Dh = q.shape
    nb = -(-S // MOBA_BLOCK)
    S_pad = nb * MOBA_BLOCK
    pad = ((0, 0), (0, S_pad - S), (0, 0), (0, 0))
    q = jnp.pad(q * (HEAD_DIM ** -0.5), pad)
    k = jnp.pad(k, pad)
    v = jnp.pad(v, pad)
    kb = k.reshape(B, nb, MOBA_BLOCK, N_KV_HEADS, Dh).transpose(0, 1, 3, 2, 4)
    vb = v.reshape(B, nb, MOBA_BLOCK, N_KV_HEADS, Dh).transpose(0, 1, 3, 2, 4)
    kmean = jnp.mean(kb.astype(jnp.float32), axis=3)
    qg = q.reshape(B, S_pad, N_KV_HEADS, GROUP, Dh).astype(jnp.float32)
    gate = jnp.einsum('bskgd,bnkd->bskgn', qg, kmean).reshape(B, S_pad, H, nb)
    qblk = jnp.arange(S_pad) // MOBA_BLOCK
    past = jnp.arange(nb)[None, :] < qblk[:, None]
    gate = jnp.where(past[None, :, None, :], gate, -jnp.inf)
    n_sel = min(MOBA_TOPK, nb)
    _, sel = lax.top_k(gate, n_sel)

    QC = MOBA_QCHUNK
    NC = S_pad // QC
    q_c = q.reshape(B, NC, QC, H, Dh).transpose(1, 0, 2, 3, 4)
    sel_c = sel.reshape(B, NC, QC, H, n_sel).transpose(1, 0, 2, 3, 4)
    b_idx = jnp.arange(B)[:, None, None, None]
    g_idx = (jnp.arange(H) // GROUP)[None, None, :, None]

    def chunk(args):
        qc, selc, cid = args
        t = cid * QC + jnp.arange(QC)
        c = (cid * QC) // MOBA_BLOCK
        kg = kb[b_idx, selc, g_idx]
        vg = vb[b_idx, selc, g_idx]
        lp = jnp.einsum('bqhd,bqhnjd->bqhnj', qc, kg).astype(jnp.float32)
        valid = jnp.arange(n_sel)[None, :] < (t // MOBA_BLOCK)[:, None]
        lp = jnp.where(valid[None, :, None, :, None], lp, -jnp.inf).reshape(B, QC, H, n_sel * MOBA_BLOCK)
        ko = lax.dynamic_index_in_dim(kb, c, axis=1, keepdims=False)
        vo = lax.dynamic_index_in_dim(vb, c, axis=1, keepdims=False)
        qcg = qc.reshape(B, QC, N_KV_HEADS, GROUP, Dh)
        lo = jnp.einsum('bqkgd,bkjd->bqkgj', qcg, ko).astype(jnp.float32).reshape(B, QC, H, MOBA_BLOCK)
        kpos = c * MOBA_BLOCK + jnp.arange(MOBA_BLOCK)
        lo = jnp.where((kpos[None, :] <= t[:, None])[None, :, None, :], lo, -jnp.inf)
        probs = jax.nn.softmax(jnp.concatenate([lp, lo], axis=-1), axis=-1)
        pp = probs[..., :n_sel * MOBA_BLOCK].reshape(B, QC, H, n_sel, MOBA_BLOCK).astype(v.dtype)
        po = probs[..., n_sel * MOBA_BLOCK:].reshape(B, QC, N_KV_HEADS, GROUP, MOBA_BLOCK).astype(v.dtype)
        out = jnp.einsum('bqhnj,bqhnjd->bqhd', pp, vg) + \
            jnp.einsum('bqkgj,bkjd->bqkgd', po, vo).reshape(B, QC, H, Dh)
        return out

    out = lax.map(chunk, (q_c, sel_c, jnp.arange(NC)))
    return out.transpose(1, 0, 2, 3, 4).reshape(B, S_pad, H, Dh)[:, :S]


def sliding_window_attention(q, k, v, sinks):
    B, S, H, Dh = q.shape
    W = SWA_WINDOW
    n = S // W
    qb = (q * (HEAD_DIM ** -0.5)).reshape(B, n, W, N_KV_HEADS, GROUP, Dh)
    kb = k.reshape(B, n, W, N_KV_HEADS, Dh)
    vb = v.reshape(B, n, W, N_KV_HEADS, Dh)
    zpad = ((0, 0), (1, 0), (0, 0), (0, 0), (0, 0))
    kk = jnp.concatenate([jnp.pad(kb, zpad)[:, :-1], kb], axis=2)
    vv = jnp.concatenate([jnp.pad(vb, zpad)[:, :-1], vb], axis=2)
    logits = jnp.einsum('bnqkgd,bnjkd->bnkgqj', qb, kk).astype(jnp.float32)
    rel = (jnp.arange(W)[:, None] + W) - jnp.arange(2 * W)[None, :]
    band = (rel >= 0) & (rel < W)
    first = (jnp.arange(n) == 0)[:, None, None] & (jnp.arange(2 * W) < W)[None, None, :]
    mask = band[None] & ~first
    logits = jnp.where(mask[None, :, None, None, :, :], logits, -jnp.inf)
    sink = sinks.astype(jnp.float32).reshape(1, 1, N_KV_HEADS, GROUP, 1, 1)
    m = jnp.maximum(jnp.max(logits, axis=-1, keepdims=True), sink)
    e = jnp.exp(logits - m)
    probs = e / (jnp.sum(e, axis=-1, keepdims=True) + jnp.exp(sink - m))
    out = jnp.einsum('bnkgqj,bnjkd->bnqkgd', probs.astype(v.dtype), vv)
    return out.reshape(B, S, H, Dh)


def swiglu(x, w_gate, w_up, w_down):
    return (jax.nn.silu(x @ w_gate) * (x @ w_up)) @ w_down


def moe_swiglu(x, w_router, w_gate, w_up, w_down):
    logits = (x @ w_router).astype(jnp.float32)
    top_v, top_i = lax.top_k(logits, TOP_K)
    top_w = jax.nn.softmax(top_v, axis=-1)
    gates = jnp.sum(jax.nn.one_hot(top_i, N_EXPERTS, dtype=jnp.float32) * top_w[..., None], axis=-2)
    out = jnp.zeros_like(x)
    for e in range(N_EXPERTS):
        out = out + gates[..., e:e + 1].astype(x.dtype) * swiglu(x, w_gate[e], w_up[e], w_down[e])
    return out


def setup_inputs(seed: int = 0) -> dict:
    key = jax.random.key(seed)
    ks = jax.random.split(key, 20)
    f32 = jnp.float32
    beta = (8.0 * DEPTH) ** -0.25

    def nrm(k, shape, fan_in, scale=1.0):
        return jax.random.normal(k, shape, f32) * (scale * fan_in ** -0.5)

    x = jax.random.normal(ks[0], (BATCH, SEQ, D_MODEL), f32)
    p = jax.random.normal(ks[1], (DEPTH, BATCH, SEQ, PLE_DIM), f32)
    positions = jnp.broadcast_to(jnp.arange(SEQ, dtype=jnp.int32)[None, :], (BATCH, SEQ))
    w_qkv = nrm(ks[2], (DEPTH, D_MODEL, QKV_DIM), D_MODEL)
    w_o = nrm(ks[3], (DEPTH, N_HEADS * HEAD_DIM, D_MODEL), N_HEADS * HEAD_DIM, beta)
    ln_mix_g = 1.0 + 0.05 * jax.random.normal(ks[4], (DEPTH, D_MODEL), f32)
    ln_mix_b = 0.02 * jax.random.normal(ks[5], (DEPTH, D_MODEL), f32)
    ln_ffn_g = 1.0 + 0.05 * jax.random.normal(ks[6], (DEPTH, D_MODEL), f32)
    ln_ffn_b = 0.02 * jax.random.normal(ks[7], (DEPTH, D_MODEL), f32)
    sinks = 0.5 * jax.random.normal(ks[8], (N_ODD, N_HEADS), f32)
    w_ffn_gate = nrm(ks[9], (N_EVEN, D_MODEL, D_FF), D_MODEL)
    w_ffn_up = nrm(ks[10], (N_EVEN, D_MODEL, D_FF), D_MODEL)
    w_ffn_down = nrm(ks[11], (N_EVEN, D_FF, D_MODEL), D_FF, beta)
    w_router = nrm(ks[12], (N_ODD, D_MODEL, N_EXPERTS), D_MODEL)
    w_exp_gate = nrm(ks[13], (N_ODD, N_EXPERTS, D_MODEL, D_FF), D_MODEL)
    w_exp_up = nrm(ks[14], (N_ODD, N_EXPERTS, D_MODEL, D_FF), D_MODEL)
    w_exp_down = nrm(ks[15], (N_ODD, N_EXPERTS, D_FF, D_MODEL), D_FF, beta)
    w_ple_proj = nrm(ks[16], (DEPTH, PLE_DIM, D_MODEL), PLE_DIM, 0.5)
    w_ple_gate = nrm(ks[17], (DEPTH, D_MODEL, D_MODEL), D_MODEL)
    return {"x": x, "p": p, "positions": positions, "w_qkv": w_qkv, "w_o": w_o,
            "ln_mix_g": ln_mix_g, "ln_mix_b": ln_mix_b, "ln_ffn_g": ln_ffn_g, "ln_ffn_b": ln_ffn_b,
            "sinks": sinks, "w_ffn_gate": w_ffn_gate, "w_ffn_up": w_ffn_up, "w_ffn_down": w_ffn_down,
            "w_router": w_router, "w_exp_gate": w_exp_gate, "w_exp_up": w_exp_up, "w_exp_down": w_exp_down,
            "w_ple_proj": w_ple_proj, "w_ple_gate": w_ple_gate}


def reference(x, p, positions, w_qkv, w_o, ln_mix_g, ln_mix_b, ln_ffn_g, ln_ffn_b, sinks,
              w_ffn_gate, w_ffn_up, w_ffn_down, w_router, w_exp_gate, w_exp_up, w_exp_down,
              w_ple_proj, w_ple_gate):
    alpha = (2.0 * DEPTH) ** 0.25
    B, S, _ = x.shape
    cos, sin = rope_tables(positions)
    for i in range(DEPTH):
        j = i // 2
        q, k, v = project_qkv(x, w_qkv[i], cos, sin)
        if i % 2 == 0:
            a = moba_attention(q, k, v)
        else:
            a = sliding_window_attention(q, k, v, sinks[j])
        x = layer_norm(alpha * x + a.reshape(B, S, N_HEADS * HEAD_DIM) @ w_o[i], ln_mix_g[i], ln_mix_b[i])
        if i % 2 == 0:
            f = swiglu(x, w_ffn_gate[j], w_ffn_up[j], w_ffn_down[j])
        else:
            f = moe_swiglu(x, w_router[j], w_exp_gate[j], w_exp_up[j], w_exp_down[j])
        x = layer_norm(alpha * x + f, ln_ffn_g[i], ln_ffn_b[i])
        x = x + (p[i] @ w_ple_proj[i]) * jax.nn.sigmoid(x @ w_ple_gate[i])
    return x
```

```python
import functools

import jax
import jax.numpy as jnp
import numpy as np
from jax import lax
from jax.experimental import pallas as pl
from jax.experimental.pallas import tpu as pltpu

D_MODEL = 1024
N_HEADS = 16
N_KV_HEADS = 4
HEAD_DIM = 64
GROUP = N_HEADS // N_KV_HEADS
QKV_DIM = (N_HEADS + 2 * N_KV_HEADS) * HEAD_DIM
ROT_DIM = 16
ROPE_THETA = 500000.0
MOBA_BLOCK = 256
MOBA_TOPK = 3
SWA_WINDOW = 128
D_FF = 3584
N_EXPERTS = 8
PLE_DIM = 256
LN_EPS = 1e-5
DEPTH = 2
ALPHA = (2.0 * DEPTH) ** 0.25

LANES = 128
NEG = -1e30
VMEM_LIMIT = 48 * 1024 * 1024

TM_QKV = 512
TM_MIX = 512
TM_FFN = 1024
FC_FFN = 512
TM_DISPATCH = 512
TM_OUT = 256
TQ_SWA = 512

f32 = jnp.float32
bf16 = jnp.bfloat16


def _cparams(*sem):
    return pltpu.CompilerParams(dimension_semantics=sem, vmem_limit_bytes=VMEM_LIMIT)


def _rope_kernel(pos_ref, inv_ref, cos_ref, sin_ref):
    ang = pos_ref[...].astype(f32) * inv_ref[...]
    cos_ref[...] = jnp.cos(ang)
    sin_ref[...] = jnp.sin(ang)


def _rope_tables(positions):
    n = positions.size
    half = ROT_DIM // 2
    inv = 1.0 / (ROPE_THETA ** (jnp.arange(0, ROT_DIM, 2, dtype=f32) / ROT_DIM))
    rows = n * half // LANES
    pos_rep = jnp.repeat(positions.reshape(-1), half).reshape(rows, LANES)
    inv_rep = jnp.tile(inv, LANES // half).reshape(1, LANES)
    cos, sin = pl.pallas_call(
        _rope_kernel,
        out_shape=(jax.ShapeDtypeStruct((rows, LANES), f32),) * 2,
        name="rope_tables",
    )(pos_rep, inv_rep)
    cos = cos.reshape(n, half)
    sin = sin.reshape(n, half)
    one = jnp.ones((n, HEAD_DIM - ROT_DIM), f32)
    zero = jnp.zeros((n, HEAD_DIM - ROT_DIM), f32)
    zh = jnp.zeros((n, half), f32)
    c_tab = jnp.tile(jnp.concatenate([cos, cos, one], axis=1), (1, 2))
    s_lo = jnp.tile(jnp.concatenate([-sin, zh, zero], axis=1), (1, 2))
    s_hi = jnp.tile(jnp.concatenate([zh, sin, zero], axis=1), (1, 2))
    return c_tab, s_lo, s_hi


def _qkv_kernel(x_ref, w_ref, c_ref, slo_ref, shi_ref, q_ref, k_ref, v_ref, *, tiles_per_seq):
    tm = x_ref.shape[0]
    acc = jnp.dot(x_ref[...].astype(bf16), w_ref[...], preferred_element_type=f32)
    lane = lax.broadcasted_iota(jnp.int32, (tm, LANES), 1)
    row = lax.broadcasted_iota(jnp.int32, (tm, LANES), 0)
    low = lane < HEAD_DIM
    c_tab, s_lo, s_hi = c_ref[...], slo_ref[...], shi_ref[...]
    seq0 = (pl.program_id(0) % tiles_per_seq) * tm
    blk = (seq0 + row) // MOBA_BLOCK
    k_pad = jnp.where(lane == HEAD_DIM + blk, 1.0, 0.0)
    v_pad = jnp.where(lane == HEAD_DIM, 1.0, 0.0)
    n_q = N_HEADS // 2
    n_kv = N_KV_HEADS // 2
    for c in range(n_q + n_kv):
        xc = acc[:, c * LANES:(c + 1) * LANES]
        r = xc * c_tab + pltpu.roll(xc, LANES - ROT_DIM // 2, 1) * s_lo + pltpu.roll(xc, ROT_DIM // 2, 1) * s_hi
        r_odd = pltpu.roll(r, HEAD_DIM, 1)
        if c < n_q:
            q_ref[0, 2 * c] = jnp.where(low, r * (HEAD_DIM ** -0.5), 0.0).astype(bf16)
            q_ref[0, 2 * c + 1] = jnp.where(low, r_odd * (HEAD_DIM ** -0.5), 0.0).astype(bf16)
        else:
            k_ref[0, 2 * (c - n_q)] = jnp.where(low, r, k_pad).astype(bf16)
            k_ref[0, 2 * (c - n_q) + 1] = jnp.where(low, r_odd, k_pad).astype(bf16)
    for c in range(n_kv):
        xc = acc[:, (n_q + n_kv + c) * LANES:(n_q + n_kv + c + 1) * LANES]
        v_ref[0, 2 * c] = jnp.where(low, xc, v_pad).astype(bf16)
        v_ref[0, 2 * c + 1] = jnp.where(low, pltpu.roll(xc, HEAD_DIM, 1), v_pad).astype(bf16)


def _qkv(x2d, w_qkv, tabs, batch, seq):
    n = x2d.shape[0]
    nt = seq // TM_QKV
    tok = lambda i: (i, 0)
    head = lambda i: (i // nt, 0, i % nt, 0)
    return pl.pallas_call(
        functools.partial(_qkv_kernel, tiles_per_seq=nt),
        grid=(n // TM_QKV,),
        in_specs=[pl.BlockSpec((TM_QKV, D_MODEL), tok),
                  pl.BlockSpec((D_MODEL, QKV_DIM), lambda i: (0, 0)),
                  pl.BlockSpec((TM_QKV, LANES), tok),
                  pl.BlockSpec((TM_QKV, LANES), tok),
                  pl.BlockSpec((TM_QKV, LANES), tok)],
        out_specs=[pl.BlockSpec((1, N_HEADS, TM_QKV, LANES), head),
                   pl.BlockSpec((1, N_KV_HEADS, TM_QKV, LANES), head),
                   pl.BlockSpec((1, N_KV_HEADS, TM_QKV, LANES), head)],
        out_shape=[jax.ShapeDtypeStruct((batch, N_HEADS, seq, LANES), bf16),
                   jax.ShapeDtypeStruct((batch, N_KV_HEADS, seq, LANES), bf16),
                   jax.ShapeDtypeStruct((batch, N_KV_HEADS, seq, LANES), bf16)],
        compiler_params=_cparams("parallel"),
        name="qkv_rope",
    )(x2d, w_qkv, *tabs)


def _merge_heads(o, out_ref, rows):
    lane = lax.broadcasted_iota(jnp.int32, (rows, LANES), 1)
    low = lane < HEAD_DIM
    for c in range(GROUP // 2):
        even = o[(2 * c) * rows:(2 * c + 1) * rows]
        odd = o[(2 * c + 1) * rows:(2 * c + 2) * rows]
        out_ref[0, :, c * LANES:(c + 1) * LANES] = jnp.where(low, even, pltpu.roll(odd, HEAD_DIM, 1)).astype(out_ref.dtype)


def _moba_kernel(q_ref, k_ref, v_ref, o_ref, kmean_ref):
    i = pl.program_id(2)
    blk = MOBA_BLOCK
    nb = k_ref.shape[2] // blk
    rows = GROUP * blk
    nt_dims = (((1,), (1,)), ((), ()))

    @pl.when(i == 0)
    def _():
        kmean_ref[...] = jnp.zeros_like(kmean_ref)
        for j in range(nb):
            kj = k_ref[0, 0, j * blk:(j + 1) * blk, :].astype(f32)
            kmean_ref[HEAD_DIM + j:HEAD_DIM + j + 1, :] = jnp.sum(kj, axis=0, keepdims=True) * (1.0 / blk)

    q = q_ref[0].reshape(rows, LANES)
    gate_t = lax.dot_general(kmean_ref[...], q.astype(f32), nt_dims, preferred_element_type=f32)
    g = gate_t[HEAD_DIM:HEAD_DIM + nb, :]
    jidx = lax.broadcasted_iota(jnp.int32, (nb, rows), 0)
    past = jidx < i
    g = jnp.where(past, g, -jnp.inf)
    sel = jidx == i
    for _ in range(MOBA_TOPK):
        mx = jnp.max(g, axis=0, keepdims=True)
        first = jnp.min(jnp.where(g == mx, jidx, nb), axis=0, keepdims=True)
        pick = jidx == first
        sel = jnp.logical_or(sel, jnp.logical_and(pick, past))
        g = jnp.where(pick, -jnp.inf, g)
    bias = jnp.where(sel, 0.0, NEG)
    bias_t = jnp.concatenate([jnp.zeros((HEAD_DIM, rows), f32), bias,
                              jnp.zeros((LANES - HEAD_DIM - nb, rows), f32)], axis=0)
    q_aug = (q.astype(f32) + bias_t.T).astype(bf16)

    off = pl.multiple_of(i * blk, blk)
    s = lax.dot_general(q_aug, k_ref[0, 0, pl.ds(off, blk), :], nt_dims, preferred_element_type=f32)
    t_in = lax.broadcasted_iota(jnp.int32, (rows, blk), 0) % blk
    col = lax.broadcasted_iota(jnp.int32, (rows, blk), 1)
    s = jnp.where(col <= t_in, s, NEG)
    m0 = jnp.max(s, axis=1, keepdims=True)
    p = jnp.exp(s - m0).astype(bf16)
    acc0 = jnp.dot(p, v_ref[0, 0, pl.ds(off, blk), :], preferred_element_type=f32)

    def body(j, carry):
        m, acc = carry
        o = pl.multiple_of(j * blk, blk)
        sj = lax.dot_general(q_aug, k_ref[0, 0, pl.ds(o, blk), :], nt_dims, preferred_element_type=f32)
        m_new = jnp.maximum(m, jnp.max(sj, axis=1, keepdims=True))
        pj = jnp.exp(sj - m_new).astype(bf16)
        acc = acc * jnp.exp(m - m_new) + jnp.dot(pj, v_ref[0, 0, pl.ds(o, blk), :], preferred_element_type=f32)
        return m_new, acc

    _, acc = lax.fori_loop(0, i, body, (m0, acc0))
    out = acc / acc[:, HEAD_DIM:HEAD_DIM + 1]
    _merge_heads(out, o_ref, blk)


def _moba(q, k, v):
    batch, _, seq, _ = q.shape
    nb = seq // MOBA_BLOCK
    return pl.pallas_call(
        _moba_kernel,
        grid=(batch, N_KV_HEADS, nb),
        in_specs=[pl.BlockSpec((1, GROUP, MOBA_BLOCK, LANES), lambda b, g, i: (b, g, i, 0)),
                  pl.BlockSpec((1, 1, seq, LANES), lambda b, g, i: (b, g, 0, 0)),
                  pl.BlockSpec((1, 1, seq, LANES), lambda b, g, i: (b, g, 0, 0))],
        out_specs=pl.BlockSpec((1, MOBA_BLOCK, GROUP * HEAD_DIM), lambda b, g, i: (b, i, g)),
        out_shape=jax.ShapeDtypeStruct((batch, seq, N_HEADS * HEAD_DIM), bf16),
        scratch_shapes=[pltpu.VMEM((LANES, LANES), f32)],
        compiler_params=_cparams("parallel", "parallel", "arbitrary"),
        name="moba_attention",
    )(q, k, v)


def _swa_kernel(sink_ref, q_ref, kc_ref, kp_ref, vc_ref, vp_ref, o_ref):
    g = pl.program_id(1)
    i = pl.program_id(2)
    w = SWA_WINDOW
    rows = GROUP * w
    nt_dims = (((1,), (1,)), ((), ()))
    r_iota = lax.broadcasted_iota(jnp.int32, (rows, w), 0)
    t_in = r_iota % w
    col = lax.broadcasted_iota(jnp.int32, (rows, w), 1)
    head_of_row = lax.broadcasted_iota(jnp.int32, (rows, 1), 0) // w
    sink = jnp.zeros((rows, 1), f32)
    for hh in range(GROUP):
        sink = jnp.where(head_of_row == hh, sink_ref[g * GROUP + hh], sink)
    for sb in range(q_ref.shape[2] // w):
        q = q_ref[0, :, sb * w:(sb + 1) * w, :].reshape(rows, LANES)
        k_cur = kc_ref[0, 0, sb * w:(sb + 1) * w, :]
        v_cur = vc_ref[0, 0, sb * w:(sb + 1) * w, :]
        if sb == 0:
            k_prev, v_prev = kp_ref[0, 0], vp_ref[0, 0]
            prev_ok = jnp.logical_and(col > t_in, i > 0)
        else:
            k_prev = kc_ref[0, 0, (sb - 1) * w:sb * w, :]
            v_prev = vc_ref[0, 0, (sb - 1) * w:sb * w, :]
            prev_ok = col > t_in
        s_cur = lax.dot_general(q, k_cur, nt_dims, preferred_element_type=f32)
        s_prev = lax.dot_general(q, k_prev, nt_dims, preferred_element_type=f32)
        s_cur = jnp.where(col <= t_in, s_cur, NEG)
        s_prev = jnp.where(prev_ok, s_prev, NEG)
        m = jnp.maximum(jnp.maximum(jnp.max(s_cur, axis=1, keepdims=True),
                                    jnp.max(s_prev, axis=1, keepdims=True)), sink)
        p_cur = jnp.exp(s_cur - m).astype(bf16)
        p_prev = jnp.exp(s_prev - m).astype(bf16)
        acc = (jnp.dot(p_cur, v_cur, preferred_element_type=f32)
               + jnp.dot(p_prev, v_prev, preferred_element_type=f32))
        out = acc / (acc[:, HEAD_DIM:HEAD_DIM + 1] + jnp.exp(sink - m))
        lane = lax.broadcasted_iota(jnp.int32, (w, LANES), 1)
        low = lane < HEAD_DIM
        for c in range(GROUP // 2):
            even = out[(2 * c) * w:(2 * c + 1) * w]
            odd = out[(2 * c + 1) * w:(2 * c + 2) * w]
            o_ref[0, sb * w:(sb + 1) * w, c * LANES:(c + 1) * LANES] = jnp.where(
                low, even, pltpu.roll(odd, HEAD_DIM, 1)).astype(o_ref.dtype)


def _swa(q, k, v, sinks):
    batch, _, seq, _ = q.shape
    per = TQ_SWA // SWA_WINDOW
    cur = lambda b, g, i, s: (b, g, i, 0)
    prev = lambda b, g, i, s: (b, g, jnp.maximum(i * per - 1, 0), 0)
    grid_spec = pltpu.PrefetchScalarGridSpec(
        num_scalar_prefetch=1,
        grid=(batch, N_KV_HEADS, seq // TQ_SWA),
        in_specs=[pl.BlockSpec((1, GROUP, TQ_SWA, LANES), cur),
                  pl.BlockSpec((1, 1, TQ_SWA, LANES), cur),
                  pl.BlockSpec((1, 1, SWA_WINDOW, LANES), prev),
                  pl.BlockSpec((1, 1, TQ_SWA, LANES), cur),
                  pl.BlockSpec((1, 1, SWA_WINDOW, LANES), prev)],
        out_specs=pl.BlockSpec((1, TQ_SWA, GROUP * HEAD_DIM), lambda b, g, i, s: (b, i, g)),
    )
    return pl.pallas_call(
        _swa_kernel,
        grid_spec=grid_spec,
        out_shape=jax.ShapeDtypeStruct((batch, seq, N_HEADS * HEAD_DIM), bf16),
        compiler_params=_cparams("parallel", "parallel", "arbitrary"),
        name="swa_attention",
    )(sinks, q, k, k, v, v)


def _layer_norm(x, g, b):
    mu = jnp.mean(x, axis=-1, keepdims=True)
    xc = x - mu
    var = jnp.mean(xc * xc, axis=-1, keepdims=True)
    return xc * lax.rsqrt(var + LN_EPS) * g + b


def _mix_kernel(x_ref, a_ref, wo_ref, g_ref, b_ref, x1_ref):
    y = ALPHA * x_ref[...] + jnp.dot(a_ref[...], wo_ref[...], preferred_element_type=f32)
    x1_ref[...] = _layer_norm(y, g_ref[...], b_ref[...])


def _mix_router_kernel(x_ref, a_ref, wo_ref, g_ref, b_ref, wr_ref, x1_ref, meta_ref, cnt_ref):
    tm = x_ref.shape[0]
    y = ALPHA * x_ref[...] + jnp.dot(a_ref[...], wo_ref[...], preferred_element_type=f32)
    x1 = _layer_norm(y, g_ref[...], b_ref[...])
    x1_ref[...] = x1

    @pl.when(pl.program_id(0) == 0)
    def _():
        cnt_ref[...] = jnp.zeros_like(cnt_ref)

    logits = jnp.dot(x1, wr_ref[...], preferred_element_type=f32)
    lane = lax.broadcasted_iota(jnp.int32, (tm, LANES), 1)
    lg = jnp.where(lane < N_EXPERTS, logits, -jnp.inf)
    m1 = jnp.max(lg, axis=1, keepdims=True)
    i1 = jnp.min(jnp.where(lg == m1, lane, LANES), axis=1, keepdims=True)
    lg2 = jnp.where(lane == i1, -jnp.inf, lg)
    m2 = jnp.max(lg2, axis=1, keepdims=True)
    i2 = jnp.min(jnp.where(lg2 == m2, lane, LANES), axis=1, keepdims=True)
    e2 = jnp.exp(m2 - m1)
    w1 = 1.0 / (1.0 + e2)
    w2 = e2 / (1.0 + e2)
    hit1 = lane == i1
    hit2 = lane == i2
    hits = jnp.where(jnp.logical_or(hit1, hit2), 1.0, 0.0)
    r_i = lax.broadcasted_iota(jnp.int32, (tm, tm), 0)
    c_i = lax.broadcasted_iota(jnp.int32, (tm, tm), 1)
    before = jnp.where(c_i < r_i, 1.0, 0.0).astype(bf16)
    cum = jnp.dot(before, hits.astype(bf16), preferred_element_type=f32) + cnt_ref[0:1, :]
    r1 = jnp.sum(jnp.where(hit1, cum, 0.0), axis=1, keepdims=True)
    r2 = jnp.sum(jnp.where(hit2, cum, 0.0), axis=1, keepdims=True)
    cnt_ref[...] = cnt_ref[...] + jnp.sum(hits, axis=0, keepdims=True)
    meta = jnp.where(lane == 0, i1.astype(f32), 0.0)
    meta = jnp.where(lane == 1, i2.astype(f32), meta)
    meta = jnp.where(lane == 2, w1, meta)
    meta = jnp.where(lane == 3, w2, meta)
    meta = jnp.where(lane == 4, r1, meta)
    meta = jnp.where(lane == 5, r2, meta)
    meta_ref[...] = meta


def _mix(x2d, a2d, w_o, g, b, w_router=None):
    n = x2d.shape[0]
    tok = lambda i: (i, 0)
    const = lambda i: (0, 0)
    in_specs = [pl.BlockSpec((TM_MIX, D_MODEL), tok),
                pl.BlockSpec((TM_MIX, D_MODEL), tok),
                pl.BlockSpec((D_MODEL, D_MODEL), const),
                pl.BlockSpec((1, D_MODEL), const),
                pl.BlockSpec((1, D_MODEL), const)]
    x1_spec = pl.BlockSpec((TM_MIX, D_MODEL), tok)
    x1_shape = jax.ShapeDtypeStruct((n, D_MODEL), f32)
    if w_router is None:
        return pl.pallas_call(
            _mix_kernel, grid=(n // TM_MIX,), in_specs=in_specs, out_specs=x1_spec, out_shape=x1_shape,
            compiler_params=_cparams("parallel"), name="mix_ln",
        )(x2d, a2d, w_o, g, b)
    wr = jnp.zeros((D_MODEL, LANES), f32).at[:, :N_EXPERTS].set(w_router)
    return pl.pallas_call(
        _mix_router_kernel, grid=(n // TM_MIX,),
        in_specs=in_specs + [pl.BlockSpec((D_MODEL, LANES), const)],
        out_specs=[x1_spec, pl.BlockSpec((TM_MIX, LANES), tok), pl.BlockSpec((8, LANES), const)],
        out_shape=[x1_shape, jax.ShapeDtypeStruct((n, LANES), f32), jax.ShapeDtypeStruct((8, LANES), f32)],
        compiler_params=_cparams("arbitrary"), name="mix_ln_router",
    )(x2d, a2d, w_o, g, b, wr)


def _ffn_kernel(te_ref, nact_ref, x_ref, wg_ref, wu_ref, wd_ref, y_ref, xb_ref):
    n = pl.program_id(0)
    fc = pl.program_id(1)
    active = n < nact_ref[0]

    @pl.when(jnp.logical_and(jnp.logical_not(active), fc == 0))
    def _():
        y_ref[...] = jnp.zeros_like(y_ref)

    @pl.when(active)
    def _():
        @pl.when(fc == 0)
        def _():
            xb_ref[...] = x_ref[...].astype(bf16)

        xb = xb_ref[...]
        gate = jnp.dot(xb, wg_ref[0], preferred_element_type=f32)
        up = jnp.dot(xb, wu_ref[0], preferred_element_type=f32)
        h = (gate / (1.0 + jnp.exp(-gate)) * up).astype(bf16)
        part = jnp.dot(h, wd_ref[0], preferred_element_type=f32)

        @pl.when(fc == 0)
        def _():
            y_ref[...] = part

        @pl.when(fc > 0)
        def _():
            y_ref[...] = y_ref[...] + part


def _ffn(xs, w_gate, w_up, w_down, tile_expert, n_active):
    n_tiles = xs.shape[0] // TM_FFN
    n_fc = D_FF // FC_FFN

    def row(n, f, te, na):
        return jnp.minimum(n, na[0] - 1), 0

    def w_in(n, f, te, na):
        return te[jnp.minimum(n, na[0] - 1)], 0, jnp.where(n < na[0], f, n_fc - 1)

    def w_out(n, f, te, na):
        return te[jnp.minimum(n, na[0] - 1)], jnp.where(n < na[0], f, n_fc - 1), 0

    grid_spec = pltpu.PrefetchScalarGridSpec(
        num_scalar_prefetch=2,
        grid=(n_tiles, n_fc),
        in_specs=[pl.BlockSpec((TM_FFN, D_MODEL), row),
                  pl.BlockSpec((1, D_MODEL, FC_FFN), w_in),
                  pl.BlockSpec((1, D_MODEL, FC_FFN), w_in),
                  pl.BlockSpec((1, FC_FFN, D_MODEL), w_out)],
        out_specs=pl.BlockSpec((TM_FFN, D_MODEL), lambda n, f, te, na: (n, 0)),
        scratch_shapes=[pltpu.VMEM((TM_FFN, D_MODEL), bf16)],
    )
    return pl.pallas_call(
        _ffn_kernel, grid_spec=grid_spec,
        out_shape=jax.ShapeDtypeStruct(xs.shape, f32),
        compiler_params=_cparams("arbitrary", "arbitrary"), name="swiglu",
    )(tile_expert, n_active, xs, w_gate, w_up, w_down)


def _dispatch_kernel(pos_ref, x_ref, init_ref, xs_ref, sem):
    del init_ref
    tm = x_ref.shape[0]

    def start(r, _):
        for k in range(2):
            dst = pos_ref[2 * r + k]
            pltpu.make_async_copy(x_ref.at[pl.ds(r, 1)], xs_ref.at[pl.ds(dst, 1)], sem).start()
        return 0

    lax.fori_loop(0, tm, start, 0)
    for _ in range(2):
        pltpu.make_async_copy(x_ref, xs_ref.at[pl.ds(0, tm)], sem).wait()


def _dispatch(x1, pos_flat, n_rows):
    n = x1.shape[0]
    init = jnp.zeros((n_rows, D_MODEL), f32)
    return pl.pallas_call(
        _dispatch_kernel, grid=(n // TM_DISPATCH,),
        in_specs=[pl.BlockSpec((2 * TM_DISPATCH,), lambda i: (i,), memory_space=pltpu.SMEM),
                  pl.BlockSpec((TM_DISPATCH, D_MODEL), lambda i: (i, 0)),
                  pl.BlockSpec(memory_space=pl.ANY)],
        out_specs=pl.BlockSpec(memory_space=pl.ANY),
        out_shape=jax.ShapeDtypeStruct((n_rows, D_MODEL), f32),
        scratch_shapes=[pltpu.SemaphoreType.DMA(())],
        input_output_aliases={2: 0},
        compiler_params=_cparams("arbitrary"), name="moe_dispatch",
    )(pos_flat, x1, init)


def _ple(x2, p_ref, wp_ref, wgate_ref, o_ref):
    pe = jnp.dot(p_ref[...].astype(bf16), wp_ref[...], preferred_element_type=f32)
    z = jnp.dot(x2.astype(bf16), wgate_ref[...], preferred_element_type=f32)
    o_ref[...] = x2 + pe / (1.0 + jnp.exp(-z))


def _out_dense_kernel(x1_ref, f_ref, g_ref, b_ref, p_ref, wp_ref, wgate_ref, o_ref):
    x2 = _layer_norm(ALPHA * x1_ref[...] + f_ref[...], g_ref[...], b_ref[...])
    _ple(x2, p_ref, wp_ref, wgate_ref, o_ref)


def _out_moe_kernel(pos_ref, x1_ref, meta_ref, y_ref, g_ref, b_ref, p_ref, wp_ref, wgate_ref, o_ref, buf_ref, sem):
    tm = x1_ref.shape[0]

    def start(r, _):
        for k in range(2):
            src = pos_ref[2 * r + k]
            pltpu.make_async_copy(y_ref.at[pl.ds(src, 1)], buf_ref.at[k, pl.ds(r, 1)], sem).start()
        return 0

    lax.fori_loop(0, tm, start, 0)
    for k in range(2):
        pltpu.make_async_copy(y_ref.at[pl.ds(0, tm)], buf_ref.at[k], sem).wait()
    meta = meta_ref[...]
    f = meta[:, 2:3] * buf_ref[0] + meta[:, 3:4] * buf_ref[1]
    x2 = _layer_norm(ALPHA * x1_ref[...] + f, g_ref[...], b_ref[...])
    _ple(x2, p_ref, wp_ref, wgate_ref, o_ref)


def _out_specs_common():
    tok = lambda i: (i, 0)
    const = lambda i: (0, 0)
    return tok, [pl.BlockSpec((1, D_MODEL), const),
                 pl.BlockSpec((1, D_MODEL), const),
                 pl.BlockSpec((TM_OUT, PLE_DIM), tok),
                 pl.BlockSpec((PLE_DIM, D_MODEL), const),
                 pl.BlockSpec((D_MODEL, D_MODEL), const)]


def _out_dense(x1, f, g, b, p2d, w_proj, w_gate):
    n = x1.shape[0]
    tok, tail = _out_specs_common()
    return pl.pallas_call(
        _out_dense_kernel, grid=(n // TM_OUT,),
        in_specs=[pl.BlockSpec((TM_OUT, D_MODEL), tok), pl.BlockSpec((TM_OUT, D_MODEL), tok)] + tail,
        out_specs=pl.BlockSpec((TM_OUT, D_MODEL), tok),
        out_shape=jax.ShapeDtypeStruct((n, D_MODEL), f32),
        compiler_params=_cparams("parallel"), name="ffn_ln_ple",
    )(x1, f, g, b, p2d, w_proj, w_gate)


def _out_moe(x1, meta, y, pos_flat, g, b, p2d, w_proj, w_gate):
    n = x1.shape[0]
    tok, tail = _out_specs_common()
    return pl.pallas_call(
        _out_moe_kernel, grid=(n // TM_OUT,),
        in_specs=[pl.BlockSpec((2 * TM_OUT,), lambda i: (i,), memory_space=pltpu.SMEM),
                  pl.BlockSpec((TM_OUT, D_MODEL), tok),
                  pl.BlockSpec((TM_OUT, LANES), tok),
                  pl.BlockSpec(memory_space=pl.ANY)] + tail,
        out_specs=pl.BlockSpec((TM_OUT, D_MODEL), tok),
        out_shape=jax.ShapeDtypeStruct((n, D_MODEL), f32),
        scratch_shapes=[pltpu.VMEM((2, TM_OUT, D_MODEL), f32), pltpu.SemaphoreType.DMA(())],
        compiler_params=_cparams("arbitrary"), name="moe_combine_ln_ple",
    )(pos_flat, x1, meta, y, g, b, p2d, w_proj, w_gate)


def _routing_plan(meta, counts_row, n_tiles):
    i1 = meta[:, 0].astype(jnp.int32)
    i2 = meta[:, 1].astype(jnp.int32)
    r1 = meta[:, 4].astype(jnp.int32)
    r2 = meta[:, 5].astype(jnp.int32)
    counts = counts_row[0, :N_EXPERTS].astype(jnp.int32)
    tiles = (counts + TM_FFN - 1) // TM_FFN
    tile_end = jnp.cumsum(tiles)
    offset = (tile_end - tiles) * TM_FFN
    pos = jnp.stack([offset[i1] + r1, offset[i2] + r2], axis=1).reshape(-1)
    tile_expert = jnp.sum(jnp.arange(n_tiles)[:, None] >= tile_end[None, :], axis=1)
    tile_expert = jnp.minimum(tile_expert, N_EXPERTS - 1)
    return pos.astype(jnp.int32), tile_expert.astype(jnp.int32), tile_end[-1:].astype(jnp.int32)


def kernel(x, p, positions, w_qkv, w_o, ln_mix_g, ln_mix_b, ln_ffn_g, ln_ffn_b, sinks, w_ffn_gate, w_ffn_up,
           w_ffn_down, w_router, w_exp_gate, w_exp_up, w_exp_down, w_ple_proj, w_ple_gate):
    batch, seq, _ = x.shape
    n = batch * seq
    assert seq % TQ_SWA == 0 and (seq // MOBA_BLOCK) % 8 == 0 and seq // MOBA_BLOCK <= LANES - HEAD_DIM
    tabs = _rope_tables(positions)
    x2d = x.reshape(n, D_MODEL)
    row = lambda v: v.reshape(1, D_MODEL)
    for i in range(DEPTH):
        j = i // 2
        q, k, v = _qkv(x2d, w_qkv[i].astype(bf16), tabs, batch, seq)
        if i % 2 == 0:
            a = _moba(q, k, v)
        else:
            a = _swa(q, k, v, sinks[j])
        a2d = a.reshape(n, D_MODEL)
        wo = w_o[i].astype(bf16)
        ple_args = (row(ln_ffn_g[i]), row(ln_ffn_b[i]), p[i].reshape(n, PLE_DIM),
                    w_ple_proj[i].astype(bf16), w_ple_gate[i].astype(bf16))
        if i % 2 == 0:
            x1 = _mix(x2d, a2d, wo, row(ln_mix_g[i]), row(ln_mix_b[i]))
            n_tiles = n // TM_FFN
            f = _ffn(x1, w_ffn_gate[j].astype(bf16)[None], w_ffn_up[j].astype(bf16)[None],
                     w_ffn_down[j].astype(bf16)[None], jnp.zeros((n_tiles,), jnp.int32),
                     jnp.full((1,), n_tiles, jnp.int32))
            x2d = _out_dense(x1, f, *ple_args)
        else:
            x1, meta, counts = _mix(x2d, a2d, wo, row(ln_mix_g[i]), row(ln_mix_b[i]), w_router[j])
            n_tiles = 2 * n // TM_FFN + N_EXPERTS
            pos, tile_expert, n_active = _routing_plan(meta, counts, n_tiles)
            xs = _dispatch(x1, pos, n_tiles * TM_FFN)
            y = _ffn(xs, w_exp_gate[j].astype(bf16), w_exp_up[j].astype(bf16), w_exp_down[j].astype(bf16),
                     tile_expert, n_active)
            x2d = _out_moe(x1, meta, y, pos, *ple_args)
    return x2d.reshape(batch, seq, D_MODEL)
```

```python
import functools

import jax
import jax.numpy as jnp
import numpy as np
from jax import lax
from jax.experimental import pallas as pl
from jax.experimental.pallas import tpu as pltpu

D_MODEL = 1024
N_HEADS = 16
N_KV_HEADS = 4
HEAD_DIM = 64
GROUP = N_HEADS // N_KV_HEADS
QKV_DIM = (N_HEADS + 2 * N_KV_HEADS) * HEAD_DIM
ROT_DIM = 16
ROPE_THETA = 500000.0
MOBA_BLOCK = 256
MOBA_TOPK = 3
SWA_WINDOW = 128
D_FF = 3584
N_EXPERTS = 8
PLE_DIM = 256
LN_EPS = 1e-5
DEPTH = 2
ALPHA = (2.0 * DEPTH) ** 0.25
LOG2E = 1.4426950408889634
Q_SCALE = HEAD_DIM ** -0.5 * LOG2E

LANES = 128
ROW_TILE = D_MODEL // LANES
DMA_UNROLL = 8
NEG = -1e30
VMEM_LIMIT = 48 * 1024 * 1024

TM_QKV = 512
TM_MIX = 512
TM_FFN = 1024
FC_FFN = 512
TM_DISPATCH = 512
TM_OUT = 256
TQ_SWA = 512
MOBA_SUB = 512
MOBA_KEYS = 2 * MOBA_BLOCK

f32 = jnp.float32
bf16 = jnp.bfloat16


def _cparams(*sem):
    return pltpu.CompilerParams(dimension_semantics=sem, vmem_limit_bytes=VMEM_LIMIT)


def _rope_kernel(pos_ref, inv_ref, cos_ref, sin_ref):
    ang = pos_ref[...].astype(f32) * inv_ref[...]
    cos_ref[...] = jnp.cos(ang)
    sin_ref[...] = jnp.sin(ang)


def _rope_tables(positions):
    n = positions.size
    half = ROT_DIM // 2
    inv = 1.0 / (ROPE_THETA ** (jnp.arange(0, ROT_DIM, 2, dtype=f32) / ROT_DIM))
    rows = n * half // LANES
    pos_rep = jnp.repeat(positions.reshape(-1), half).reshape(rows, LANES)
    inv_rep = jnp.tile(inv, LANES // half).reshape(1, LANES)
    cos, sin = pl.pallas_call(
        _rope_kernel,
        out_shape=(jax.ShapeDtypeStruct((rows, LANES), f32),) * 2,
        name="rope_tables",
    )(pos_rep, inv_rep)
    cos = cos.reshape(n, half)
    sin = sin.reshape(n, half)
    one = jnp.ones((n, HEAD_DIM - ROT_DIM), f32)
    zero = jnp.zeros((n, HEAD_DIM - ROT_DIM), f32)
    zh = jnp.zeros((n, half), f32)
    c_tab = jnp.tile(jnp.concatenate([cos, cos, one], axis=1), (1, 2))
    s_lo = jnp.tile(jnp.concatenate([-sin, zh, zero], axis=1), (1, 2))
    s_hi = jnp.tile(jnp.concatenate([zh, sin, zero], axis=1), (1, 2))
    return c_tab, s_lo, s_hi


def _qkv_kernel(x_ref, w_ref, c_ref, slo_ref, shi_ref, q_ref, k_ref, v_ref, *, tiles_per_seq):
    tm = x_ref.shape[0]
    acc = jnp.dot(x_ref[...].astype(bf16), w_ref[...], preferred_element_type=f32)
    lane = lax.broadcasted_iota(jnp.int32, (tm, LANES), 1)
    row = lax.broadcasted_iota(jnp.int32, (tm, LANES), 0)
    low = lane < HEAD_DIM
    c_tab, s_lo, s_hi = c_ref[...], slo_ref[...], shi_ref[...]
    seq0 = (pl.program_id(0) % tiles_per_seq) * tm
    blk = (seq0 + row) // MOBA_BLOCK
    k_pad = jnp.where(lane == HEAD_DIM + blk, 1.0, 0.0)
    v_pad = jnp.where(lane == HEAD_DIM, 1.0, 0.0)
    n_q = N_HEADS // 2
    n_kv = N_KV_HEADS // 2
    for c in range(n_q + n_kv):
        xc = acc[:, c * LANES:(c + 1) * LANES]
        r = xc * c_tab + pltpu.roll(xc, LANES - ROT_DIM // 2, 1) * s_lo + pltpu.roll(xc, ROT_DIM // 2, 1) * s_hi
        r_odd = pltpu.roll(r, HEAD_DIM, 1)
        if c < n_q:
            q_ref[0, 2 * c] = jnp.where(low, r * Q_SCALE, 0.0).astype(bf16)
            q_ref[0, 2 * c + 1] = jnp.where(low, r_odd * Q_SCALE, 0.0).astype(bf16)
        else:
            k_ref[0, 2 * (c - n_q)] = jnp.where(low, r, k_pad).astype(bf16)
            k_ref[0, 2 * (c - n_q) + 1] = jnp.where(low, r_odd, k_pad).astype(bf16)
    for c in range(n_kv):
        xc = acc[:, (n_q + n_kv + c) * LANES:(n_q + n_kv + c + 1) * LANES]
        v_ref[0, 2 * c] = jnp.where(low, xc, v_pad).astype(bf16)
        v_ref[0, 2 * c + 1] = jnp.where(low, pltpu.roll(xc, HEAD_DIM, 1), v_pad).astype(bf16)


def _qkv(x2d, w_qkv, tabs, batch, seq):
    n = x2d.shape[0]
    nt = seq // TM_QKV
    tok = lambda i: (i, 0)
    head = lambda i: (i // nt, 0, i % nt, 0)
    return pl.pallas_call(
        functools.partial(_qkv_kernel, tiles_per_seq=nt),
        grid=(n // TM_QKV,),
        in_specs=[pl.BlockSpec((TM_QKV, D_MODEL), tok),
                  pl.BlockSpec((D_MODEL, QKV_DIM), lambda i: (0, 0)),
                  pl.BlockSpec((TM_QKV, LANES), tok),
                  pl.BlockSpec((TM_QKV, LANES), tok),
                  pl.BlockSpec((TM_QKV, LANES), tok)],
        out_specs=[pl.BlockSpec((1, N_HEADS, TM_QKV, LANES), head),
                   pl.BlockSpec((1, N_KV_HEADS, TM_QKV, LANES), head),
                   pl.BlockSpec((1, N_KV_HEADS, TM_QKV, LANES), head)],
        out_shape=[jax.ShapeDtypeStruct((batch, N_HEADS, seq, LANES), bf16),
                   jax.ShapeDtypeStruct((batch, N_KV_HEADS, seq, LANES), bf16),
                   jax.ShapeDtypeStruct((batch, N_KV_HEADS, seq, LANES), bf16)],
        compiler_params=_cparams("parallel"),
        name="qkv_rope",
    )(x2d, w_qkv, *tabs)


def _merge_heads(o, out_ref, rows):
    lane = lax.broadcasted_iota(jnp.int32, (rows, LANES), 1)
    low = lane < HEAD_DIM
    for c in range(GROUP // 2):
        even = o[(2 * c) * rows:(2 * c + 1) * rows]
        odd = o[(2 * c + 1) * rows:(2 * c + 2) * rows]
        out_ref[0, :, c * LANES:(c + 1) * LANES] = jnp.where(low, even, pltpu.roll(odd, HEAD_DIM, 1)).astype(out_ref.dtype)


def _moba_kernel(q_ref, k_ref, v_ref, o_ref, kmean_ref, qa_ref, s0_ref, s1_ref, m_ref, acc_ref):
    u = pl.program_id(2)
    blk = MOBA_BLOCK
    tq = q_ref.shape[2]
    nb = k_ref.shape[2] // blk
    rows = GROUP * tq
    sub = MOBA_SUB
    n_sub = rows // sub
    nt_dims = (((1,), (1,)), ((), ()))

    @pl.when(u == 0)
    def _():
        for j in range(nb):
            kj = k_ref[0, 0, j * blk:(j + 1) * blk, :].astype(f32)
            kmean_ref[j:j + 1, :] = jnp.sum(kj, axis=0, keepdims=True) * (1.0 / blk)

    q = q_ref[0].reshape(rows, LANES)
    km = kmean_ref[...]
    km_hi = km.astype(bf16)
    km_lo = (km - km_hi.astype(f32)).astype(bf16)
    g = (lax.dot_general(km_hi, q, nt_dims, preferred_element_type=f32)
         + lax.dot_general(km_lo, q, nt_dims, preferred_element_type=f32))
    jidx = lax.broadcasted_iota(jnp.int32, (nb, rows), 0)
    q_blk = u * (tq // blk) + (lax.broadcasted_iota(jnp.int32, (nb, rows), 1) % tq) // blk
    past = jidx < q_blk
    g = jnp.where(past, g, -jnp.inf)
    sel = jidx == q_blk
    for _ in range(MOBA_TOPK):
        mx = jnp.max(g, axis=0, keepdims=True)
        first = jnp.min(jnp.where(g == mx, jidx, nb), axis=0, keepdims=True)
        pick = jidx == first
        sel = jnp.logical_or(sel, jnp.logical_and(pick, past))
        g = jnp.where(pick, -jnp.inf, g)
    bias = jnp.where(sel, 0.0, NEG)
    bias_t = jnp.concatenate([jnp.zeros((HEAD_DIM, rows), f32), bias,
                              jnp.zeros((LANES - HEAD_DIM - nb, rows), f32)], axis=0)
    qa_ref[...] = (q.astype(f32) + bias_t.T).astype(bf16)

    m_ref[...] = jnp.full_like(m_ref, NEG)
    acc_ref[...] = jnp.zeros_like(acc_ref)

    def scores(c, s_ref, r):
        o = pl.multiple_of(c * MOBA_KEYS, MOBA_KEYS)
        rs = slice(r * sub, (r + 1) * sub)
        s_ref[rs] = lax.dot_general(qa_ref[rs], k_ref[0, 0, pl.ds(o, MOBA_KEYS), :], nt_dims,
                                    preferred_element_type=f32)

    def accumulate(c, s_ref, r, causal):
        o = pl.multiple_of(c * MOBA_KEYS, MOBA_KEYS)
        rs = slice(r * sub, (r + 1) * sub)
        s = s_ref[rs]
        if causal:
            col = lax.broadcasted_iota(jnp.int32, (sub, MOBA_KEYS), 1)
            tok = (lax.broadcasted_iota(jnp.int32, (sub, MOBA_KEYS), 0) + r * sub) % tq
            s = jnp.where(jnp.logical_and(col // blk == tok // blk, col > tok), NEG, s)
        m_old = m_ref[rs]
        m_new = jnp.maximum(m_old, jnp.max(s, axis=1, keepdims=True))
        p = jnp.exp2(s - m_new).astype(bf16)
        pv = jnp.dot(p, v_ref[0, 0, pl.ds(o, MOBA_KEYS), :], preferred_element_type=f32)
        acc_ref[rs] = acc_ref[rs] * jnp.exp2(m_old - m_new) + pv
        m_ref[rs] = m_new

    def step(c_next, s_next, c_cur, s_cur, causal=False):
        for r in range(n_sub):
            if c_next is not None:
                scores(c_next, s_next, r)
            accumulate(c_cur, s_cur, r, causal)

    for r in range(n_sub):
        scores(0, s0_ref, r)

    def body(t, carry):
        step(2 * t + 1, s1_ref, 2 * t, s0_ref)
        step(2 * t + 2, s0_ref, 2 * t + 1, s1_ref)
        return carry

    lax.fori_loop(0, u // 2, body, 0)

    @pl.when(u % 2 == 1)
    def _():
        step(u, s1_ref, u - 1, s0_ref)
        step(None, None, u, s1_ref, causal=True)

    @pl.when(u % 2 == 0)
    def _():
        step(None, None, u, s0_ref, causal=True)

    acc = acc_ref[...]
    _merge_heads(acc / acc[:, HEAD_DIM:HEAD_DIM + 1], o_ref, tq)


def _moba(q, k, v):
    batch, _, seq, _ = q.shape
    nb = seq // MOBA_BLOCK
    rows = GROUP * MOBA_KEYS
    return pl.pallas_call(
        _moba_kernel,
        grid=(batch, N_KV_HEADS, seq // MOBA_KEYS),
        in_specs=[pl.BlockSpec((1, GROUP, MOBA_KEYS, LANES), lambda b, g, u: (b, g, u, 0)),
                  pl.BlockSpec((1, 1, seq, LANES), lambda b, g, u: (b, g, 0, 0)),
                  pl.BlockSpec((1, 1, seq, LANES), lambda b, g, u: (b, g, 0, 0))],
        out_specs=pl.BlockSpec((1, MOBA_KEYS, GROUP * HEAD_DIM), lambda b, g, u: (b, u, g)),
        out_shape=jax.ShapeDtypeStruct((batch, seq, N_HEADS * HEAD_DIM), bf16),
        scratch_shapes=[pltpu.VMEM((nb, LANES), f32),
                        pltpu.VMEM((rows, LANES), bf16),
                        pltpu.VMEM((rows, MOBA_KEYS), f32),
                        pltpu.VMEM((rows, MOBA_KEYS), f32),
                        pltpu.VMEM((rows, 1), f32),
                        pltpu.VMEM((rows, LANES), f32)],
        compiler_params=_cparams("parallel", "parallel", "arbitrary"),
        name="moba_attention",
    )(q, k, v)


def _swa_kernel(sink_ref, q_ref, kc_ref, kp_ref, vc_ref, vp_ref, o_ref):
    g = pl.program_id(1)
    i = pl.program_id(2)
    w = SWA_WINDOW
    rows = GROUP * w
    nt_dims = (((1,), (1,)), ((), ()))
    n_sb = q_ref.shape[2] // w
    t_in = lax.broadcasted_iota(jnp.int32, (rows, 2 * w), 0) % w
    col = lax.broadcasted_iota(jnp.int32, (rows, 2 * w), 1)
    band = jnp.logical_and(col > t_in, col <= t_in + w)
    head_of_row = lax.broadcasted_iota(jnp.int32, (rows, 1), 0) // w
    sink = jnp.zeros((rows, 1), f32)
    for hh in range(GROUP):
        sink = jnp.where(head_of_row == hh, sink_ref[g * GROUP + hh] * LOG2E, sink)

    def window(ref_cur, ref_prev, sb):
        if sb == 0:
            return jnp.concatenate([ref_prev[0, 0], ref_cur[0, 0, 0:w, :]], axis=0)
        return ref_cur[0, 0, (sb - 1) * w:(sb + 1) * w, :]

    scores = [lax.dot_general(q_ref[0, :, sb * w:(sb + 1) * w, :].reshape(rows, LANES), window(kc_ref, kp_ref, sb),
                              nt_dims, preferred_element_type=f32) for sb in range(n_sb)]
    for sb in range(n_sb):
        ok = band
        if sb == 0:
            ok = jnp.logical_and(band, jnp.logical_or(col >= w, i > 0))
        s = jnp.where(ok, scores[sb], NEG)
        m = jnp.maximum(jnp.max(s, axis=1, keepdims=True), sink)
        p = jnp.exp2(s - m).astype(bf16)
        acc = jnp.dot(p, window(vc_ref, vp_ref, sb), preferred_element_type=f32)
        out = acc / (acc[:, HEAD_DIM:HEAD_DIM + 1] + jnp.exp2(sink - m))
        lane = lax.broadcasted_iota(jnp.int32, (w, LANES), 1)
        low = lane < HEAD_DIM
        for c in range(GROUP // 2):
            even = out[(2 * c) * w:(2 * c + 1) * w]
            odd = out[(2 * c + 1) * w:(2 * c + 2) * w]
            o_ref[0, sb * w:(sb + 1) * w, c * LANES:(c + 1) * LANES] = jnp.where(
                low, even, pltpu.roll(odd, HEAD_DIM, 1)).astype(o_ref.dtype)


def _swa(q, k, v, sinks):
    batch, _, seq, _ = q.shape
    per = TQ_SWA // SWA_WINDOW
    cur = lambda b, g, i, s: (b, g, i, 0)
    prev = lambda b, g, i, s: (b, g, jnp.maximum(i * per - 1, 0), 0)
    grid_spec = pltpu.PrefetchScalarGridSpec(
        num_scalar_prefetch=1,
        grid=(batch, N_KV_HEADS, seq // TQ_SWA),
        in_specs=[pl.BlockSpec((1, GROUP, TQ_SWA, LANES), cur),
                  pl.BlockSpec((1, 1, TQ_SWA, LANES), cur),
                  pl.BlockSpec((1, 1, SWA_WINDOW, LANES), prev),
                  pl.BlockSpec((1, 1, TQ_SWA, LANES), cur),
                  pl.BlockSpec((1, 1, SWA_WINDOW, LANES), prev)],
        out_specs=pl.BlockSpec((1, TQ_SWA, GROUP * HEAD_DIM), lambda b, g, i, s: (b, i, g)),
    )
    return pl.pallas_call(
        _swa_kernel,
        grid_spec=grid_spec,
        out_shape=jax.ShapeDtypeStruct((batch, seq, N_HEADS * HEAD_DIM), bf16),
        compiler_params=_cparams("parallel", "parallel", "arbitrary"),
        name="swa_attention",
    )(sinks, q, k, k, v, v)


def _layer_norm(x, g, b):
    mu = jnp.mean(x, axis=-1, keepdims=True)
    xc = x - mu
    var = jnp.mean(xc * xc, axis=-1, keepdims=True)
    return xc * lax.rsqrt(var + LN_EPS) * g + b


def _mix_kernel(x_ref, a_ref, wo_ref, g_ref, b_ref, x1_ref):
    y = ALPHA * x_ref[...] + jnp.dot(a_ref[...], wo_ref[...], preferred_element_type=f32)
    x1_ref[...] = _layer_norm(y, g_ref[...], b_ref[...])


def _load_rows(ref, n_rows):
    return jnp.concatenate([ref[pl.ds(k, n_rows, stride=ROW_TILE), :] for k in range(ROW_TILE)], axis=1)


def _store_rows(ref, val):
    for k in range(ROW_TILE):
        ref[pl.ds(k, val.shape[0], stride=ROW_TILE), :] = val[:, k * LANES:(k + 1) * LANES]


def _mix_router_kernel(x_ref, a_ref, wo_ref, g_ref, b_ref, wr_ref, x1_ref, meta_ref, cnt_ref):
    tm = x_ref.shape[0]
    y = ALPHA * x_ref[...] + jnp.dot(a_ref[...], wo_ref[...], preferred_element_type=f32)
    x1 = _layer_norm(y, g_ref[...], b_ref[...])
    _store_rows(x1_ref, x1)

    @pl.when(pl.program_id(0) == 0)
    def _():
        cnt_ref[...] = jnp.zeros_like(cnt_ref)

    logits = jnp.dot(x1, wr_ref[...], preferred_element_type=f32)
    lane = lax.broadcasted_iota(jnp.int32, (tm, LANES), 1)
    lg = jnp.where(lane < N_EXPERTS, logits, -jnp.inf)
    m1 = jnp.max(lg, axis=1, keepdims=True)
    i1 = jnp.min(jnp.where(lg == m1, lane, LANES), axis=1, keepdims=True)
    lg2 = jnp.where(lane == i1, -jnp.inf, lg)
    m2 = jnp.max(lg2, axis=1, keepdims=True)
    i2 = jnp.min(jnp.where(lg2 == m2, lane, LANES), axis=1, keepdims=True)
    e2 = jnp.exp(m2 - m1)
    w1 = 1.0 / (1.0 + e2)
    w2 = e2 / (1.0 + e2)
    hit1 = lane == i1
    hit2 = lane == i2
    hits = jnp.where(jnp.logical_or(hit1, hit2), 1.0, 0.0)
    r_i = lax.broadcasted_iota(jnp.int32, (tm, tm), 0)
    c_i = lax.broadcasted_iota(jnp.int32, (tm, tm), 1)
    before = jnp.where(c_i < r_i, 1.0, 0.0).astype(bf16)
    cum = jnp.dot(before, hits.astype(bf16), preferred_element_type=f32) + cnt_ref[0:1, :]
    r1 = jnp.sum(jnp.where(hit1, cum, 0.0), axis=1, keepdims=True)
    r2 = jnp.sum(jnp.where(hit2, cum, 0.0), axis=1, keepdims=True)
    cnt_ref[...] = cnt_ref[...] + jnp.sum(hits, axis=0, keepdims=True)
    meta = jnp.where(lane == 0, i1.astype(f32), 0.0)
    meta = jnp.where(lane == 1, i2.astype(f32), meta)
    meta = jnp.where(lane == 2, w1, meta)
    meta = jnp.where(lane == 3, w2, meta)
    meta = jnp.where(lane == 4, r1, meta)
    meta = jnp.where(lane == 5, r2, meta)
    meta_ref[...] = meta


def _mix(x2d, a2d, w_o, g, b, w_router=None):
    n = x2d.shape[0]
    tok = lambda i: (i, 0)
    const = lambda i: (0, 0)
    in_specs = [pl.BlockSpec((TM_MIX, D_MODEL), tok),
                pl.BlockSpec((TM_MIX, D_MODEL), tok),
                pl.BlockSpec((D_MODEL, D_MODEL), const),
                pl.BlockSpec((1, D_MODEL), const),
                pl.BlockSpec((1, D_MODEL), const)]
    x1_spec = pl.BlockSpec((TM_MIX, D_MODEL), tok)
    x1_shape = jax.ShapeDtypeStruct((n, D_MODEL), f32)
    if w_router is None:
        return pl.pallas_call(
            _mix_kernel, grid=(n // TM_MIX,), in_specs=in_specs, out_specs=x1_spec, out_shape=x1_shape,
            compiler_params=_cparams("parallel"), name="mix_ln",
        )(x2d, a2d, w_o, g, b)
    wr = jnp.zeros((D_MODEL, LANES), f32).at[:, :N_EXPERTS].set(w_router)
    return pl.pallas_call(
        _mix_router_kernel, grid=(n // TM_MIX,),
        in_specs=in_specs + [pl.BlockSpec((D_MODEL, LANES), const)],
        out_specs=[pl.BlockSpec((TM_MIX * ROW_TILE, LANES), tok),
                   pl.BlockSpec((TM_MIX, LANES), tok), pl.BlockSpec((8, LANES), const)],
        out_shape=[jax.ShapeDtypeStruct((n * ROW_TILE, LANES), f32),
                   jax.ShapeDtypeStruct((n, LANES), f32), jax.ShapeDtypeStruct((8, LANES), f32)],
        compiler_params=_cparams("arbitrary"), name="mix_ln_router",
    )(x2d, a2d, w_o, g, b, wr)


def _ffn_kernel(te_ref, nact_ref, x_ref, wg_ref, wu_ref, wd_ref, y_ref, xb_ref, *acc, row_major):
    n = pl.program_id(0)
    fc = pl.program_id(1)
    n_fc = pl.num_programs(1)
    active = n < nact_ref[0]
    acc_ref = acc[0] if row_major else y_ref

    @pl.when(jnp.logical_and(jnp.logical_not(active), fc == 0))
    def _():
        y_ref[...] = jnp.zeros_like(y_ref)

    @pl.when(active)
    def _():
        @pl.when(fc == 0)
        def _():
            x = _load_rows(x_ref, TM_FFN) if row_major else x_ref[...]
            xb_ref[...] = x.astype(bf16)

        xb = xb_ref[...]
        gate = jnp.dot(xb, wg_ref[0], preferred_element_type=f32)
        up = jnp.dot(xb, wu_ref[0], preferred_element_type=f32)
        h = (gate / (1.0 + jnp.exp(-gate)) * up).astype(bf16)
        part = jnp.dot(h, wd_ref[0], preferred_element_type=f32)

        @pl.when(fc == 0)
        def _():
            acc_ref[...] = part

        @pl.when(fc > 0)
        def _():
            acc_ref[...] = acc_ref[...] + part

        if row_major:
            @pl.when(fc == n_fc - 1)
            def _():
                _store_rows(y_ref, acc_ref[...])


def _ffn(xs, w_gate, w_up, w_down, tile_expert, n_active, row_major):
    rows_per = ROW_TILE if row_major else 1
    blk = (TM_FFN * ROW_TILE, LANES) if row_major else (TM_FFN, D_MODEL)
    n_tiles = xs.shape[0] // (TM_FFN * rows_per)
    n_fc = D_FF // FC_FFN

    def row(n, f, te, na):
        return jnp.minimum(n, na[0] - 1), 0

    def w_in(n, f, te, na):
        return te[jnp.minimum(n, na[0] - 1)], 0, jnp.where(n < na[0], f, n_fc - 1)

    def w_out(n, f, te, na):
        return te[jnp.minimum(n, na[0] - 1)], jnp.where(n < na[0], f, n_fc - 1), 0

    grid_spec = pltpu.PrefetchScalarGridSpec(
        num_scalar_prefetch=2,
        grid=(n_tiles, n_fc),
        in_specs=[pl.BlockSpec(blk, row),
                  pl.BlockSpec((1, D_MODEL, FC_FFN), w_in),
                  pl.BlockSpec((1, D_MODEL, FC_FFN), w_in),
                  pl.BlockSpec((1, FC_FFN, D_MODEL), w_out)],
        out_specs=pl.BlockSpec(blk, lambda n, f, te, na: (n, 0)),
        scratch_shapes=[pltpu.VMEM((TM_FFN, D_MODEL), bf16)]
        + ([pltpu.VMEM((TM_FFN, D_MODEL), f32)] if row_major else []),
    )
    return pl.pallas_call(
        functools.partial(_ffn_kernel, row_major=row_major), grid_spec=grid_spec,
        out_shape=jax.ShapeDtypeStruct(xs.shape, f32),
        compiler_params=_cparams("arbitrary", "arbitrary"), name="swiglu",
    )(tile_expert, n_active, xs, w_gate, w_up, w_down)


def _dispatch_kernel(pos_ref, x_ref, init_ref, xs_ref, sem):
    del init_ref
    tm = x_ref.shape[0] // ROW_TILE

    def start(r, _):
        src = x_ref.at[pl.ds(pl.multiple_of(r * ROW_TILE, ROW_TILE), ROW_TILE)]
        for k in range(2):
            dst = pl.multiple_of(pos_ref[2 * r + k] * ROW_TILE, ROW_TILE)
            pltpu.make_async_copy(src, xs_ref.at[pl.ds(dst, ROW_TILE)], sem).start()
        return 0

    lax.fori_loop(0, tm, start, 0, unroll=DMA_UNROLL)
    for _ in range(2):
        pltpu.make_async_copy(x_ref, xs_ref.at[pl.ds(0, tm * ROW_TILE)], sem).wait()


def _dispatch(x1_rows, pos_flat, n_rows):
    n = x1_rows.shape[0] // ROW_TILE
    init = jnp.zeros((n_rows * ROW_TILE, LANES), f32)
    return pl.pallas_call(
        _dispatch_kernel, grid=(n // TM_DISPATCH,),
        in_specs=[pl.BlockSpec((2 * TM_DISPATCH,), lambda i: (i,), memory_space=pltpu.SMEM),
                  pl.BlockSpec((TM_DISPATCH * ROW_TILE, LANES), lambda i: (i, 0)),
                  pl.BlockSpec(memory_space=pl.ANY)],
        out_specs=pl.BlockSpec(memory_space=pl.ANY),
        out_shape=jax.ShapeDtypeStruct((n_rows * ROW_TILE, LANES), f32),
        scratch_shapes=[pltpu.SemaphoreType.DMA(())],
        input_output_aliases={2: 0},
        compiler_params=_cparams("arbitrary"), name="moe_dispatch",
    )(pos_flat, x1_rows, init)


def _ple(x2, p_ref, wp_ref, wgate_ref, o_ref):
    pe = jnp.dot(p_ref[...].astype(bf16), wp_ref[...], preferred_element_type=f32)
    z = jnp.dot(x2.astype(bf16), wgate_ref[...], preferred_element_type=f32)
    o_ref[...] = x2 + pe / (1.0 + jnp.exp(-z))


def _out_dense_kernel(x1_ref, f_ref, g_ref, b_ref, p_ref, wp_ref, wgate_ref, o_ref):
    x2 = _layer_norm(ALPHA * x1_ref[...] + f_ref[...], g_ref[...], b_ref[...])
    _ple(x2, p_ref, wp_ref, wgate_ref, o_ref)


def _out_moe_kernel(pos_ref, x1_ref, meta_ref, y_ref, g_ref, b_ref, p_ref, wp_ref, wgate_ref, o_ref, buf_ref, sem):
    tm = x1_ref.shape[0] // ROW_TILE

    def start(r, _):
        dst = pl.ds(pl.multiple_of(r * ROW_TILE, ROW_TILE), ROW_TILE)
        for k in range(2):
            src = pl.multiple_of(pos_ref[2 * r + k] * ROW_TILE, ROW_TILE)
            pltpu.make_async_copy(y_ref.at[pl.ds(src, ROW_TILE)], buf_ref.at[k, dst], sem).start()
        return 0

    lax.fori_loop(0, tm, start, 0, unroll=DMA_UNROLL)
    for k in range(2):
        pltpu.make_async_copy(y_ref.at[pl.ds(0, tm * ROW_TILE)], buf_ref.at[k], sem).wait()
    meta = meta_ref[...]
    f = meta[:, 2:3] * _load_rows(buf_ref.at[0], tm) + meta[:, 3:4] * _load_rows(buf_ref.at[1], tm)
    x2 = _layer_norm(ALPHA * _load_rows(x1_ref, tm) + f, g_ref[...], b_ref[...])
    _ple(x2, p_ref, wp_ref, wgate_ref, o_ref)


def _out_specs_common():
    tok = lambda i: (i, 0)
    const = lambda i: (0, 0)
    return tok, [pl.BlockSpec((1, D_MODEL), const),
                 pl.BlockSpec((1, D_MODEL), const),
                 pl.BlockSpec((TM_OUT, PLE_DIM), tok),
                 pl.BlockSpec((PLE_DIM, D_MODEL), const),
                 pl.BlockSpec((D_MODEL, D_MODEL), const)]


def _out_dense(x1, f, g, b, p2d, w_proj, w_gate):
    n = x1.shape[0]
    tok, tail = _out_specs_common()
    return pl.pallas_call(
        _out_dense_kernel, grid=(n // TM_OUT,),
        in_specs=[pl.BlockSpec((TM_OUT, D_MODEL), tok), pl.BlockSpec((TM_OUT, D_MODEL), tok)] + tail,
        out_specs=pl.BlockSpec((TM_OUT, D_MODEL), tok),
        out_shape=jax.ShapeDtypeStruct((n, D_MODEL), f32),
        compiler_params=_cparams("parallel"), name="ffn_ln_ple",
    )(x1, f, g, b, p2d, w_proj, w_gate)


def _out_moe(x1_rows, meta, y_rows, pos_flat, g, b, p2d, w_proj, w_gate):
    n = x1_rows.shape[0] // ROW_TILE
    tok, tail = _out_specs_common()
    return pl.pallas_call(
        _out_moe_kernel, grid=(n // TM_OUT,),
        in_specs=[pl.BlockSpec((2 * TM_OUT,), lambda i: (i,), memory_space=pltpu.SMEM),
                  pl.BlockSpec((TM_OUT * ROW_TILE, LANES), tok),
                  pl.BlockSpec((TM_OUT, LANES), tok),
                  pl.BlockSpec(memory_space=pl.ANY)] + tail,
        out_specs=pl.BlockSpec((TM_OUT, D_MODEL), tok),
        out_shape=jax.ShapeDtypeStruct((n, D_MODEL), f32),
        scratch_shapes=[pltpu.VMEM((2, TM_OUT * ROW_TILE, LANES), f32), pltpu.SemaphoreType.DMA(())],
        compiler_params=_cparams("arbitrary"), name="moe_combine_ln_ple",
    )(pos_flat, x1_rows, meta, y_rows, g, b, p2d, w_proj, w_gate)


def _routing_plan(meta, counts_row, n_tiles):
    i1 = meta[:, 0].astype(jnp.int32)
    i2 = meta[:, 1].astype(jnp.int32)
    r1 = meta[:, 4].astype(jnp.int32)
    r2 = meta[:, 5].astype(jnp.int32)
    counts = counts_row[0, :N_EXPERTS].astype(jnp.int32)
    tiles = (counts + TM_FFN - 1) // TM_FFN
    tile_end = jnp.cumsum(tiles)
    offset = (tile_end - tiles) * TM_FFN
    pos = jnp.stack([offset[i1] + r1, offset[i2] + r2], axis=1).reshape(-1)
    tile_expert = jnp.sum(jnp.arange(n_tiles)[:, None] >= tile_end[None, :], axis=1)
    tile_expert = jnp.minimum(tile_expert, N_EXPERTS - 1)
    return pos.astype(jnp.int32), tile_expert.astype(jnp.int32), tile_end[-1:].astype(jnp.int32)


def kernel(x, p, positions, w_qkv, w_o, ln_mix_g, ln_mix_b, ln_ffn_g, ln_ffn_b, sinks, w_ffn_gate, w_ffn_up,
           w_ffn_down, w_router, w_exp_gate, w_exp_up, w_exp_down, w_ple_proj, w_ple_gate):
    batch, seq, _ = x.shape
    n = batch * seq
    assert seq % TQ_SWA == 0 and (seq // MOBA_BLOCK) % 8 == 0 and seq // MOBA_BLOCK <= LANES - HEAD_DIM
    tabs = _rope_tables(positions)
    x2d = x.reshape(n, D_MODEL)
    row = lambda v: v.reshape(1, D_MODEL)
    for i in range(DEPTH):
        j = i // 2
        q, k, v = _qkv(x2d, w_qkv[i].astype(bf16), tabs, batch, seq)
        if i % 2 == 0:
            a = _moba(q, k, v)
        else:
            a = _swa(q, k, v, sinks[j])
        a2d = a.reshape(n, D_MODEL)
        wo = w_o[i].astype(bf16)
        ple_args = (row(ln_ffn_g[i]), row(ln_ffn_b[i]), p[i].reshape(n, PLE_DIM),
                    w_ple_proj[i].astype(bf16), w_ple_gate[i].astype(bf16))
        if i % 2 == 0:
            x1 = _mix(x2d, a2d, wo, row(ln_mix_g[i]), row(ln_mix_b[i]))
            n_tiles = n // TM_FFN
            f = _ffn(x1, w_ffn_gate[j].astype(bf16)[None], w_ffn_up[j].astype(bf16)[None],
                     w_ffn_down[j].astype(bf16)[None], jnp.zeros((n_tiles,), jnp.int32),
                     jnp.full((1,), n_tiles, jnp.int32), row_major=False)
            x2d = _out_dense(x1, f, *ple_args)
        else:
            x1, meta, counts = _mix(x2d, a2d, wo, row(ln_mix_g[i]), row(ln_mix_b[i]), w_router[j])
            n_tiles = 2 * n // TM_FFN + N_EXPERTS
            pos, tile_expert, n_active = _routing_plan(meta, counts, n_tiles)
            xs = _dispatch(x1, pos, n_tiles * TM_FFN)
            y = _ffn(xs, w_exp_gate[j].astype(bf16), w_exp_up[j].astype(bf16), w_exp_down[j].astype(bf16),
                     tile_expert, n_active, row_major=True)
            x2d = _out_moe(x1, meta, y, pos, *ple_args)
    return x2d.reshape(batch, seq, D_MODEL)
```

```python
import functools

import jax
import jax.numpy as jnp
import numpy as np
from jax import lax
from jax.experimental import pallas as pl
from jax.experimental.pallas import tpu as pltpu

D_MODEL = 1024
N_HEADS = 16
N_KV_HEADS = 4
HEAD_DIM = 64
GROUP = N_HEADS // N_KV_HEADS
QKV_DIM = (N_HEADS + 2 * N_KV_HEADS) * HEAD_DIM
ROT_DIM = 16
ROPE_THETA = 500000.0
MOBA_BLOCK = 256
MOBA_TOPK = 3
SWA_WINDOW = 128
D_FF = 3584
N_EXPERTS = 8
PLE_DIM = 256
LN_EPS = 1e-5
DEPTH = 2
ALPHA = (2.0 * DEPTH) ** 0.25
LOG2E = 1.4426950408889634
Q_SCALE = HEAD_DIM ** -0.5 * LOG2E

LANES = 128
ROW_TILE = D_MODEL // LANES
DMA_UNROLL = 8
NEG = -1e30
VMEM_LIMIT = 48 * 1024 * 1024

TM_QKV = 512
TM_MIX = 512
TM_DENSE = 512
TM_FFN = 1024
FC_FFN = 512
TM_DISPATCH = 512
TM_OUT = 256
TQ_SWA = 512
MOBA_SUB = 512
MOBA_KEYS = 2 * MOBA_BLOCK

f32 = jnp.float32
bf16 = jnp.bfloat16


def _cparams(*sem):
    return pltpu.CompilerParams(dimension_semantics=sem, vmem_limit_bytes=VMEM_LIMIT)


def _rope_kernel(pos_ref, inv_ref, cos_ref, sin_ref):
    ang = pos_ref[...].astype(f32) * inv_ref[...]
    cos_ref[...] = jnp.cos(ang)
    sin_ref[...] = jnp.sin(ang)


def _rope_tables(positions):
    n = positions.size
    half = ROT_DIM // 2
    inv = 1.0 / (ROPE_THETA ** (jnp.arange(0, ROT_DIM, 2, dtype=f32) / ROT_DIM))
    rows = n * half // LANES
    pos_rep = jnp.repeat(positions.reshape(-1), half).reshape(rows, LANES)
    inv_rep = jnp.tile(inv, LANES // half).reshape(1, LANES)
    cos, sin = pl.pallas_call(
        _rope_kernel,
        out_shape=(jax.ShapeDtypeStruct((rows, LANES), f32),) * 2,
        name="rope_tables",
    )(pos_rep, inv_rep)
    cos = cos.reshape(n, half)
    sin = sin.reshape(n, half)
    one = jnp.ones((n, HEAD_DIM - ROT_DIM), f32)
    zero = jnp.zeros((n, HEAD_DIM - ROT_DIM), f32)
    zh = jnp.zeros((n, half), f32)
    c_tab = jnp.tile(jnp.concatenate([cos, cos, one], axis=1), (1, 2))
    s_lo = jnp.tile(jnp.concatenate([-sin, zh, zero], axis=1), (1, 2))
    s_hi = jnp.tile(jnp.concatenate([zh, sin, zero], axis=1), (1, 2))
    return c_tab, s_lo, s_hi


def _qkv_kernel(x_ref, w_ref, c_ref, slo_ref, shi_ref, q_ref, k_ref, v_ref, *, tiles_per_seq):
    tm = x_ref.shape[0]
    acc = jnp.dot(x_ref[...].astype(bf16), w_ref[...], preferred_element_type=f32)
    lane = lax.broadcasted_iota(jnp.int32, (tm, LANES), 1)
    row = lax.broadcasted_iota(jnp.int32, (tm, LANES), 0)
    low = lane < HEAD_DIM
    c_tab, s_lo, s_hi = c_ref[...], slo_ref[...], shi_ref[...]
    seq0 = (pl.program_id(0) % tiles_per_seq) * tm
    blk = (seq0 + row) // MOBA_BLOCK
    k_pad = jnp.where(lane == HEAD_DIM + blk, 1.0, 0.0)
    v_pad = jnp.where(lane == HEAD_DIM, 1.0, 0.0)
    n_q = N_HEADS // 2
    n_kv = N_KV_HEADS // 2
    for c in range(n_q + n_kv):
        xc = acc[:, c * LANES:(c + 1) * LANES]
        r = xc * c_tab + pltpu.roll(xc, LANES - ROT_DIM // 2, 1) * s_lo + pltpu.roll(xc, ROT_DIM // 2, 1) * s_hi
        r_odd = pltpu.roll(r, HEAD_DIM, 1)
        if c < n_q:
            q_ref[0, 2 * c] = jnp.where(low, r * Q_SCALE, 0.0).astype(bf16)
            q_ref[0, 2 * c + 1] = jnp.where(low, r_odd * Q_SCALE, 0.0).astype(bf16)
        else:
            k_ref[0, 2 * (c - n_q)] = jnp.where(low, r, k_pad).astype(bf16)
            k_ref[0, 2 * (c - n_q) + 1] = jnp.where(low, r_odd, k_pad).astype(bf16)
    for c in range(n_kv):
        xc = acc[:, (n_q + n_kv + c) * LANES:(n_q + n_kv + c + 1) * LANES]
        v_ref[0, 2 * c] = jnp.where(low, xc, v_pad).astype(bf16)
        v_ref[0, 2 * c + 1] = jnp.where(low, pltpu.roll(xc, HEAD_DIM, 1), v_pad).astype(bf16)


def _qkv(x2d, w_qkv, tabs, batch, seq):
    n = x2d.shape[0]
    nt = seq // TM_QKV
    tok = lambda i: (i, 0)
    head = lambda i: (i // nt, 0, i % nt, 0)
    return pl.pallas_call(
        functools.partial(_qkv_kernel, tiles_per_seq=nt),
        grid=(n // TM_QKV,),
        in_specs=[pl.BlockSpec((TM_QKV, D_MODEL), tok),
                  pl.BlockSpec((D_MODEL, QKV_DIM), lambda i: (0, 0)),
                  pl.BlockSpec((TM_QKV, LANES), tok),
                  pl.BlockSpec((TM_QKV, LANES), tok),
                  pl.BlockSpec((TM_QKV, LANES), tok)],
        out_specs=[pl.BlockSpec((1, N_HEADS, TM_QKV, LANES), head),
                   pl.BlockSpec((1, N_KV_HEADS, TM_QKV, LANES), head),
                   pl.BlockSpec((1, N_KV_HEADS, TM_QKV, LANES), head)],
        out_shape=[jax.ShapeDtypeStruct((batch, N_HEADS, seq, LANES), bf16),
                   jax.ShapeDtypeStruct((batch, N_KV_HEADS, seq, LANES), bf16),
                   jax.ShapeDtypeStruct((batch, N_KV_HEADS, seq, LANES), bf16)],
        compiler_params=_cparams("parallel"),
        name="qkv_rope",
    )(x2d, w_qkv, *tabs)


def _merge_heads(o, out_ref, rows):
    lane = lax.broadcasted_iota(jnp.int32, (rows, LANES), 1)
    low = lane < HEAD_DIM
    for c in range(GROUP // 2):
        even = o[(2 * c) * rows:(2 * c + 1) * rows]
        odd = o[(2 * c + 1) * rows:(2 * c + 2) * rows]
        out_ref[0, :, c * LANES:(c + 1) * LANES] = jnp.where(low, even, pltpu.roll(odd, HEAD_DIM, 1)).astype(out_ref.dtype)


def _moba_kernel(q_ref, k_ref, v_ref, o_ref, kmean_ref, qa_ref, s0_ref, s1_ref, m_ref, acc_ref):
    u = pl.program_id(2)
    blk = MOBA_BLOCK
    tq = q_ref.shape[2]
    nb = k_ref.shape[2] // blk
    rows = GROUP * tq
    sub = MOBA_SUB
    n_sub = rows // sub
    nt_dims = (((1,), (1,)), ((), ()))

    @pl.when(u == 0)
    def _():
        for j in range(nb):
            kj = k_ref[0, 0, j * blk:(j + 1) * blk, :].astype(f32)
            kmean_ref[j:j + 1, :] = jnp.sum(kj, axis=0, keepdims=True) * (1.0 / blk)

    q = q_ref[0].reshape(rows, LANES)
    km = kmean_ref[...]
    km_hi = km.astype(bf16)
    km_lo = (km - km_hi.astype(f32)).astype(bf16)
    g = (lax.dot_general(km_hi, q, nt_dims, preferred_element_type=f32)
         + lax.dot_general(km_lo, q, nt_dims, preferred_element_type=f32))
    jidx = lax.broadcasted_iota(jnp.int32, (nb, rows), 0)
    q_blk = u * (tq // blk) + (lax.broadcasted_iota(jnp.int32, (nb, rows), 1) % tq) // blk
    past = jidx < q_blk
    g = jnp.where(past, g, -jnp.inf)
    sel = jidx == q_blk
    for _ in range(MOBA_TOPK):
        mx = jnp.max(g, axis=0, keepdims=True)
        first = jnp.min(jnp.where(g == mx, jidx, nb), axis=0, keepdims=True)
        pick = jidx == first
        sel = jnp.logical_or(sel, jnp.logical_and(pick, past))
        g = jnp.where(pick, -jnp.inf, g)
    bias = jnp.where(sel, 0.0, NEG)
    bias_t = jnp.concatenate([jnp.zeros((HEAD_DIM, rows), f32), bias,
                              jnp.zeros((LANES - HEAD_DIM - nb, rows), f32)], axis=0)
    qa_ref[...] = (q.astype(f32) + bias_t.T).astype(bf16)

    m_ref[...] = jnp.full_like(m_ref, NEG)
    acc_ref[...] = jnp.zeros_like(acc_ref)

    def scores(c, s_ref, r):
        o = pl.multiple_of(c * MOBA_KEYS, MOBA_KEYS)
        rs = slice(r * sub, (r + 1) * sub)
        s_ref[rs] = lax.dot_general(qa_ref[rs], k_ref[0, 0, pl.ds(o, MOBA_KEYS), :], nt_dims,
                                    preferred_element_type=f32)

    def accumulate(c, s_ref, r, causal):
        o = pl.multiple_of(c * MOBA_KEYS, MOBA_KEYS)
        rs = slice(r * sub, (r + 1) * sub)
        s = s_ref[rs]
        if causal:
            col = lax.broadcasted_iota(jnp.int32, (sub, MOBA_KEYS), 1)
            tok = (lax.broadcasted_iota(jnp.int32, (sub, MOBA_KEYS), 0) + r * sub) % tq
            s = jnp.where(jnp.logical_and(col // blk == tok // blk, col > tok), NEG, s)
        m_old = m_ref[rs]
        m_new = jnp.maximum(m_old, jnp.max(s, axis=1, keepdims=True))
        p = jnp.exp2(s - m_new).astype(bf16)
        pv = jnp.dot(p, v_ref[0, 0, pl.ds(o, MOBA_KEYS), :], preferred_element_type=f32)
        acc_ref[rs] = acc_ref[rs] * jnp.exp2(m_old - m_new) + pv
        m_ref[rs] = m_new

    def step(c_next, s_next, c_cur, s_cur, causal=False):
        for r in range(n_sub):
            if c_next is not None:
                scores(c_next, s_next, r)
            accumulate(c_cur, s_cur, r, causal)

    for r in range(n_sub):
        scores(0, s0_ref, r)

    def body(t, carry):
        step(2 * t + 1, s1_ref, 2 * t, s0_ref)
        step(2 * t + 2, s0_ref, 2 * t + 1, s1_ref)
        return carry

    lax.fori_loop(0, u // 2, body, 0)

    @pl.when(u % 2 == 1)
    def _():
        step(u, s1_ref, u - 1, s0_ref)
        step(None, None, u, s1_ref, causal=True)

    @pl.when(u % 2 == 0)
    def _():
        step(None, None, u, s0_ref, causal=True)

    acc = acc_ref[...]
    _merge_heads(acc / acc[:, HEAD_DIM:HEAD_DIM + 1], o_ref, tq)


def _moba(q, k, v):
    batch, _, seq, _ = q.shape
    nb = seq // MOBA_BLOCK
    rows = GROUP * MOBA_KEYS
    return pl.pallas_call(
        _moba_kernel,
        grid=(batch, N_KV_HEADS, seq // MOBA_KEYS),
        in_specs=[pl.BlockSpec((1, GROUP, MOBA_KEYS, LANES), lambda b, g, u: (b, g, u, 0)),
                  pl.BlockSpec((1, 1, seq, LANES), lambda b, g, u: (b, g, 0, 0)),
                  pl.BlockSpec((1, 1, seq, LANES), lambda b, g, u: (b, g, 0, 0))],
        out_specs=pl.BlockSpec((1, MOBA_KEYS, GROUP * HEAD_DIM), lambda b, g, u: (b, u, g)),
        out_shape=jax.ShapeDtypeStruct((batch, seq, N_HEADS * HEAD_DIM), bf16),
        scratch_shapes=[pltpu.VMEM((nb, LANES), f32),
                        pltpu.VMEM((rows, LANES), bf16),
                        pltpu.VMEM((rows, MOBA_KEYS), f32),
                        pltpu.VMEM((rows, MOBA_KEYS), f32),
                        pltpu.VMEM((rows, 1), f32),
                        pltpu.VMEM((rows, LANES), f32)],
        compiler_params=_cparams("parallel", "parallel", "arbitrary"),
        name="moba_attention",
    )(q, k, v)


def _swa_kernel(sink_ref, q_ref, kc_ref, kp_ref, vc_ref, vp_ref, o_ref):
    g = pl.program_id(1)
    i = pl.program_id(2)
    w = SWA_WINDOW
    rows = GROUP * w
    nt_dims = (((1,), (1,)), ((), ()))
    n_sb = q_ref.shape[2] // w
    t_in = lax.broadcasted_iota(jnp.int32, (rows, 2 * w), 0) % w
    col = lax.broadcasted_iota(jnp.int32, (rows, 2 * w), 1)
    band = jnp.logical_and(col > t_in, col <= t_in + w)
    head_of_row = lax.broadcasted_iota(jnp.int32, (rows, 1), 0) // w
    sink = jnp.zeros((rows, 1), f32)
    for hh in range(GROUP):
        sink = jnp.where(head_of_row == hh, sink_ref[g * GROUP + hh] * LOG2E, sink)

    def window(ref_cur, ref_prev, sb):
        if sb == 0:
            return jnp.concatenate([ref_prev[0, 0], ref_cur[0, 0, 0:w, :]], axis=0)
        return ref_cur[0, 0, (sb - 1) * w:(sb + 1) * w, :]

    scores = [lax.dot_general(q_ref[0, :, sb * w:(sb + 1) * w, :].reshape(rows, LANES), window(kc_ref, kp_ref, sb),
                              nt_dims, preferred_element_type=f32) for sb in range(n_sb)]
    for sb in range(n_sb):
        ok = band
        if sb == 0:
            ok = jnp.logical_and(band, jnp.logical_or(col >= w, i > 0))
        s = jnp.where(ok, scores[sb], NEG)
        m = jnp.maximum(jnp.max(s, axis=1, keepdims=True), sink)
        p = jnp.exp2(s - m).astype(bf16)
        acc = jnp.dot(p, window(vc_ref, vp_ref, sb), preferred_element_type=f32)
        out = acc / (acc[:, HEAD_DIM:HEAD_DIM + 1] + jnp.exp2(sink - m))
        lane = lax.broadcasted_iota(jnp.int32, (w, LANES), 1)
        low = lane < HEAD_DIM
        for c in range(GROUP // 2):
            even = out[(2 * c) * w:(2 * c + 1) * w]
            odd = out[(2 * c + 1) * w:(2 * c + 2) * w]
            o_ref[0, sb * w:(sb + 1) * w, c * LANES:(c + 1) * LANES] = jnp.where(
                low, even, pltpu.roll(odd, HEAD_DIM, 1)).astype(o_ref.dtype)


def _swa(q, k, v, sinks):
    batch, _, seq, _ = q.shape
    per = TQ_SWA // SWA_WINDOW
    cur = lambda b, g, i, s: (b, g, i, 0)
    prev = lambda b, g, i, s: (b, g, jnp.maximum(i * per - 1, 0), 0)
    grid_spec = pltpu.PrefetchScalarGridSpec(
        num_scalar_prefetch=1,
        grid=(batch, N_KV_HEADS, seq // TQ_SWA),
        in_specs=[pl.BlockSpec((1, GROUP, TQ_SWA, LANES), cur),
                  pl.BlockSpec((1, 1, TQ_SWA, LANES), cur),
                  pl.BlockSpec((1, 1, SWA_WINDOW, LANES), prev),
                  pl.BlockSpec((1, 1, TQ_SWA, LANES), cur),
                  pl.BlockSpec((1, 1, SWA_WINDOW, LANES), prev)],
        out_specs=pl.BlockSpec((1, TQ_SWA, GROUP * HEAD_DIM), lambda b, g, i, s: (b, i, g)),
    )
    return pl.pallas_call(
        _swa_kernel,
        grid_spec=grid_spec,
        out_shape=jax.ShapeDtypeStruct((batch, seq, N_HEADS * HEAD_DIM), bf16),
        compiler_params=_cparams("parallel", "parallel", "arbitrary"),
        name="swa_attention",
    )(sinks, q, k, k, v, v)


def _layer_norm(x, g, b):
    mu = jnp.mean(x, axis=-1, keepdims=True)
    xc = x - mu
    var = jnp.mean(xc * xc, axis=-1, keepdims=True)
    return xc * lax.rsqrt(var + LN_EPS) * g + b


def _load_rows(ref, n_rows):
    return jnp.concatenate([ref[pl.ds(k, n_rows, stride=ROW_TILE), :] for k in range(ROW_TILE)], axis=1)


def _store_rows(ref, val):
    for k in range(ROW_TILE):
        ref[pl.ds(k, val.shape[0], stride=ROW_TILE), :] = val[:, k * LANES:(k + 1) * LANES]


def _mix_router_kernel(x_ref, a_ref, wo_ref, g_ref, b_ref, wr_ref, x1_ref, meta_ref, cnt_ref):
    tm = x_ref.shape[0]
    y = ALPHA * x_ref[...] + jnp.dot(a_ref[...], wo_ref[...], preferred_element_type=f32)
    x1 = _layer_norm(y, g_ref[...], b_ref[...])
    _store_rows(x1_ref, x1)

    @pl.when(pl.program_id(0) == 0)
    def _():
        cnt_ref[...] = jnp.zeros_like(cnt_ref)

    logits = jnp.dot(x1, wr_ref[...], preferred_element_type=f32)
    lane = lax.broadcasted_iota(jnp.int32, (tm, LANES), 1)
    lg = jnp.where(lane < N_EXPERTS, logits, -jnp.inf)
    m1 = jnp.max(lg, axis=1, keepdims=True)
    i1 = jnp.min(jnp.where(lg == m1, lane, LANES), axis=1, keepdims=True)
    lg2 = jnp.where(lane == i1, -jnp.inf, lg)
    m2 = jnp.max(lg2, axis=1, keepdims=True)
    i2 = jnp.min(jnp.where(lg2 == m2, lane, LANES), axis=1, keepdims=True)
    e2 = jnp.exp(m2 - m1)
    w1 = 1.0 / (1.0 + e2)
    w2 = e2 / (1.0 + e2)
    hit1 = lane == i1
    hit2 = lane == i2
    hits = jnp.where(jnp.logical_or(hit1, hit2), 1.0, 0.0)
    r_i = lax.broadcasted_iota(jnp.int32, (tm, tm), 0)
    c_i = lax.broadcasted_iota(jnp.int32, (tm, tm), 1)
    before = jnp.where(c_i < r_i, 1.0, 0.0).astype(bf16)
    cum = jnp.dot(before, hits.astype(bf16), preferred_element_type=f32) + cnt_ref[0:1, :]
    r1 = jnp.sum(jnp.where(hit1, cum, 0.0), axis=1, keepdims=True)
    r2 = jnp.sum(jnp.where(hit2, cum, 0.0), axis=1, keepdims=True)
    cnt_ref[...] = cnt_ref[...] + jnp.sum(hits, axis=0, keepdims=True)
    meta = jnp.where(lane == 0, i1.astype(f32), 0.0)
    meta = jnp.where(lane == 1, i2.astype(f32), meta)
    meta = jnp.where(lane == 2, w1, meta)
    meta = jnp.where(lane == 3, w2, meta)
    meta = jnp.where(lane == 4, r1, meta)
    meta = jnp.where(lane == 5, r2, meta)
    meta_ref[...] = meta


def _mix_router(x2d, a2d, w_o, g, b, w_router):
    n = x2d.shape[0]
    tok = lambda i: (i, 0)
    const = lambda i: (0, 0)
    wr = jnp.zeros((D_MODEL, LANES), f32).at[:, :N_EXPERTS].set(w_router)
    return pl.pallas_call(
        _mix_router_kernel, grid=(n // TM_MIX,),
        in_specs=[pl.BlockSpec((TM_MIX, D_MODEL), tok),
                  pl.BlockSpec((TM_MIX, D_MODEL), tok),
                  pl.BlockSpec((D_MODEL, D_MODEL), const),
                  pl.BlockSpec((1, D_MODEL), const),
                  pl.BlockSpec((1, D_MODEL), const),
                  pl.BlockSpec((D_MODEL, LANES), const)],
        out_specs=[pl.BlockSpec((TM_MIX * ROW_TILE, LANES), tok),
                   pl.BlockSpec((TM_MIX, LANES), tok), pl.BlockSpec((8, LANES), const)],
        out_shape=[jax.ShapeDtypeStruct((n * ROW_TILE, LANES), f32),
                   jax.ShapeDtypeStruct((n, LANES), f32), jax.ShapeDtypeStruct((8, LANES), f32)],
        compiler_params=_cparams("arbitrary"), name="mix_ln_router",
    )(x2d, a2d, w_o, g, b, wr)


def _swiglu_chunk(xb, wg, wu, wd):
    gate = jnp.dot(xb, wg.astype(bf16), preferred_element_type=f32)
    up = jnp.dot(xb, wu.astype(bf16), preferred_element_type=f32)
    h = (gate / (1.0 + jnp.exp(-gate)) * up).astype(bf16)
    return jnp.dot(h, wd.astype(bf16), preferred_element_type=f32)


def _accumulate(acc_ref, part, fc):
    @pl.when(fc == 0)
    def _():
        acc_ref[...] = part

    @pl.when(fc > 0)
    def _():
        acc_ref[...] = acc_ref[...] + part


def _moe_ffn_kernel(te_ref, nact_ref, x_ref, wg_ref, wu_ref, wd_ref, y_ref, xb_ref, acc_ref):
    n = pl.program_id(0)
    fc = pl.program_id(1)
    active = n < nact_ref[0]

    @pl.when(jnp.logical_and(jnp.logical_not(active), fc == 0))
    def _():
        y_ref[...] = jnp.zeros_like(y_ref)

    @pl.when(active)
    def _():
        @pl.when(fc == 0)
        def _():
            xb_ref[...] = _load_rows(x_ref, TM_FFN).astype(bf16)

        _accumulate(acc_ref, _swiglu_chunk(xb_ref[...], wg_ref[0], wu_ref[0], wd_ref[0]), fc)

        @pl.when(fc == pl.num_programs(1) - 1)
        def _():
            _store_rows(y_ref, acc_ref[...])


def _moe_ffn(xs_rows, w_gate, w_up, w_down, tile_expert, n_active):
    blk = (TM_FFN * ROW_TILE, LANES)
    n_tiles = xs_rows.shape[0] // blk[0]
    n_fc = D_FF // FC_FFN

    def row(n, f, te, na):
        return jnp.minimum(n, na[0] - 1), 0

    def w_in(n, f, te, na):
        return te[jnp.minimum(n, na[0] - 1)], 0, jnp.where(n < na[0], f, n_fc - 1)

    def w_out(n, f, te, na):
        return te[jnp.minimum(n, na[0] - 1)], jnp.where(n < na[0], f, n_fc - 1), 0

    grid_spec = pltpu.PrefetchScalarGridSpec(
        num_scalar_prefetch=2,
        grid=(n_tiles, n_fc),
        in_specs=[pl.BlockSpec(blk, row),
                  pl.BlockSpec((1, D_MODEL, FC_FFN), w_in),
                  pl.BlockSpec((1, D_MODEL, FC_FFN), w_in),
                  pl.BlockSpec((1, FC_FFN, D_MODEL), w_out)],
        out_specs=pl.BlockSpec(blk, lambda n, f, te, na: (n, 0)),
        scratch_shapes=[pltpu.VMEM((TM_FFN, D_MODEL), bf16), pltpu.VMEM((TM_FFN, D_MODEL), f32)],
    )
    return pl.pallas_call(
        _moe_ffn_kernel, grid_spec=grid_spec,
        out_shape=jax.ShapeDtypeStruct(xs_rows.shape, f32),
        compiler_params=_cparams("arbitrary", "arbitrary"), name="moe_swiglu",
    )(tile_expert, n_active, xs_rows, w_gate, w_up, w_down)


def _dense_tail_kernel(x_ref, a_ref, wo_ref, g1_ref, b1_ref, wg_ref, wu_ref, wd_ref, g2_ref, b2_ref,
                       p_ref, wp_ref, wgate_ref, o_ref, x1_ref, xb_ref, acc_ref):
    fc = pl.program_id(1)

    @pl.when(fc == 0)
    def _():
        y = ALPHA * x_ref[...] + jnp.dot(a_ref[...], wo_ref[...], preferred_element_type=f32)
        x1 = _layer_norm(y, g1_ref[...], b1_ref[...])
        x1_ref[...] = x1
        xb_ref[...] = x1.astype(bf16)

    _accumulate(acc_ref, _swiglu_chunk(xb_ref[...], wg_ref[...], wu_ref[...], wd_ref[...]), fc)

    @pl.when(fc == pl.num_programs(1) - 1)
    def _():
        x2 = _layer_norm(ALPHA * x1_ref[...] + acc_ref[...], g2_ref[...], b2_ref[...])
        _ple(x2, p_ref, wp_ref, wgate_ref, o_ref)


def _dense_tail(x2d, a2d, w_o, g1, b1, w_gate, w_up, w_down, g2, b2, p2d, w_proj, w_pgate):
    n = x2d.shape[0]
    tok = lambda i, f: (i, 0)
    const = lambda i, f: (0, 0)
    return pl.pallas_call(
        _dense_tail_kernel, grid=(n // TM_DENSE, D_FF // FC_FFN),
        in_specs=[pl.BlockSpec((TM_DENSE, D_MODEL), tok),
                  pl.BlockSpec((TM_DENSE, D_MODEL), tok),
                  pl.BlockSpec((D_MODEL, D_MODEL), const),
                  pl.BlockSpec((1, D_MODEL), const),
                  pl.BlockSpec((1, D_MODEL), const),
                  pl.BlockSpec((D_MODEL, FC_FFN), lambda i, f: (0, f)),
                  pl.BlockSpec((D_MODEL, FC_FFN), lambda i, f: (0, f)),
                  pl.BlockSpec((FC_FFN, D_MODEL), lambda i, f: (f, 0)),
                  pl.BlockSpec((1, D_MODEL), const),
                  pl.BlockSpec((1, D_MODEL), const),
                  pl.BlockSpec((TM_DENSE, PLE_DIM), tok),
                  pl.BlockSpec((PLE_DIM, D_MODEL), const),
                  pl.BlockSpec((D_MODEL, D_MODEL), const)],
        out_specs=pl.BlockSpec((TM_DENSE, D_MODEL), tok),
        out_shape=jax.ShapeDtypeStruct((n, D_MODEL), f32),
        scratch_shapes=[pltpu.VMEM((TM_DENSE, D_MODEL), f32), pltpu.VMEM((TM_DENSE, D_MODEL), bf16),
                        pltpu.VMEM((TM_DENSE, D_MODEL), f32)],
        compiler_params=_cparams("parallel", "arbitrary"), name="dense_tail",
    )(x2d, a2d, w_o, g1, b1, w_gate, w_up, w_down, g2, b2, p2d, w_proj, w_pgate)


def _dispatch_kernel(pos_ref, x_ref, init_ref, xs_ref, sem):
    del init_ref
    tm = x_ref.shape[0] // ROW_TILE

    def start(r, _):
        src = x_ref.at[pl.ds(pl.multiple_of(r * ROW_TILE, ROW_TILE), ROW_TILE)]
        for k in range(2):
            dst = pl.multiple_of(pos_ref[2 * r + k] * ROW_TILE, ROW_TILE)
            pltpu.make_async_copy(src, xs_ref.at[pl.ds(dst, ROW_TILE)], sem).start()
        return 0

    lax.fori_loop(0, tm, start, 0, unroll=DMA_UNROLL)
    for _ in range(2):
        pltpu.make_async_copy(x_ref, xs_ref.at[pl.ds(0, tm * ROW_TILE)], sem).wait()


def _dispatch(x1_rows, pos_flat, n_rows):
    n = x1_rows.shape[0] // ROW_TILE
    init = jnp.zeros((n_rows * ROW_TILE, LANES), f32)
    return pl.pallas_call(
        _dispatch_kernel, grid=(n // TM_DISPATCH,),
        in_specs=[pl.BlockSpec((2 * TM_DISPATCH,), lambda i: (i,), memory_space=pltpu.SMEM),
                  pl.BlockSpec((TM_DISPATCH * ROW_TILE, LANES), lambda i: (i, 0)),
                  pl.BlockSpec(memory_space=pl.ANY)],
        out_specs=pl.BlockSpec(memory_space=pl.ANY),
        out_shape=jax.ShapeDtypeStruct((n_rows * ROW_TILE, LANES), f32),
        scratch_shapes=[pltpu.SemaphoreType.DMA(())],
        input_output_aliases={2: 0},
        compiler_params=_cparams("arbitrary"), name="moe_dispatch",
    )(pos_flat, x1_rows, init)


def _ple(x2, p_ref, wp_ref, wgate_ref, o_ref):
    pe = jnp.dot(p_ref[...].astype(bf16), wp_ref[...], preferred_element_type=f32)
    z = jnp.dot(x2.astype(bf16), wgate_ref[...], preferred_element_type=f32)
    o_ref[...] = x2 + pe / (1.0 + jnp.exp(-z))


def _out_moe_kernel(pos_ref, nxt_ref, x1_ref, meta_ref, y_ref, g_ref, b_ref, p_ref, wp_ref, wgate_ref, o_ref,
                    buf_ref, sems):
    i = pl.program_id(0)
    tm = x1_ref.shape[0] // ROW_TILE
    slot = i % 2

    def gather(idx_ref, to_slot):
        def start(r, _):
            dst = pl.ds(pl.multiple_of(r * ROW_TILE, ROW_TILE), ROW_TILE)
            for k in range(2):
                src = pl.multiple_of(idx_ref[2 * r + k] * ROW_TILE, ROW_TILE)
                pltpu.make_async_copy(y_ref.at[pl.ds(src, ROW_TILE)], buf_ref.at[to_slot, k, dst],
                                      sems.at[to_slot]).start()
            return 0

        lax.fori_loop(0, tm, start, 0, unroll=DMA_UNROLL)

    @pl.when(i == 0)
    def _():
        gather(pos_ref, slot)

    @pl.when(i + 1 < pl.num_programs(0))
    def _():
        gather(nxt_ref, 1 - slot)

    for k in range(2):
        pltpu.make_async_copy(y_ref.at[pl.ds(0, tm * ROW_TILE)], buf_ref.at[slot, k], sems.at[slot]).wait()
    meta = meta_ref[...]
    f = meta[:, 2:3] * _load_rows(buf_ref.at[slot, 0], tm) + meta[:, 3:4] * _load_rows(buf_ref.at[slot, 1], tm)
    x2 = _layer_norm(ALPHA * _load_rows(x1_ref, tm) + f, g_ref[...], b_ref[...])
    _ple(x2, p_ref, wp_ref, wgate_ref, o_ref)


def _out_moe(x1_rows, meta, y_rows, pos_flat, g, b, p2d, w_proj, w_gate):
    n = x1_rows.shape[0] // ROW_TILE
    steps = n // TM_OUT
    tok = lambda i: (i, 0)
    const = lambda i: (0, 0)
    return pl.pallas_call(
        _out_moe_kernel, grid=(steps,),
        in_specs=[pl.BlockSpec((2 * TM_OUT,), lambda i: (i,), memory_space=pltpu.SMEM),
                  pl.BlockSpec((2 * TM_OUT,), lambda i: (jnp.minimum(i + 1, steps - 1),), memory_space=pltpu.SMEM),
                  pl.BlockSpec((TM_OUT * ROW_TILE, LANES), tok),
                  pl.BlockSpec((TM_OUT, LANES), tok),
                  pl.BlockSpec(memory_space=pl.ANY),
                  pl.BlockSpec((1, D_MODEL), const),
                  pl.BlockSpec((1, D_MODEL), const),
                  pl.BlockSpec((TM_OUT, PLE_DIM), tok),
                  pl.BlockSpec((PLE_DIM, D_MODEL), const),
                  pl.BlockSpec((D_MODEL, D_MODEL), const)],
        out_specs=pl.BlockSpec((TM_OUT, D_MODEL), tok),
        out_shape=jax.ShapeDtypeStruct((n, D_MODEL), f32),
        scratch_shapes=[pltpu.VMEM((2, 2, TM_OUT * ROW_TILE, LANES), f32), pltpu.SemaphoreType.DMA((2,))],
        compiler_params=_cparams("arbitrary"), name="moe_combine_ln_ple",
    )(pos_flat, pos_flat, x1_rows, meta, y_rows, g, b, p2d, w_proj, w_gate)


def _routing_plan(meta, counts_row, n_tiles):
    i1 = meta[:, 0].astype(jnp.int32)
    i2 = meta[:, 1].astype(jnp.int32)
    r1 = meta[:, 4].astype(jnp.int32)
    r2 = meta[:, 5].astype(jnp.int32)
    counts = counts_row[0, :N_EXPERTS].astype(jnp.int32)
    tiles = (counts + TM_FFN - 1) // TM_FFN
    tile_end = jnp.cumsum(tiles)
    offset = (tile_end - tiles) * TM_FFN
    pos = jnp.stack([offset[i1] + r1, offset[i2] + r2], axis=1).reshape(-1)
    tile_expert = jnp.sum(jnp.arange(n_tiles)[:, None] >= tile_end[None, :], axis=1)
    tile_expert = jnp.minimum(tile_expert, N_EXPERTS - 1)
    return pos.astype(jnp.int32), tile_expert.astype(jnp.int32), tile_end[-1:].astype(jnp.int32)


def kernel(x, p, positions, w_qkv, w_o, ln_mix_g, ln_mix_b, ln_ffn_g, ln_ffn_b, sinks, w_ffn_gate, w_ffn_up,
           w_ffn_down, w_router, w_exp_gate, w_exp_up, w_exp_down, w_ple_proj, w_ple_gate):
    batch, seq, _ = x.shape
    n = batch * seq
    assert seq % TQ_SWA == 0 and (seq // MOBA_BLOCK) % 8 == 0 and seq // MOBA_BLOCK <= LANES - HEAD_DIM
    tabs = _rope_tables(positions)
    x2d = x.reshape(n, D_MODEL)
    row = lambda v: v.reshape(1, D_MODEL)
    for i in range(DEPTH):
        j = i // 2
        q, k, v = _qkv(x2d, w_qkv[i].astype(bf16), tabs, batch, seq)
        if i % 2 == 0:
            a = _moba(q, k, v)
        else:
            a = _swa(q, k, v, sinks[j])
        a2d = a.reshape(n, D_MODEL)
        wo = w_o[i].astype(bf16)
        ple_args = (row(ln_ffn_g[i]), row(ln_ffn_b[i]), p[i].reshape(n, PLE_DIM),
                    w_ple_proj[i].astype(bf16), w_ple_gate[i].astype(bf16))
        if i % 2 == 0:
            x2d = _dense_tail(x2d, a2d, wo, row(ln_mix_g[i]), row(ln_mix_b[i]), w_ffn_gate[j].astype(bf16),
                              w_ffn_up[j].astype(bf16), w_ffn_down[j].astype(bf16), *ple_args)
        else:
            x1, meta, counts = _mix_router(x2d, a2d, wo, row(ln_mix_g[i]), row(ln_mix_b[i]), w_router[j])
            n_tiles = 2 * n // TM_FFN + N_EXPERTS
            pos, tile_expert, n_active = _routing_plan(meta, counts, n_tiles)
            xs = _dispatch(x1, pos, n_tiles * TM_FFN)
            y = _moe_ffn(xs, w_exp_gate[j], w_exp_up[j], w_exp_down[j], tile_expert, n_active)
            x2d = _out_moe(x1, meta, y, pos, *ple_args)
    return x2d.reshape(batch, seq, D_MODEL)
```

```python
import functools

import jax
import jax.numpy as jnp
import numpy as np
from jax import lax
from jax.experimental import pallas as pl
from jax.experimental.pallas import tpu as pltpu

D_MODEL = 1024
N_HEADS = 16
N_KV_HEADS = 4
HEAD_DIM = 64
GROUP = N_HEADS // N_KV_HEADS
QKV_DIM = (N_HEADS + 2 * N_KV_HEADS) * HEAD_DIM
ROT_DIM = 16
ROPE_THETA = 500000.0
MOBA_BLOCK = 256
MOBA_TOPK = 3
SWA_WINDOW = 128
D_FF = 3584
N_EXPERTS = 8
PLE_DIM = 256
LN_EPS = 1e-5
DEPTH = 2
ALPHA = (2.0 * DEPTH) ** 0.25
LOG2E = 1.4426950408889634
Q_SCALE = HEAD_DIM ** -0.5 * LOG2E

LANES = 128
ROW_TILE = D_MODEL // LANES
DMA_UNROLL = 8
NEG = -1e30
VMEM_LIMIT = 48 * 1024 * 1024

TM_QKV = 512
TM_MIX = 512
TM_DENSE = 1024
TM_FFN = 1024
FC_FFN = 512
TM_DISPATCH = 512
TM_OUT = 256
TQ_SWA = 512
MOBA_SUB = 512
MOBA_KEYS = 2 * MOBA_BLOCK
MOBA_KV_PER_STEP = 2
MOBA_VMEM_LIMIT = 56 * 1024 * 1024

f32 = jnp.float32
bf16 = jnp.bfloat16


def _cparams(*sem):
    return pltpu.CompilerParams(dimension_semantics=sem, vmem_limit_bytes=VMEM_LIMIT)


def _rope_kernel(pos_ref, inv_ref, cos_ref, sin_ref):
    ang = pos_ref[...].astype(f32) * inv_ref[...]
    cos_ref[...] = jnp.cos(ang)
    sin_ref[...] = jnp.sin(ang)


def _rope_tables(positions):
    n = positions.size
    half = ROT_DIM // 2
    inv = 1.0 / (ROPE_THETA ** (jnp.arange(0, ROT_DIM, 2, dtype=f32) / ROT_DIM))
    rows = n * half // LANES
    pos_rep = jnp.repeat(positions.reshape(-1), half).reshape(rows, LANES)
    inv_rep = jnp.tile(inv, LANES // half).reshape(1, LANES)
    cos, sin = pl.pallas_call(
        _rope_kernel,
        out_shape=(jax.ShapeDtypeStruct((rows, LANES), f32),) * 2,
        name="rope_tables",
    )(pos_rep, inv_rep)
    cos = cos.reshape(n, half)
    sin = sin.reshape(n, half)
    one = jnp.ones((n, HEAD_DIM - ROT_DIM), f32)
    zero = jnp.zeros((n, HEAD_DIM - ROT_DIM), f32)
    zh = jnp.zeros((n, half), f32)
    c_tab = jnp.tile(jnp.concatenate([cos, cos, one], axis=1), (1, 2))
    s_lo = jnp.tile(jnp.concatenate([-sin, zh, zero], axis=1), (1, 2))
    s_hi = jnp.tile(jnp.concatenate([zh, sin, zero], axis=1), (1, 2))
    return c_tab, s_lo, s_hi


def _qkv_kernel(x_ref, w_ref, c_ref, slo_ref, shi_ref, q_ref, k_ref, v_ref, *, tiles_per_seq):
    tm = x_ref.shape[0]
    acc = jnp.dot(x_ref[...].astype(bf16), w_ref[...], preferred_element_type=f32)
    lane = lax.broadcasted_iota(jnp.int32, (tm, LANES), 1)
    row = lax.broadcasted_iota(jnp.int32, (tm, LANES), 0)
    low = lane < HEAD_DIM
    c_tab, s_lo, s_hi = c_ref[...], slo_ref[...], shi_ref[...]
    seq0 = (pl.program_id(0) % tiles_per_seq) * tm
    blk = (seq0 + row) // MOBA_BLOCK
    k_pad = jnp.where(lane == HEAD_DIM + blk, 1.0, 0.0)
    v_pad = jnp.where(lane == HEAD_DIM, 1.0, 0.0)
    n_q = N_HEADS // 2
    n_kv = N_KV_HEADS // 2
    for c in range(n_q + n_kv):
        xc = acc[:, c * LANES:(c + 1) * LANES]
        r = xc * c_tab + pltpu.roll(xc, LANES - ROT_DIM // 2, 1) * s_lo + pltpu.roll(xc, ROT_DIM // 2, 1) * s_hi
        r_odd = pltpu.roll(r, HEAD_DIM, 1)
        if c < n_q:
            q_ref[0, 2 * c] = jnp.where(low, r * Q_SCALE, 0.0).astype(bf16)
            q_ref[0, 2 * c + 1] = jnp.where(low, r_odd * Q_SCALE, 0.0).astype(bf16)
        else:
            k_ref[0, 2 * (c - n_q)] = jnp.where(low, r, k_pad).astype(bf16)
            k_ref[0, 2 * (c - n_q) + 1] = jnp.where(low, r_odd, k_pad).astype(bf16)
    for c in range(n_kv):
        xc = acc[:, (n_q + n_kv + c) * LANES:(n_q + n_kv + c + 1) * LANES]
        v_ref[0, 2 * c] = jnp.where(low, xc, v_pad).astype(bf16)
        v_ref[0, 2 * c + 1] = jnp.where(low, pltpu.roll(xc, HEAD_DIM, 1), v_pad).astype(bf16)


def _qkv(x2d, w_qkv, tabs, batch, seq):
    n = x2d.shape[0]
    nt = seq // TM_QKV
    tok = lambda i: (i, 0)
    head = lambda i: (i // nt, 0, i % nt, 0)
    return pl.pallas_call(
        functools.partial(_qkv_kernel, tiles_per_seq=nt),
        grid=(n // TM_QKV,),
        in_specs=[pl.BlockSpec((TM_QKV, D_MODEL), tok),
                  pl.BlockSpec((D_MODEL, QKV_DIM), lambda i: (0, 0)),
                  pl.BlockSpec((TM_QKV, LANES), tok),
                  pl.BlockSpec((TM_QKV, LANES), tok),
                  pl.BlockSpec((TM_QKV, LANES), tok)],
        out_specs=[pl.BlockSpec((1, N_HEADS, TM_QKV, LANES), head),
                   pl.BlockSpec((1, N_KV_HEADS, TM_QKV, LANES), head),
                   pl.BlockSpec((1, N_KV_HEADS, TM_QKV, LANES), head)],
        out_shape=[jax.ShapeDtypeStruct((batch, N_HEADS, seq, LANES), bf16),
                   jax.ShapeDtypeStruct((batch, N_KV_HEADS, seq, LANES), bf16),
                   jax.ShapeDtypeStruct((batch, N_KV_HEADS, seq, LANES), bf16)],
        compiler_params=_cparams("parallel"),
        name="qkv_rope",
    )(x2d, w_qkv, *tabs)


def _merge_heads(o, out_ref, rows):
    lane = lax.broadcasted_iota(jnp.int32, (rows, LANES), 1)
    low = lane < HEAD_DIM
    for c in range(o.shape[0] // rows // 2):
        even = o[(2 * c) * rows:(2 * c + 1) * rows]
        odd = o[(2 * c + 1) * rows:(2 * c + 2) * rows]
        out_ref[0, :, c * LANES:(c + 1) * LANES] = jnp.where(low, even, pltpu.roll(odd, HEAD_DIM, 1)).astype(out_ref.dtype)


def _moba_kernel(q_ref, k_ref, v_ref, o_ref, kmean_ref, qa_ref, s0_ref, s1_ref, m_ref, acc_ref):
    u = pl.program_id(2)
    blk = MOBA_BLOCK
    tq = q_ref.shape[2]
    nb = k_ref.shape[2] // blk
    n_grp = k_ref.shape[1]
    grp_rows = GROUP * tq
    rows = n_grp * grp_rows
    sub = MOBA_SUB
    n_sub = rows // sub
    nt_dims = (((1,), (1,)), ((), ()))

    @pl.when(u == 0)
    def _():
        for kv in range(n_grp):
            for j in range(nb):
                kj = k_ref[0, kv, j * blk:(j + 1) * blk, :].astype(f32)
                kmean_ref[kv, j:j + 1, :] = jnp.sum(kj, axis=0, keepdims=True) * (1.0 / blk)

    for kv in range(n_grp):
        q = q_ref[0, kv * GROUP:(kv + 1) * GROUP].reshape(grp_rows, LANES)
        km = kmean_ref[kv]
        km_hi = km.astype(bf16)
        km_lo = (km - km_hi.astype(f32)).astype(bf16)
        g = (lax.dot_general(km_hi, q, nt_dims, preferred_element_type=f32)
             + lax.dot_general(km_lo, q, nt_dims, preferred_element_type=f32))
        jidx = lax.broadcasted_iota(jnp.int32, (nb, grp_rows), 0)
        q_blk = u * (tq // blk) + (lax.broadcasted_iota(jnp.int32, (nb, grp_rows), 1) % tq) // blk
        past = jidx < q_blk
        g = jnp.where(past, g, -jnp.inf)
        sel = jidx == q_blk
        for _ in range(MOBA_TOPK):
            mx = jnp.max(g, axis=0, keepdims=True)
            first = jnp.min(jnp.where(g == mx, jidx, nb), axis=0, keepdims=True)
            pick = jidx == first
            sel = jnp.logical_or(sel, jnp.logical_and(pick, past))
            g = jnp.where(pick, -jnp.inf, g)
        bias = jnp.where(sel, 0.0, NEG)
        bias_t = jnp.concatenate([jnp.zeros((HEAD_DIM, grp_rows), f32), bias,
                                  jnp.zeros((LANES - HEAD_DIM - nb, grp_rows), f32)], axis=0)
        qa_ref[kv * grp_rows:(kv + 1) * grp_rows] = (q.astype(f32) + bias_t.T).astype(bf16)

    m_ref[...] = jnp.full_like(m_ref, NEG)
    acc_ref[...] = jnp.zeros_like(acc_ref)

    def scores(c, s_ref, r):
        o = pl.multiple_of(c * MOBA_KEYS, MOBA_KEYS)
        rs = slice(r * sub, (r + 1) * sub)
        kv = r * sub // grp_rows
        s_ref[rs] = lax.dot_general(qa_ref[rs], k_ref[0, kv, pl.ds(o, MOBA_KEYS), :], nt_dims,
                                    preferred_element_type=f32)

    def accumulate(c, s_ref, r, causal):
        o = pl.multiple_of(c * MOBA_KEYS, MOBA_KEYS)
        rs = slice(r * sub, (r + 1) * sub)
        kv = r * sub // grp_rows
        s = s_ref[rs]
        if causal:
            col = lax.broadcasted_iota(jnp.int32, (sub, MOBA_KEYS), 1)
            tok = (lax.broadcasted_iota(jnp.int32, (sub, MOBA_KEYS), 0) + r * sub) % tq
            s = jnp.where(jnp.logical_and(col // blk == tok // blk, col > tok), NEG, s)
        m_old = m_ref[rs]
        m_new = jnp.maximum(m_old, jnp.max(s, axis=1, keepdims=True))
        p = jnp.exp2(s - m_new).astype(bf16)
        pv = jnp.dot(p, v_ref[0, kv, pl.ds(o, MOBA_KEYS), :], preferred_element_type=f32)
        acc_ref[rs] = acc_ref[rs] * jnp.exp2(m_old - m_new) + pv
        m_ref[rs] = m_new

    def step(c_next, s_next, c_cur, s_cur, causal=False):
        for r in range(n_sub):
            if c_next is not None:
                scores(c_next, s_next, r)
            accumulate(c_cur, s_cur, r, causal)

    for r in range(n_sub):
        scores(0, s0_ref, r)

    def body(t, carry):
        step(2 * t + 1, s1_ref, 2 * t, s0_ref)
        step(2 * t + 2, s0_ref, 2 * t + 1, s1_ref)
        return carry

    lax.fori_loop(0, u // 2, body, 0)

    @pl.when(u % 2 == 1)
    def _():
        step(u, s1_ref, u - 1, s0_ref)
        step(None, None, u, s1_ref, causal=True)

    @pl.when(u % 2 == 0)
    def _():
        step(None, None, u, s0_ref, causal=True)

    acc = acc_ref[...]
    _merge_heads(acc / acc[:, HEAD_DIM:HEAD_DIM + 1], o_ref, tq)


def _moba(q, k, v):
    batch, _, seq, _ = q.shape
    nb = seq // MOBA_BLOCK
    n_grp = MOBA_KV_PER_STEP
    rows = n_grp * GROUP * MOBA_KEYS
    return pl.pallas_call(
        _moba_kernel,
        grid=(batch, N_KV_HEADS // n_grp, seq // MOBA_KEYS),
        in_specs=[pl.BlockSpec((1, n_grp * GROUP, MOBA_KEYS, LANES), lambda b, g, u: (b, g, u, 0)),
                  pl.BlockSpec((1, n_grp, seq, LANES), lambda b, g, u: (b, g, 0, 0)),
                  pl.BlockSpec((1, n_grp, seq, LANES), lambda b, g, u: (b, g, 0, 0))],
        out_specs=pl.BlockSpec((1, MOBA_KEYS, n_grp * GROUP * HEAD_DIM), lambda b, g, u: (b, u, g)),
        out_shape=jax.ShapeDtypeStruct((batch, seq, N_HEADS * HEAD_DIM), bf16),
        scratch_shapes=[pltpu.VMEM((n_grp, nb, LANES), f32),
                        pltpu.VMEM((rows, LANES), bf16),
                        pltpu.VMEM((rows, MOBA_KEYS), f32),
                        pltpu.VMEM((rows, MOBA_KEYS), f32),
                        pltpu.VMEM((rows, 1), f32),
                        pltpu.VMEM((rows, LANES), f32)],
        compiler_params=pltpu.CompilerParams(dimension_semantics=("parallel", "parallel", "arbitrary"),
                                             vmem_limit_bytes=MOBA_VMEM_LIMIT),
        name="moba_attention",
    )(q, k, v)


def _swa_kernel(sink_ref, q_ref, kc_ref, kp_ref, vc_ref, vp_ref, o_ref):
    g = pl.program_id(1)
    i = pl.program_id(2)
    w = SWA_WINDOW
    rows = GROUP * w
    nt_dims = (((1,), (1,)), ((), ()))
    n_sb = q_ref.shape[2] // w
    t_in = lax.broadcasted_iota(jnp.int32, (rows, 2 * w), 0) % w
    col = lax.broadcasted_iota(jnp.int32, (rows, 2 * w), 1)
    band = jnp.logical_and(col > t_in, col <= t_in + w)
    head_of_row = lax.broadcasted_iota(jnp.int32, (rows, 1), 0) // w
    sink = jnp.zeros((rows, 1), f32)
    for hh in range(GROUP):
        sink = jnp.where(head_of_row == hh, sink_ref[g * GROUP + hh] * LOG2E, sink)

    def window(ref_cur, ref_prev, sb):
        if sb == 0:
            return jnp.concatenate([ref_prev[0, 0], ref_cur[0, 0, 0:w, :]], axis=0)
        return ref_cur[0, 0, (sb - 1) * w:(sb + 1) * w, :]

    scores = [lax.dot_general(q_ref[0, :, sb * w:(sb + 1) * w, :].reshape(rows, LANES), window(kc_ref, kp_ref, sb),
                              nt_dims, preferred_element_type=f32) for sb in range(n_sb)]
    for sb in range(n_sb):
        ok = band
        if sb == 0:
            ok = jnp.logical_and(band, jnp.logical_or(col >= w, i > 0))
        s = jnp.where(ok, scores[sb], NEG)
        m = jnp.maximum(jnp.max(s, axis=1, keepdims=True), sink)
        p = jnp.exp2(s - m).astype(bf16)
        acc = jnp.dot(p, window(vc_ref, vp_ref, sb), preferred_element_type=f32)
        out = acc / (acc[:, HEAD_DIM:HEAD_DIM + 1] + jnp.exp2(sink - m))
        lane = lax.broadcasted_iota(jnp.int32, (w, LANES), 1)
        low = lane < HEAD_DIM
        for c in range(GROUP // 2):
            even = out[(2 * c) * w:(2 * c + 1) * w]
            odd = out[(2 * c + 1) * w:(2 * c + 2) * w]
            o_ref[0, sb * w:(sb + 1) * w, c * LANES:(c + 1) * LANES] = jnp.where(
                low, even, pltpu.roll(odd, HEAD_DIM, 1)).astype(o_ref.dtype)


def _swa(q, k, v, sinks):
    batch, _, seq, _ = q.shape
    per = TQ_SWA // SWA_WINDOW
    cur = lambda b, g, i, s: (b, g, i, 0)
    prev = lambda b, g, i, s: (b, g, jnp.maximum(i * per - 1, 0), 0)
    grid_spec = pltpu.PrefetchScalarGridSpec(
        num_scalar_prefetch=1,
        grid=(batch, N_KV_HEADS, seq // TQ_SWA),
        in_specs=[pl.BlockSpec((1, GROUP, TQ_SWA, LANES), cur),
                  pl.BlockSpec((1, 1, TQ_SWA, LANES), cur),
                  pl.BlockSpec((1, 1, SWA_WINDOW, LANES), prev),
                  pl.BlockSpec((1, 1, TQ_SWA, LANES), cur),
                  pl.BlockSpec((1, 1, SWA_WINDOW, LANES), prev)],
        out_specs=pl.BlockSpec((1, TQ_SWA, GROUP * HEAD_DIM), lambda b, g, i, s: (b, i, g)),
    )
    return pl.pallas_call(
        _swa_kernel,
        grid_spec=grid_spec,
        out_shape=jax.ShapeDtypeStruct((batch, seq, N_HEADS * HEAD_DIM), bf16),
        compiler_params=_cparams("parallel", "parallel", "arbitrary"),
        name="swa_attention",
    )(sinks, q, k, k, v, v)


def _layer_norm(x, g, b):
    mu = jnp.mean(x, axis=-1, keepdims=True)
    xc = x - mu
    var = jnp.mean(xc * xc, axis=-1, keepdims=True)
    return xc * lax.rsqrt(var + LN_EPS) * g + b


def _load_rows(ref, n_rows):
    return jnp.concatenate([ref[pl.ds(k, n_rows, stride=ROW_TILE), :] for k in range(ROW_TILE)], axis=1)


def _store_rows(ref, val):
    for k in range(ROW_TILE):
        ref[pl.ds(k, val.shape[0], stride=ROW_TILE), :] = val[:, k * LANES:(k + 1) * LANES]


def _mix_router_kernel(x_ref, a_ref, wo_ref, g_ref, b_ref, wr_ref, x1_ref, meta_ref, cnt_ref):
    tm = x_ref.shape[0]
    y = ALPHA * x_ref[...] + jnp.dot(a_ref[...], wo_ref[...], preferred_element_type=f32)
    x1 = _layer_norm(y, g_ref[...], b_ref[...])
    _store_rows(x1_ref, x1)

    @pl.when(pl.program_id(0) == 0)
    def _():
        cnt_ref[...] = jnp.zeros_like(cnt_ref)

    wr = wr_ref[...]
    x_hi, w_hi = x1.astype(bf16), wr.astype(bf16)
    x_lo, w_lo = (x1 - x_hi.astype(f32)).astype(bf16), (wr - w_hi.astype(f32)).astype(bf16)
    logits = (jnp.dot(x_hi, w_hi, preferred_element_type=f32) + jnp.dot(x_lo, w_hi, preferred_element_type=f32)
              + jnp.dot(x_hi, w_lo, preferred_element_type=f32))
    lane = lax.broadcasted_iota(jnp.int32, (tm, LANES), 1)
    lg = jnp.where(lane < N_EXPERTS, logits, -jnp.inf)
    m1 = jnp.max(lg, axis=1, keepdims=True)
    i1 = jnp.min(jnp.where(lg == m1, lane, LANES), axis=1, keepdims=True)
    lg2 = jnp.where(lane == i1, -jnp.inf, lg)
    m2 = jnp.max(lg2, axis=1, keepdims=True)
    i2 = jnp.min(jnp.where(lg2 == m2, lane, LANES), axis=1, keepdims=True)
    e2 = jnp.exp(m2 - m1)
    w1 = 1.0 / (1.0 + e2)
    w2 = e2 / (1.0 + e2)
    hit1 = lane == i1
    hit2 = lane == i2
    hits = jnp.where(jnp.logical_or(hit1, hit2), 1.0, 0.0)
    r_i = lax.broadcasted_iota(jnp.int32, (tm, tm), 0)
    c_i = lax.broadcasted_iota(jnp.int32, (tm, tm), 1)
    before = jnp.where(c_i < r_i, 1.0, 0.0).astype(bf16)
    cum = jnp.dot(before, hits.astype(bf16), preferred_element_type=f32) + cnt_ref[0:1, :]
    r1 = jnp.sum(jnp.where(hit1, cum, 0.0), axis=1, keepdims=True)
    r2 = jnp.sum(jnp.where(hit2, cum, 0.0), axis=1, keepdims=True)
    cnt_ref[...] = cnt_ref[...] + jnp.sum(hits, axis=0, keepdims=True)
    meta = jnp.where(lane == 0, i1.astype(f32), 0.0)
    meta = jnp.where(lane == 1, i2.astype(f32), meta)
    meta = jnp.where(lane == 2, w1, meta)
    meta = jnp.where(lane == 3, w2, meta)
    meta = jnp.where(lane == 4, r1, meta)
    meta = jnp.where(lane == 5, r2, meta)
    meta_ref[...] = meta


def _mix_router(x2d, a2d, w_o, g, b, w_router):
    n = x2d.shape[0]
    tok = lambda i: (i, 0)
    const = lambda i: (0, 0)
    wr = jnp.zeros((D_MODEL, LANES), f32).at[:, :N_EXPERTS].set(w_router)
    return pl.pallas_call(
        _mix_router_kernel, grid=(n // TM_MIX,),
        in_specs=[pl.BlockSpec((TM_MIX, D_MODEL), tok),
                  pl.BlockSpec((TM_MIX, D_MODEL), tok),
                  pl.BlockSpec((D_MODEL, D_MODEL), const),
                  pl.BlockSpec((1, D_MODEL), const),
                  pl.BlockSpec((1, D_MODEL), const),
                  pl.BlockSpec((D_MODEL, LANES), const)],
        out_specs=[pl.BlockSpec((TM_MIX * ROW_TILE, LANES), tok),
                   pl.BlockSpec((TM_MIX, LANES), tok), pl.BlockSpec((8, LANES), const)],
        out_shape=[jax.ShapeDtypeStruct((n * ROW_TILE, LANES), f32),
                   jax.ShapeDtypeStruct((n, LANES), f32), jax.ShapeDtypeStruct((8, LANES), f32)],
        compiler_params=_cparams("arbitrary"), name="mix_ln_router",
    )(x2d, a2d, w_o, g, b, wr)


def _swiglu_chunk(xb, wg, wu, wd):
    gate = jnp.dot(xb, wg.astype(bf16), preferred_element_type=f32)
    up = jnp.dot(xb, wu.astype(bf16), preferred_element_type=f32)
    h = (gate / (1.0 + jnp.exp(-gate)) * up).astype(bf16)
    return jnp.dot(h, wd.astype(bf16), preferred_element_type=f32)


def _accumulate(acc_ref, part, fc):
    @pl.when(fc == 0)
    def _():
        acc_ref[...] = part

    @pl.when(fc > 0)
    def _():
        acc_ref[...] = acc_ref[...] + part


def _moe_ffn_kernel(te_ref, nact_ref, x_ref, wg_ref, wu_ref, wd_ref, y_ref, xb_ref, acc_ref):
    n = pl.program_id(0)
    fc = pl.program_id(1)
    active = n < nact_ref[0]

    @pl.when(jnp.logical_and(jnp.logical_not(active), fc == 0))
    def _():
        y_ref[...] = jnp.zeros_like(y_ref)

    @pl.when(active)
    def _():
        @pl.when(fc == 0)
        def _():
            xb_ref[...] = _load_rows(x_ref, TM_FFN).astype(bf16)

        _accumulate(acc_ref, _swiglu_chunk(xb_ref[...], wg_ref[0], wu_ref[0], wd_ref[0]), fc)

        @pl.when(fc == pl.num_programs(1) - 1)
        def _():
            _store_rows(y_ref, acc_ref[...])


def _moe_ffn(xs_rows, w_gate, w_up, w_down, tile_expert, n_active):
    blk = (TM_FFN * ROW_TILE, LANES)
    n_tiles = xs_rows.shape[0] // blk[0]
    n_fc = D_FF // FC_FFN

    def row(n, f, te, na):
        return jnp.minimum(n, na[0] - 1), 0

    def w_in(n, f, te, na):
        return te[jnp.minimum(n, na[0] - 1)], 0, jnp.where(n < na[0], f, n_fc - 1)

    def w_out(n, f, te, na):
        return te[jnp.minimum(n, na[0] - 1)], jnp.where(n < na[0], f, n_fc - 1), 0

    grid_spec = pltpu.PrefetchScalarGridSpec(
        num_scalar_prefetch=2,
        grid=(n_tiles, n_fc),
        in_specs=[pl.BlockSpec(blk, row),
                  pl.BlockSpec((1, D_MODEL, FC_FFN), w_in),
                  pl.BlockSpec((1, D_MODEL, FC_FFN), w_in),
                  pl.BlockSpec((1, FC_FFN, D_MODEL), w_out)],
        out_specs=pl.BlockSpec(blk, lambda n, f, te, na: (n, 0)),
        scratch_shapes=[pltpu.VMEM((TM_FFN, D_MODEL), bf16), pltpu.VMEM((TM_FFN, D_MODEL), f32)],
    )
    return pl.pallas_call(
        _moe_ffn_kernel, grid_spec=grid_spec,
        out_shape=jax.ShapeDtypeStruct(xs_rows.shape, f32),
        compiler_params=_cparams("arbitrary", "arbitrary"), name="moe_swiglu",
    )(tile_expert, n_active, xs_rows, w_gate, w_up, w_down)


def _dense_tail_kernel(x_ref, a_ref, wo_ref, g1_ref, b1_ref, wg_ref, wu_ref, wd_ref, g2_ref, b2_ref,
                       p_ref, wp_ref, wgate_ref, o_ref, x1_ref, xb_ref):
    fc = pl.program_id(1)
    acc_ref = o_ref

    @pl.when(fc == 0)
    def _():
        y = ALPHA * x_ref[...] + jnp.dot(a_ref[...], wo_ref[...], preferred_element_type=f32)
        x1 = _layer_norm(y, g1_ref[...], b1_ref[...])
        x1_ref[...] = x1
        xb_ref[...] = x1.astype(bf16)

    _accumulate(acc_ref, _swiglu_chunk(xb_ref[...], wg_ref[...], wu_ref[...], wd_ref[...]), fc)

    @pl.when(fc == pl.num_programs(1) - 1)
    def _():
        x2 = _layer_norm(ALPHA * x1_ref[...] + acc_ref[...], g2_ref[...], b2_ref[...])
        _ple(x2, p_ref, wp_ref, wgate_ref, o_ref)


def _dense_tail(x2d, a2d, w_o, g1, b1, w_gate, w_up, w_down, g2, b2, p2d, w_proj, w_pgate):
    n = x2d.shape[0]
    tok = lambda i, f: (i, 0)
    const = lambda i, f: (0, 0)
    once = dict(pipeline_mode=pl.Buffered(1))
    return pl.pallas_call(
        _dense_tail_kernel, grid=(n // TM_DENSE, D_FF // FC_FFN),
        in_specs=[pl.BlockSpec((TM_DENSE, D_MODEL), tok),
                  pl.BlockSpec((TM_DENSE, D_MODEL), tok),
                  pl.BlockSpec((D_MODEL, D_MODEL), const, **once),
                  pl.BlockSpec((1, D_MODEL), const, **once),
                  pl.BlockSpec((1, D_MODEL), const, **once),
                  pl.BlockSpec((D_MODEL, FC_FFN), lambda i, f: (0, f)),
                  pl.BlockSpec((D_MODEL, FC_FFN), lambda i, f: (0, f)),
                  pl.BlockSpec((FC_FFN, D_MODEL), lambda i, f: (f, 0)),
                  pl.BlockSpec((1, D_MODEL), const, **once),
                  pl.BlockSpec((1, D_MODEL), const, **once),
                  pl.BlockSpec((TM_DENSE, PLE_DIM), tok),
                  pl.BlockSpec((PLE_DIM, D_MODEL), const, **once),
                  pl.BlockSpec((D_MODEL, D_MODEL), const, **once)],
        out_specs=pl.BlockSpec((TM_DENSE, D_MODEL), tok),
        out_shape=jax.ShapeDtypeStruct((n, D_MODEL), f32),
        scratch_shapes=[pltpu.VMEM((TM_DENSE, D_MODEL), f32), pltpu.VMEM((TM_DENSE, D_MODEL), bf16)],
        compiler_params=pltpu.CompilerParams(dimension_semantics=("parallel", "arbitrary"),
                                             vmem_limit_bytes=MOBA_VMEM_LIMIT), name="dense_tail",
    )(x2d, a2d, w_o, g1, b1, w_gate, w_up, w_down, g2, b2, p2d, w_proj, w_pgate)


def _dispatch_kernel(pos_ref, x_ref, init_ref, xs_ref, sem):
    del init_ref
    tm = x_ref.shape[0] // ROW_TILE

    def start(r, _):
        src = x_ref.at[pl.ds(pl.multiple_of(r * ROW_TILE, ROW_TILE), ROW_TILE)]
        for k in range(2):
            dst = pl.multiple_of(pos_ref[2 * r + k] * ROW_TILE, ROW_TILE)
            pltpu.make_async_copy(src, xs_ref.at[pl.ds(dst, ROW_TILE)], sem).start(priority=k)
        return 0

    lax.fori_loop(0, tm, start, 0, unroll=DMA_UNROLL)
    for _ in range(2):
        pltpu.make_async_copy(x_ref, xs_ref.at[pl.ds(0, tm * ROW_TILE)], sem).wait()


def _dispatch(x1_rows, pos_flat, n_rows):
    n = x1_rows.shape[0] // ROW_TILE
    init = jnp.zeros((n_rows * ROW_TILE, LANES), f32)
    return pl.pallas_call(
        _dispatch_kernel, grid=(n // TM_DISPATCH,),
        in_specs=[pl.BlockSpec((2 * TM_DISPATCH,), lambda i: (i,), memory_space=pltpu.SMEM),
                  pl.BlockSpec((TM_DISPATCH * ROW_TILE, LANES), lambda i: (i, 0)),
                  pl.BlockSpec(memory_space=pl.ANY)],
        out_specs=pl.BlockSpec(memory_space=pl.ANY),
        out_shape=jax.ShapeDtypeStruct((n_rows * ROW_TILE, LANES), f32),
        scratch_shapes=[pltpu.SemaphoreType.DMA(())],
        input_output_aliases={2: 0},
        compiler_params=_cparams("arbitrary"), name="moe_dispatch",
    )(pos_flat, x1_rows, init)


def _ple(x2, p_ref, wp_ref, wgate_ref, o_ref):
    pe = jnp.dot(p_ref[...].astype(bf16), wp_ref[...], preferred_element_type=f32)
    z = jnp.dot(x2.astype(bf16), wgate_ref[...], preferred_element_type=f32)
    o_ref[...] = x2 + pe / (1.0 + jnp.exp(-z))


def _out_moe_kernel(pos_ref, nxt_ref, x1_ref, meta_ref, y_ref, g_ref, b_ref, p_ref, wp_ref, wgate_ref, o_ref,
                    buf_ref, sems):
    i = pl.program_id(0)
    tm = x1_ref.shape[0] // ROW_TILE
    slot = i % 2

    def gather(idx_ref, to_slot):
        def start(r, _):
            dst = pl.ds(pl.multiple_of(r * ROW_TILE, ROW_TILE), ROW_TILE)
            for k in range(2):
                src = pl.multiple_of(idx_ref[2 * r + k] * ROW_TILE, ROW_TILE)
                pltpu.make_async_copy(y_ref.at[pl.ds(src, ROW_TILE)], buf_ref.at[to_slot, k, dst],
                                      sems.at[to_slot]).start(priority=k)
            return 0

        lax.fori_loop(0, tm, start, 0, unroll=DMA_UNROLL)

    @pl.when(i == 0)
    def _():
        gather(pos_ref, slot)

    @pl.when(i + 1 < pl.num_programs(0))
    def _():
        gather(nxt_ref, 1 - slot)

    for k in range(2):
        pltpu.make_async_copy(y_ref.at[pl.ds(0, tm * ROW_TILE)], buf_ref.at[slot, k], sems.at[slot]).wait()
    meta = meta_ref[...]
    f = meta[:, 2:3] * _load_rows(buf_ref.at[slot, 0], tm) + meta[:, 3:4] * _load_rows(buf_ref.at[slot, 1], tm)
    x2 = _layer_norm(ALPHA * _load_rows(x1_ref, tm) + f, g_ref[...], b_ref[...])
    _ple(x2, p_ref, wp_ref, wgate_ref, o_ref)


def _out_moe(x1_rows, meta, y_rows, pos_flat, g, b, p2d, w_proj, w_gate):
    n = x1_rows.shape[0] // ROW_TILE
    steps = n // TM_OUT
    tok = lambda i: (i, 0)
    const = lambda i: (0, 0)
    return pl.pallas_call(
        _out_moe_kernel, grid=(steps,),
        in_specs=[pl.BlockSpec((2 * TM_OUT,), lambda i: (i,), memory_space=pltpu.SMEM),
                  pl.BlockSpec((2 * TM_OUT,), lambda i: (jnp.minimum(i + 1, steps - 1),), memory_space=pltpu.SMEM),
                  pl.BlockSpec((TM_OUT * ROW_TILE, LANES), tok),
                  pl.BlockSpec((TM_OUT, LANES), tok),
                  pl.BlockSpec(memory_space=pl.ANY),
                  pl.BlockSpec((1, D_MODEL), const),
                  pl.BlockSpec((1, D_MODEL), const),
                  pl.BlockSpec((TM_OUT, PLE_DIM), tok),
                  pl.BlockSpec((PLE_DIM, D_MODEL), const),
                  pl.BlockSpec((D_MODEL, D_MODEL), const)],
        out_specs=pl.BlockSpec((TM_OUT, D_MODEL), tok),
        out_shape=jax.ShapeDtypeStruct((n, D_MODEL), f32),
        scratch_shapes=[pltpu.VMEM((2, 2, TM_OUT * ROW_TILE, LANES), f32), pltpu.SemaphoreType.DMA((2,))],
        compiler_params=_cparams("arbitrary"), name="moe_combine_ln_ple",
    )(pos_flat, pos_flat, x1_rows, meta, y_rows, g, b, p2d, w_proj, w_gate)


def _routing_plan(meta, counts_row, n_tiles):
    i1 = meta[:, 0].astype(jnp.int32)
    i2 = meta[:, 1].astype(jnp.int32)
    r1 = meta[:, 4].astype(jnp.int32)
    r2 = meta[:, 5].astype(jnp.int32)
    counts = counts_row[0, :N_EXPERTS].astype(jnp.int32)
    tiles = (counts + TM_FFN - 1) // TM_FFN
    tile_end = jnp.cumsum(tiles)
    offset = (tile_end - tiles) * TM_FFN
    pos = jnp.stack([offset[i1] + r1, offset[i2] + r2], axis=1).reshape(-1)
    tile_expert = jnp.sum(jnp.arange(n_tiles)[:, None] >= tile_end[None, :], axis=1)
    tile_expert = jnp.minimum(tile_expert, N_EXPERTS - 1)
    return pos.astype(jnp.int32), tile_expert.astype(jnp.int32), tile_end[-1:].astype(jnp.int32)


def kernel(x, p, positions, w_qkv, w_o, ln_mix_g, ln_mix_b, ln_ffn_g, ln_ffn_b, sinks, w_ffn_gate, w_ffn_up,
           w_ffn_down, w_router, w_exp_gate, w_exp_up, w_exp_down, w_ple_proj, w_ple_gate):
    batch, seq, _ = x.shape
    n = batch * seq
    assert seq % TQ_SWA == 0 and (seq // MOBA_BLOCK) % 8 == 0 and seq // MOBA_BLOCK <= LANES - HEAD_DIM
    tabs = _rope_tables(positions)
    x2d = x.reshape(n, D_MODEL)
    row = lambda v: v.reshape(1, D_MODEL)
    for i in range(DEPTH):
        j = i // 2
        q, k, v = _qkv(x2d, w_qkv[i].astype(bf16), tabs, batch, seq)
        if i % 2 == 0:
            a = _moba(q, k, v)
        else:
            a = _swa(q, k, v, sinks[j])
        a2d = a.reshape(n, D_MODEL)
        wo = w_o[i].astype(bf16)
        ple_args = (row(ln_ffn_g[i]), row(ln_ffn_b[i]), p[i].reshape(n, PLE_DIM),
                    w_ple_proj[i].astype(bf16), w_ple_gate[i].astype(bf16))
        if i % 2 == 0:
            x2d = _dense_tail(x2d, a2d, wo, row(ln_mix_g[i]), row(ln_mix_b[i]), w_ffn_gate[j].astype(bf16),
                              w_ffn_up[j].astype(bf16), w_ffn_down[j].astype(bf16), *ple_args)
        else:
            x1, meta, counts = _mix_router(x2d, a2d, wo, row(ln_mix_g[i]), row(ln_mix_b[i]), w_router[j])
            n_tiles = 2 * n // TM_FFN + N_EXPERTS
            pos, tile_expert, n_active = _routing_plan(meta, counts, n_tiles)
            xs = _dispatch(x1, pos, n_tiles * TM_FFN)
            y = _moe_ffn(xs, w_exp_gate[j], w_exp_up[j], w_exp_down[j], tile_expert, n_active)
            x2d = _out_moe(x1, meta, y, pos, *ple_args)
    return x2d.reshape(batch, seq, D_MODEL)
```

```python
import functools

import jax
import jax.numpy as jnp
import numpy as np
from jax import lax
from jax.experimental import pallas as pl
from jax.experimental.pallas import tpu as pltpu

D_MODEL = 1024
N_HEADS = 16
N_KV_HEADS = 4
HEAD_DIM = 64
GROUP = N_HEADS // N_KV_HEADS
QKV_DIM = (N_HEADS + 2 * N_KV_HEADS) * HEAD_DIM
ROT_DIM = 16
ROPE_THETA = 500000.0
MOBA_BLOCK = 256
MOBA_TOPK = 3
SWA_WINDOW = 128
D_FF = 3584
N_EXPERTS = 8
PLE_DIM = 256
LN_EPS = 1e-5
DEPTH = 2
ALPHA = (2.0 * DEPTH) ** 0.25
LOG2E = 1.4426950408889634
Q_SCALE = HEAD_DIM ** -0.5 * LOG2E

LANES = 128
ROW_TILE = D_MODEL // LANES
DMA_UNROLL = 8
NEG = -1e30
VMEM_LIMIT = 48 * 1024 * 1024

TM_QKV = 512
TM_MIX = 512
TM_DENSE = 1024
TM_FFN = 1024
FC_FFN = 512
TM_DISPATCH = 512
TM_OUT = 256
TQ_SWA = 512
SWA_KV_PER_STEP = 1
MOBA_SUB = 512
MOBA_KEYS = 2 * MOBA_BLOCK
MOBA_KV_PER_STEP = 2
MOBA_VMEM_LIMIT = 56 * 1024 * 1024

f32 = jnp.float32
bf16 = jnp.bfloat16


def _cparams(*sem):
    return pltpu.CompilerParams(dimension_semantics=sem, vmem_limit_bytes=VMEM_LIMIT)


def _rope_kernel(pos_ref, inv_ref, cos_ref, sin_ref):
    ang = pos_ref[...].astype(f32) * inv_ref[...]
    cos_ref[...] = jnp.cos(ang)
    sin_ref[...] = jnp.sin(ang)


def _rope_tables(positions):
    n = positions.size
    half = ROT_DIM // 2
    inv = 1.0 / (ROPE_THETA ** (jnp.arange(0, ROT_DIM, 2, dtype=f32) / ROT_DIM))
    rows = n * half // LANES
    pos_rep = jnp.repeat(positions.reshape(-1), half).reshape(rows, LANES)
    inv_rep = jnp.tile(inv, LANES // half).reshape(1, LANES)
    cos, sin = pl.pallas_call(
        _rope_kernel,
        out_shape=(jax.ShapeDtypeStruct((rows, LANES), f32),) * 2,
        name="rope_tables",
    )(pos_rep, inv_rep)
    cos = jnp.tile(cos.reshape(n, half), (1, LANES // half))
    sin = jnp.tile(sin.reshape(n, half), (1, LANES // half))
    d = jnp.arange(LANES) % HEAD_DIM
    c_tab = jnp.where(d < ROT_DIM, cos, 1.0)
    s_lo = jnp.where(d < half, -sin, 0.0)
    s_hi = jnp.where((d >= half) & (d < ROT_DIM), sin, 0.0)
    return c_tab, s_lo, s_hi


def _qkv_kernel(x_ref, w_ref, c_ref, slo_ref, shi_ref, q_ref, k_ref, v_ref, *, tiles_per_seq):
    tm = x_ref.shape[0]
    acc = jnp.dot(x_ref[...].astype(bf16), w_ref[...], preferred_element_type=f32)
    lane = lax.broadcasted_iota(jnp.int32, (tm, LANES), 1)
    row = lax.broadcasted_iota(jnp.int32, (tm, LANES), 0)
    low = lane < HEAD_DIM
    c_tab, s_lo, s_hi = c_ref[...], slo_ref[...], shi_ref[...]
    seq0 = (pl.program_id(0) % tiles_per_seq) * tm
    blk = (seq0 + row) // MOBA_BLOCK
    k_pad = jnp.where(lane == HEAD_DIM + blk, 1.0, 0.0)
    v_pad = jnp.where(lane == HEAD_DIM, 1.0, 0.0)
    n_q = N_HEADS // 2
    n_kv = N_KV_HEADS // 2
    for c in range(n_q + n_kv):
        xc = acc[:, c * LANES:(c + 1) * LANES]
        r = xc * c_tab + pltpu.roll(xc, LANES - ROT_DIM // 2, 1) * s_lo + pltpu.roll(xc, ROT_DIM // 2, 1) * s_hi
        r_odd = pltpu.roll(r, HEAD_DIM, 1)
        if c < n_q:
            q_ref[0, 2 * c] = jnp.where(low, r * Q_SCALE, 0.0).astype(bf16)
            q_ref[0, 2 * c + 1] = jnp.where(low, r_odd * Q_SCALE, 0.0).astype(bf16)
        else:
            k_ref[0, 2 * (c - n_q)] = jnp.where(low, r, k_pad).astype(bf16)
            k_ref[0, 2 * (c - n_q) + 1] = jnp.where(low, r_odd, k_pad).astype(bf16)
    for c in range(n_kv):
        xc = acc[:, (n_q + n_kv + c) * LANES:(n_q + n_kv + c + 1) * LANES]
        v_ref[0, 2 * c] = jnp.where(low, xc, v_pad).astype(bf16)
        v_ref[0, 2 * c + 1] = jnp.where(low, pltpu.roll(xc, HEAD_DIM, 1), v_pad).astype(bf16)


def _qkv(x2d, w_qkv, tabs, batch, seq):
    n = x2d.shape[0]
    nt = seq // TM_QKV
    tok = lambda i: (i, 0)
    head = lambda i: (i // nt, 0, i % nt, 0)
    return pl.pallas_call(
        functools.partial(_qkv_kernel, tiles_per_seq=nt),
        grid=(n // TM_QKV,),
        in_specs=[pl.BlockSpec((TM_QKV, D_MODEL), tok),
                  pl.BlockSpec((D_MODEL, QKV_DIM), lambda i: (0, 0)),
                  pl.BlockSpec((TM_QKV, LANES), tok),
                  pl.BlockSpec((TM_QKV, LANES), tok),
                  pl.BlockSpec((TM_QKV, LANES), tok)],
        out_specs=[pl.BlockSpec((1, N_HEADS, TM_QKV, LANES), head),
                   pl.BlockSpec((1, N_KV_HEADS, TM_QKV, LANES), head),
                   pl.BlockSpec((1, N_KV_HEADS, TM_QKV, LANES), head)],
        out_shape=[jax.ShapeDtypeStruct((batch, N_HEADS, seq, LANES), bf16),
                   jax.ShapeDtypeStruct((batch, N_KV_HEADS, seq, LANES), bf16),
                   jax.ShapeDtypeStruct((batch, N_KV_HEADS, seq, LANES), bf16)],
        compiler_params=_cparams("parallel"),
        name="qkv_rope",
    )(x2d, w_qkv, *tabs)


def _merge_heads(o, out_ref, rows):
    lane = lax.broadcasted_iota(jnp.int32, (rows, LANES), 1)
    low = lane < HEAD_DIM
    for c in range(o.shape[0] // rows // 2):
        even = o[(2 * c) * rows:(2 * c + 1) * rows]
        odd = o[(2 * c + 1) * rows:(2 * c + 2) * rows]
        out_ref[0, :, c * LANES:(c + 1) * LANES] = jnp.where(low, even, pltpu.roll(odd, HEAD_DIM, 1)).astype(out_ref.dtype)


def _moba_kernel(q_ref, k_ref, v_ref, o_ref, kmean_ref, qa_ref, s0_ref, s1_ref, m_ref, acc_ref):
    u = pl.program_id(2)
    blk = MOBA_BLOCK
    tq = q_ref.shape[2]
    nb = k_ref.shape[2] // blk
    n_grp = k_ref.shape[1]
    grp_rows = GROUP * tq
    rows = n_grp * grp_rows
    sub = MOBA_SUB
    n_sub = rows // sub
    nt_dims = (((1,), (1,)), ((), ()))

    @pl.when(u == 0)
    def _():
        for kv in range(n_grp):
            for j in range(nb):
                kj = k_ref[0, kv, j * blk:(j + 1) * blk, :].astype(f32)
                kmean_ref[kv, j:j + 1, :] = jnp.sum(kj, axis=0, keepdims=True) * (1.0 / blk)

    for kv in range(n_grp):
        q = q_ref[0, kv * GROUP:(kv + 1) * GROUP].reshape(grp_rows, LANES)
        km = kmean_ref[kv]
        km_hi = km.astype(bf16)
        km_lo = (km - km_hi.astype(f32)).astype(bf16)
        g = (lax.dot_general(km_hi, q, nt_dims, preferred_element_type=f32)
             + lax.dot_general(km_lo, q, nt_dims, preferred_element_type=f32))
        jidx = lax.broadcasted_iota(jnp.int32, (nb, grp_rows), 0)
        q_blk = u * (tq // blk) + (lax.broadcasted_iota(jnp.int32, (nb, grp_rows), 1) % tq) // blk
        past = jidx < q_blk
        g = jnp.where(past, g, -jnp.inf)
        sel = jidx == q_blk
        for _ in range(MOBA_TOPK):
            mx = jnp.max(g, axis=0, keepdims=True)
            first = jnp.min(jnp.where(g == mx, jidx, nb), axis=0, keepdims=True)
            pick = jidx == first
            sel = jnp.logical_or(sel, jnp.logical_and(pick, past))
            g = jnp.where(pick, -jnp.inf, g)
        bias = jnp.where(sel, 0.0, NEG)
        bias_t = jnp.concatenate([jnp.zeros((HEAD_DIM, grp_rows), f32), bias,
                                  jnp.zeros((LANES - HEAD_DIM - nb, grp_rows), f32)], axis=0)
        qa_ref[kv * grp_rows:(kv + 1) * grp_rows] = (q.astype(f32) + bias_t.T).astype(bf16)

    m_ref[...] = jnp.full_like(m_ref, NEG)
    acc_ref[...] = jnp.zeros_like(acc_ref)

    def scores(c, s_ref, r):
        o = pl.multiple_of(c * MOBA_KEYS, MOBA_KEYS)
        rs = slice(r * sub, (r + 1) * sub)
        kv = r * sub // grp_rows
        s_ref[rs] = lax.dot_general(qa_ref[rs], k_ref[0, kv, pl.ds(o, MOBA_KEYS), :], nt_dims,
                                    preferred_element_type=f32)

    def accumulate(c, s_ref, r, causal):
        o = pl.multiple_of(c * MOBA_KEYS, MOBA_KEYS)
        rs = slice(r * sub, (r + 1) * sub)
        kv = r * sub // grp_rows
        s = s_ref[rs]
        if causal:
            col = lax.broadcasted_iota(jnp.int32, (sub, MOBA_KEYS), 1)
            tok = (lax.broadcasted_iota(jnp.int32, (sub, MOBA_KEYS), 0) + r * sub) % tq
            s = jnp.where(jnp.logical_and(col // blk == tok // blk, col > tok), NEG, s)
        m_old = m_ref[rs]
        m_new = jnp.maximum(m_old, jnp.max(s, axis=1, keepdims=True))
        p = jnp.exp2(s - m_new).astype(bf16)
        pv = jnp.dot(p, v_ref[0, kv, pl.ds(o, MOBA_KEYS), :], preferred_element_type=f32)
        acc_ref[rs] = acc_ref[rs] * jnp.exp2(m_old - m_new) + pv
        m_ref[rs] = m_new

    def step(c_next, s_next, c_cur, s_cur, causal=False):
        for r in range(n_sub):
            if c_next is not None:
                scores(c_next, s_next, r)
            accumulate(c_cur, s_cur, r, causal)

    for r in range(n_sub):
        scores(0, s0_ref, r)

    def body(t, carry):
        step(2 * t + 1, s1_ref, 2 * t, s0_ref)
        step(2 * t + 2, s0_ref, 2 * t + 1, s1_ref)
        return carry

    lax.fori_loop(0, u // 2, body, 0)

    @pl.when(u % 2 == 1)
    def _():
        step(u, s1_ref, u - 1, s0_ref)
        step(None, None, u, s1_ref, causal=True)

    @pl.when(u % 2 == 0)
    def _():
        step(None, None, u, s0_ref, causal=True)

    acc = acc_ref[...]
    _merge_heads(acc / acc[:, HEAD_DIM:HEAD_DIM + 1], o_ref, tq)


def _moba(q, k, v):
    batch, _, seq, _ = q.shape
    nb = seq // MOBA_BLOCK
    n_grp = MOBA_KV_PER_STEP
    rows = n_grp * GROUP * MOBA_KEYS
    return pl.pallas_call(
        _moba_kernel,
        grid=(batch, N_KV_HEADS // n_grp, seq // MOBA_KEYS),
        in_specs=[pl.BlockSpec((1, n_grp * GROUP, MOBA_KEYS, LANES), lambda b, g, u: (b, g, u, 0)),
                  pl.BlockSpec((1, n_grp, seq, LANES), lambda b, g, u: (b, g, 0, 0)),
                  pl.BlockSpec((1, n_grp, seq, LANES), lambda b, g, u: (b, g, 0, 0))],
        out_specs=pl.BlockSpec((1, MOBA_KEYS, n_grp * GROUP * HEAD_DIM), lambda b, g, u: (b, u, g)),
        out_shape=jax.ShapeDtypeStruct((batch, seq, N_HEADS * HEAD_DIM), bf16),
        scratch_shapes=[pltpu.VMEM((n_grp, nb, LANES), f32),
                        pltpu.VMEM((rows, LANES), bf16),
                        pltpu.VMEM((rows, MOBA_KEYS), f32),
                        pltpu.VMEM((rows, MOBA_KEYS), f32),
                        pltpu.VMEM((rows, 1), f32),
                        pltpu.VMEM((rows, LANES), f32)],
        compiler_params=pltpu.CompilerParams(dimension_semantics=("parallel", "parallel", "arbitrary"),
                                             vmem_limit_bytes=MOBA_VMEM_LIMIT),
        name="moba_attention",
    )(q, k, v)


def _swa_kernel(sink_ref, q_ref, kc_ref, kp_ref, vc_ref, vp_ref, o_ref):
    g = pl.program_id(1)
    i = pl.program_id(2)
    w = SWA_WINDOW
    n_grp = kc_ref.shape[1]
    rows = GROUP * w
    nt_dims = (((1,), (1,)), ((), ()))
    n_sb = q_ref.shape[2] // w
    t_in = lax.broadcasted_iota(jnp.int32, (rows, 2 * w), 0) % w
    col = lax.broadcasted_iota(jnp.int32, (rows, 2 * w), 1)
    band = jnp.logical_and(col > t_in, col <= t_in + w)
    first = jnp.logical_and(band, jnp.logical_or(col >= w, i > 0))
    head_of_row = lax.broadcasted_iota(jnp.int32, (rows, 1), 0) // w
    lane = lax.broadcasted_iota(jnp.int32, (w, LANES), 1)
    low = lane < HEAD_DIM

    def window(ref_cur, ref_prev, kv, sb):
        if sb == 0:
            return jnp.concatenate([ref_prev[0, kv], ref_cur[0, kv, 0:w, :]], axis=0)
        return ref_cur[0, kv, (sb - 1) * w:(sb + 1) * w, :]

    work = [(kv, sb) for kv in range(n_grp) for sb in range(n_sb)]
    scores = [lax.dot_general(q_ref[0, kv * GROUP:(kv + 1) * GROUP, sb * w:(sb + 1) * w, :].reshape(rows, LANES),
                              window(kc_ref, kp_ref, kv, sb), nt_dims, preferred_element_type=f32)
              for kv, sb in work]
    for (kv, sb), sc in zip(work, scores):
        sink = jnp.zeros((rows, 1), f32)
        for hh in range(GROUP):
            sink = jnp.where(head_of_row == hh, sink_ref[(g * n_grp + kv) * GROUP + hh] * LOG2E, sink)
        s = jnp.where(first if sb == 0 else band, sc, NEG)
        m = jnp.maximum(jnp.max(s, axis=1, keepdims=True), sink)
        p = jnp.exp2(s - m).astype(bf16)
        acc = jnp.dot(p, window(vc_ref, vp_ref, kv, sb), preferred_element_type=f32)
        out = acc / (acc[:, HEAD_DIM:HEAD_DIM + 1] + jnp.exp2(sink - m))
        for c in range(GROUP // 2):
            even = out[(2 * c) * w:(2 * c + 1) * w]
            odd = out[(2 * c + 1) * w:(2 * c + 2) * w]
            chunk = kv * (GROUP // 2) + c
            o_ref[0, sb * w:(sb + 1) * w, chunk * LANES:(chunk + 1) * LANES] = jnp.where(
                low, even, pltpu.roll(odd, HEAD_DIM, 1)).astype(o_ref.dtype)


def _swa(q, k, v, sinks):
    batch, _, seq, _ = q.shape
    per = TQ_SWA // SWA_WINDOW
    n_grp = SWA_KV_PER_STEP
    cur = lambda b, g, i, s: (b, g, i, 0)
    prev = lambda b, g, i, s: (b, g, jnp.maximum(i * per - 1, 0), 0)
    grid_spec = pltpu.PrefetchScalarGridSpec(
        num_scalar_prefetch=1,
        grid=(batch, N_KV_HEADS // n_grp, seq // TQ_SWA),
        in_specs=[pl.BlockSpec((1, n_grp * GROUP, TQ_SWA, LANES), cur),
                  pl.BlockSpec((1, n_grp, TQ_SWA, LANES), cur),
                  pl.BlockSpec((1, n_grp, SWA_WINDOW, LANES), prev),
                  pl.BlockSpec((1, n_grp, TQ_SWA, LANES), cur),
                  pl.BlockSpec((1, n_grp, SWA_WINDOW, LANES), prev)],
        out_specs=pl.BlockSpec((1, TQ_SWA, n_grp * GROUP * HEAD_DIM), lambda b, g, i, s: (b, i, g)),
    )
    return pl.pallas_call(
        _swa_kernel,
        grid_spec=grid_spec,
        out_shape=jax.ShapeDtypeStruct((batch, seq, N_HEADS * HEAD_DIM), bf16),
        compiler_params=_cparams("parallel", "parallel", "arbitrary"),
        name="swa_attention",
    )(sinks, q, k, k, v, v)


def _layer_norm(x, g, b):
    mu = jnp.mean(x, axis=-1, keepdims=True)
    xc = x - mu
    var = jnp.mean(xc * xc, axis=-1, keepdims=True)
    return xc * lax.rsqrt(var + LN_EPS) * g + b


def _load_rows(ref, n_rows):
    return jnp.concatenate([ref[pl.ds(k, n_rows, stride=ROW_TILE), :] for k in range(ROW_TILE)], axis=1)


def _store_rows(ref, val):
    for k in range(ROW_TILE):
        ref[pl.ds(k, val.shape[0], stride=ROW_TILE), :] = val[:, k * LANES:(k + 1) * LANES]


def _mix_router_kernel(x_ref, a_ref, wo_ref, g_ref, b_ref, wr_ref, x1_ref, meta_ref, metat_ref, cnt_ref):
    tm = x_ref.shape[0]
    y = ALPHA * x_ref[...] + jnp.dot(a_ref[...], wo_ref[...], preferred_element_type=f32)
    x1 = _layer_norm(y, g_ref[...], b_ref[...])
    _store_rows(x1_ref, x1)

    @pl.when(pl.program_id(0) == 0)
    def _():
        cnt_ref[...] = jnp.zeros_like(cnt_ref)

    logits = jnp.dot(x1, wr_ref[...], preferred_element_type=f32)
    lane = lax.broadcasted_iota(jnp.int32, (tm, LANES), 1)
    lg = jnp.where(lane < N_EXPERTS, logits, -jnp.inf)
    m1 = jnp.max(lg, axis=1, keepdims=True)
    i1 = jnp.min(jnp.where(lg == m1, lane, LANES), axis=1, keepdims=True)
    lg2 = jnp.where(lane == i1, -jnp.inf, lg)
    m2 = jnp.max(lg2, axis=1, keepdims=True)
    i2 = jnp.min(jnp.where(lg2 == m2, lane, LANES), axis=1, keepdims=True)
    e2 = jnp.exp(m2 - m1)
    w1 = 1.0 / (1.0 + e2)
    w2 = e2 / (1.0 + e2)
    hit1 = lane == i1
    hit2 = lane == i2
    hits = jnp.where(jnp.logical_or(hit1, hit2), 1.0, 0.0)
    r_i = lax.broadcasted_iota(jnp.int32, (tm, tm), 0)
    c_i = lax.broadcasted_iota(jnp.int32, (tm, tm), 1)
    before = jnp.where(c_i < r_i, 1.0, 0.0).astype(bf16)
    cum = jnp.dot(before, hits.astype(bf16), preferred_element_type=f32) + cnt_ref[0:1, :]
    r1 = jnp.sum(jnp.where(hit1, cum, 0.0), axis=1, keepdims=True)
    r2 = jnp.sum(jnp.where(hit2, cum, 0.0), axis=1, keepdims=True)
    cnt_ref[...] = cnt_ref[...] + jnp.sum(hits, axis=0, keepdims=True)
    meta = jnp.where(lane == 0, i1.astype(f32), 0.0)
    meta = jnp.where(lane == 1, i2.astype(f32), meta)
    meta = jnp.where(lane == 2, w1, meta)
    meta = jnp.where(lane == 3, w2, meta)
    meta = jnp.where(lane == 4, r1, meta)
    meta = jnp.where(lane == 5, r2, meta)
    meta_ref[...] = meta
    metat_ref[...] = meta.T[0:8, :]


def _mix_router(x2d, a2d, w_o, g, b, w_router):
    n = x2d.shape[0]
    tok = lambda i: (i, 0)
    const = lambda i: (0, 0)
    wr = jnp.zeros((D_MODEL, LANES), f32).at[:, :N_EXPERTS].set(w_router)
    return pl.pallas_call(
        _mix_router_kernel, grid=(n // TM_MIX,),
        in_specs=[pl.BlockSpec((TM_MIX, D_MODEL), tok),
                  pl.BlockSpec((TM_MIX, D_MODEL), tok),
                  pl.BlockSpec((D_MODEL, D_MODEL), const),
                  pl.BlockSpec((1, D_MODEL), const),
                  pl.BlockSpec((1, D_MODEL), const),
                  pl.BlockSpec((D_MODEL, LANES), const)],
        out_specs=[pl.BlockSpec((TM_MIX * ROW_TILE, LANES), tok),
                   pl.BlockSpec((TM_MIX, LANES), tok), pl.BlockSpec((8, TM_MIX), lambda i: (0, i)),
                   pl.BlockSpec((8, LANES), const)],
        out_shape=[jax.ShapeDtypeStruct((n * ROW_TILE, LANES), f32),
                   jax.ShapeDtypeStruct((n, LANES), f32), jax.ShapeDtypeStruct((8, n), f32),
                   jax.ShapeDtypeStruct((8, LANES), f32)],
        compiler_params=_cparams("arbitrary"), name="mix_ln_router",
    )(x2d, a2d, w_o, g, b, wr)


def _swiglu_accumulate(acc_ref, xb, wg, wu, wd, fc):
    gate = jnp.dot(xb, wg.astype(bf16), preferred_element_type=f32)
    up = jnp.dot(xb, wu.astype(bf16), preferred_element_type=f32)
    h = (gate / (1.0 + jnp.exp(-gate)) * up).astype(bf16)
    part = jnp.dot(h, wd.astype(bf16), preferred_element_type=f32)

    @pl.when(fc == 0)
    def _():
        acc_ref[...] = part

    @pl.when(fc > 0)
    def _():
        acc_ref[...] = acc_ref[...] + part


def _moe_ffn_kernel(te_ref, nact_ref, x_ref, wg_ref, wu_ref, wd_ref, y_ref, xb_ref, acc_ref):
    n = pl.program_id(0)
    fc = pl.program_id(1)
    active = n < nact_ref[0]

    @pl.when(jnp.logical_and(jnp.logical_not(active), fc == 0))
    def _():
        y_ref[...] = jnp.zeros_like(y_ref)

    @pl.when(active)
    def _():
        @pl.when(fc == 0)
        def _():
            xb_ref[...] = _load_rows(x_ref, TM_FFN).astype(bf16)

        _swiglu_accumulate(acc_ref, xb_ref[...], wg_ref[0], wu_ref[0], wd_ref[0], fc)

        @pl.when(fc == pl.num_programs(1) - 1)
        def _():
            _store_rows(y_ref, acc_ref[...])


def _moe_ffn(xs_rows, w_gate, w_up, w_down, tile_expert, n_active):
    blk = (TM_FFN * ROW_TILE, LANES)
    n_tiles = xs_rows.shape[0] // blk[0]
    n_fc = D_FF // FC_FFN

    def row(n, f, te, na):
        return jnp.minimum(n, na[0] - 1), 0

    def w_in(n, f, te, na):
        return te[jnp.minimum(n, na[0] - 1)], 0, jnp.where(n < na[0], f, n_fc - 1)

    def w_out(n, f, te, na):
        return te[jnp.minimum(n, na[0] - 1)], jnp.where(n < na[0], f, n_fc - 1), 0

    grid_spec = pltpu.PrefetchScalarGridSpec(
        num_scalar_prefetch=2,
        grid=(n_tiles, n_fc),
        in_specs=[pl.BlockSpec(blk, row),
                  pl.BlockSpec((1, D_MODEL, FC_FFN), w_in),
                  pl.BlockSpec((1, D_MODEL, FC_FFN), w_in),
                  pl.BlockSpec((1, FC_FFN, D_MODEL), w_out)],
        out_specs=pl.BlockSpec(blk, lambda n, f, te, na: (n, 0)),
        scratch_shapes=[pltpu.VMEM((TM_FFN, D_MODEL), bf16), pltpu.VMEM((TM_FFN, D_MODEL), f32)],
    )
    return pl.pallas_call(
        _moe_ffn_kernel, grid_spec=grid_spec,
        out_shape=jax.ShapeDtypeStruct(xs_rows.shape, f32),
        compiler_params=_cparams("arbitrary", "arbitrary"), name="moe_swiglu",
    )(tile_expert, n_active, xs_rows, w_gate, w_up, w_down)


def _dense_tail_kernel(x_ref, a_ref, wo_ref, g1_ref, b1_ref, wg_ref, wu_ref, wd_ref, g2_ref, b2_ref,
                       p_ref, wp_ref, wgate_ref, o_ref, x1_ref, xb_ref):
    fc = pl.program_id(1)
    acc_ref = o_ref

    @pl.when(fc == 0)
    def _():
        y = ALPHA * x_ref[...] + jnp.dot(a_ref[...], wo_ref[...], preferred_element_type=f32)
        x1 = _layer_norm(y, g1_ref[...], b1_ref[...])
        x1_ref[...] = x1
        xb_ref[...] = x1.astype(bf16)

    _swiglu_accumulate(acc_ref, xb_ref[...], wg_ref[...], wu_ref[...], wd_ref[...], fc)

    @pl.when(fc == pl.num_programs(1) - 1)
    def _():
        x2 = _layer_norm(ALPHA * x1_ref[...] + acc_ref[...], g2_ref[...], b2_ref[...])
        _ple(x2, p_ref, wp_ref, wgate_ref, o_ref)


def _dense_tail(x2d, a2d, w_o, g1, b1, w_gate, w_up, w_down, g2, b2, p2d, w_proj, w_pgate):
    n = x2d.shape[0]
    tok = lambda i, f: (i, 0)
    const = lambda i, f: (0, 0)
    once = dict(pipeline_mode=pl.Buffered(1))
    return pl.pallas_call(
        _dense_tail_kernel, grid=(n // TM_DENSE, D_FF // FC_FFN),
        in_specs=[pl.BlockSpec((TM_DENSE, D_MODEL), tok),
                  pl.BlockSpec((TM_DENSE, D_MODEL), tok),
                  pl.BlockSpec((D_MODEL, D_MODEL), const, **once),
                  pl.BlockSpec((1, D_MODEL), const, **once),
                  pl.BlockSpec((1, D_MODEL), const, **once),
                  pl.BlockSpec((D_MODEL, FC_FFN), lambda i, f: (0, f)),
                  pl.BlockSpec((D_MODEL, FC_FFN), lambda i, f: (0, f)),
                  pl.BlockSpec((FC_FFN, D_MODEL), lambda i, f: (f, 0)),
                  pl.BlockSpec((1, D_MODEL), const, **once),
                  pl.BlockSpec((1, D_MODEL), const, **once),
                  pl.BlockSpec((TM_DENSE, PLE_DIM), tok),
                  pl.BlockSpec((PLE_DIM, D_MODEL), const, **once),
                  pl.BlockSpec((D_MODEL, D_MODEL), const, **once)],
        out_specs=pl.BlockSpec((TM_DENSE, D_MODEL), tok),
        out_shape=jax.ShapeDtypeStruct((n, D_MODEL), f32),
        scratch_shapes=[pltpu.VMEM((TM_DENSE, D_MODEL), f32), pltpu.VMEM((TM_DENSE, D_MODEL), bf16)],
        compiler_params=pltpu.CompilerParams(dimension_semantics=("parallel", "arbitrary"),
                                             vmem_limit_bytes=MOBA_VMEM_LIMIT), name="dense_tail",
    )(x2d, a2d, w_o, g1, b1, w_gate, w_up, w_down, g2, b2, p2d, w_proj, w_pgate)


def _dispatch_kernel(last_ref, nact_ref, pos_ref, x_ref, xs_ref, zero_ref, sem, zsem):
    tm = x_ref.shape[0] // ROW_TILE
    n_tiles = xs_ref.shape[0] // zero_ref.shape[0]

    @pl.when(pl.program_id(0) == 0)
    def _():
        zero_ref[...] = jnp.zeros_like(zero_ref)

        def fill(t):
            dst = pl.multiple_of(t * zero_ref.shape[0], zero_ref.shape[0])
            return pltpu.make_async_copy(zero_ref, xs_ref.at[pl.ds(dst, zero_ref.shape[0])], zsem)

        def start_tail(t, _):
            fill(t).start()
            return 0

        def wait_tail(t, _):
            fill(t).wait()
            return 0

        for e in range(N_EXPERTS):
            pl.when(last_ref[e] >= 0)(lambda e=e: fill(last_ref[e]).start())
        lax.fori_loop(nact_ref[0], n_tiles, start_tail, 0)
        for e in range(N_EXPERTS):
            pl.when(last_ref[e] >= 0)(lambda e=e: fill(last_ref[e]).wait())
        lax.fori_loop(nact_ref[0], n_tiles, wait_tail, 0)

    def start(r, _):
        src = x_ref.at[pl.ds(pl.multiple_of(r * ROW_TILE, ROW_TILE), ROW_TILE)]
        for k in range(2):
            dst = pl.multiple_of(pos_ref[2 * r + k] * ROW_TILE, ROW_TILE)
            pltpu.make_async_copy(src, xs_ref.at[pl.ds(dst, ROW_TILE)], sem).start(priority=k)
        return 0

    lax.fori_loop(0, tm, start, 0, unroll=DMA_UNROLL)
    for _ in range(2):
        pltpu.make_async_copy(x_ref, xs_ref.at[pl.ds(0, tm * ROW_TILE)], sem).wait()


def _dispatch(x1_rows, pos_flat, last_tile, n_active, n_tiles):
    n = x1_rows.shape[0] // ROW_TILE
    grid_spec = pltpu.PrefetchScalarGridSpec(
        num_scalar_prefetch=2,
        grid=(n // TM_DISPATCH,),
        in_specs=[pl.BlockSpec((2 * TM_DISPATCH,), lambda i, lt, na: (i,), memory_space=pltpu.SMEM),
                  pl.BlockSpec((TM_DISPATCH * ROW_TILE, LANES), lambda i, lt, na: (i, 0))],
        out_specs=pl.BlockSpec(memory_space=pl.ANY),
        scratch_shapes=[pltpu.VMEM((TM_FFN * ROW_TILE, LANES), f32), pltpu.SemaphoreType.DMA(()),
                        pltpu.SemaphoreType.DMA(())],
    )
    return pl.pallas_call(
        _dispatch_kernel, grid_spec=grid_spec,
        out_shape=jax.ShapeDtypeStruct((n_tiles * TM_FFN * ROW_TILE, LANES), f32),
        compiler_params=_cparams("arbitrary"), name="moe_dispatch",
    )(last_tile, n_active, pos_flat, x1_rows)


def _ple(x2, p_ref, wp_ref, wgate_ref, o_ref):
    pe = jnp.dot(p_ref[...].astype(bf16), wp_ref[...], preferred_element_type=f32)
    z = jnp.dot(x2.astype(bf16), wgate_ref[...], preferred_element_type=f32)
    o_ref[...] = x2 + pe / (1.0 + jnp.exp(-z))


def _out_moe_kernel(pos_ref, nxt_ref, x1_ref, meta_ref, y_ref, g_ref, b_ref, p_ref, wp_ref, wgate_ref, o_ref,
                    buf_ref, sems):
    i = pl.program_id(0)
    tm = x1_ref.shape[0] // ROW_TILE
    slot = i % 2

    def gather(idx_ref, to_slot):
        def start(r, _):
            dst = pl.ds(pl.multiple_of(r * ROW_TILE, ROW_TILE), ROW_TILE)
            for k in range(2):
                src = pl.multiple_of(idx_ref[2 * r + k] * ROW_TILE, ROW_TILE)
                pltpu.make_async_copy(y_ref.at[pl.ds(src, ROW_TILE)], buf_ref.at[to_slot, k, dst],
                                      sems.at[to_slot]).start(priority=k)
            return 0

        lax.fori_loop(0, tm, start, 0, unroll=DMA_UNROLL)

    @pl.when(i == 0)
    def _():
        gather(pos_ref, slot)

    @pl.when(i + 1 < pl.num_programs(0))
    def _():
        gather(nxt_ref, 1 - slot)

    for k in range(2):
        pltpu.make_async_copy(y_ref.at[pl.ds(0, tm * ROW_TILE)], buf_ref.at[slot, k], sems.at[slot]).wait()
    meta = meta_ref[...]
    f = meta[:, 2:3] * _load_rows(buf_ref.at[slot, 0], tm) + meta[:, 3:4] * _load_rows(buf_ref.at[slot, 1], tm)
    x2 = _layer_norm(ALPHA * _load_rows(x1_ref, tm) + f, g_ref[...], b_ref[...])
    _ple(x2, p_ref, wp_ref, wgate_ref, o_ref)


def _out_moe(x1_rows, meta, y_rows, pos_flat, g, b, p2d, w_proj, w_gate):
    n = x1_rows.shape[0] // ROW_TILE
    steps = n // TM_OUT
    tok = lambda i: (i, 0)
    const = lambda i: (0, 0)
    return pl.pallas_call(
        _out_moe_kernel, grid=(steps,),
        in_specs=[pl.BlockSpec((2 * TM_OUT,), lambda i: (i,), memory_space=pltpu.SMEM),
                  pl.BlockSpec((2 * TM_OUT,), lambda i: (jnp.minimum(i + 1, steps - 1),), memory_space=pltpu.SMEM),
                  pl.BlockSpec((TM_OUT * ROW_TILE, LANES), tok),
                  pl.BlockSpec((TM_OUT, LANES), tok),
                  pl.BlockSpec(memory_space=pl.ANY),
                  pl.BlockSpec((1, D_MODEL), const),
                  pl.BlockSpec((1, D_MODEL), const),
                  pl.BlockSpec((TM_OUT, PLE_DIM), tok),
                  pl.BlockSpec((PLE_DIM, D_MODEL), const),
                  pl.BlockSpec((D_MODEL, D_MODEL), const)],
        out_specs=pl.BlockSpec((TM_OUT, D_MODEL), tok),
        out_shape=jax.ShapeDtypeStruct((n, D_MODEL), f32),
        scratch_shapes=[pltpu.VMEM((2, 2, TM_OUT * ROW_TILE, LANES), f32), pltpu.SemaphoreType.DMA((2,))],
        compiler_params=_cparams("arbitrary"), name="moe_combine_ln_ple",
    )(pos_flat, pos_flat, x1_rows, meta, y_rows, g, b, p2d, w_proj, w_gate)


def _routing_plan(meta_t, counts_row, n_tiles):
    i1 = meta_t[0].astype(jnp.int32)
    i2 = meta_t[1].astype(jnp.int32)
    r1 = meta_t[4].astype(jnp.int32)
    r2 = meta_t[5].astype(jnp.int32)
    counts = counts_row[0, :N_EXPERTS].astype(jnp.int32)
    tiles = (counts + TM_FFN - 1) // TM_FFN
    tile_end = jnp.cumsum(tiles)
    offset = (tile_end - tiles) * TM_FFN
    pos = jnp.stack([offset[i1] + r1, offset[i2] + r2], axis=1).reshape(-1)
    tile_expert = jnp.sum(jnp.arange(n_tiles)[:, None] >= tile_end[None, :], axis=1)
    tile_expert = jnp.minimum(tile_expert, N_EXPERTS - 1)
    last_tile = jnp.where(tiles > 0, tile_end - 1, -1)
    return (pos.astype(jnp.int32), tile_expert.astype(jnp.int32), tile_end[-1:].astype(jnp.int32),
            last_tile.astype(jnp.int32))


def kernel(x, p, positions, w_qkv, w_o, ln_mix_g, ln_mix_b, ln_ffn_g, ln_ffn_b, sinks, w_ffn_gate, w_ffn_up,
           w_ffn_down, w_router, w_exp_gate, w_exp_up, w_exp_down, w_ple_proj, w_ple_gate):
    batch, seq, _ = x.shape
    n = batch * seq
    assert seq % TQ_SWA == 0 and (seq // MOBA_BLOCK) % 8 == 0 and seq // MOBA_BLOCK <= LANES - HEAD_DIM
    tabs = _rope_tables(positions)
    x2d = x.reshape(n, D_MODEL)
    row = lambda v: v.reshape(1, D_MODEL)
    for i in range(DEPTH):
        j = i // 2
        q, k, v = _qkv(x2d, w_qkv[i].astype(bf16), tabs, batch, seq)
        if i % 2 == 0:
            a = _moba(q, k, v)
        else:
            a = _swa(q, k, v, sinks[j])
        a2d = a.reshape(n, D_MODEL)
        wo = w_o[i].astype(bf16)
        ple_args = (row(ln_ffn_g[i]), row(ln_ffn_b[i]), p[i].reshape(n, PLE_DIM),
                    w_ple_proj[i].astype(bf16), w_ple_gate[i].astype(bf16))
        if i % 2 == 0:
            x2d = _dense_tail(x2d, a2d, wo, row(ln_mix_g[i]), row(ln_mix_b[i]), w_ffn_gate[j].astype(bf16),
                              w_ffn_up[j].astype(bf16), w_ffn_down[j].astype(bf16), *ple_args)
        else:
            x1, meta, meta_t, counts = _mix_router(x2d, a2d, wo, row(ln_mix_g[i]), row(ln_mix_b[i]), w_router[j])
            n_tiles = 2 * n // TM_FFN + N_EXPERTS
            pos, tile_expert, n_active, last_tile = _routing_plan(meta_t, counts, n_tiles)
            xs = _dispatch(x1, pos, last_tile, n_active, n_tiles)
            y = _moe_ffn(xs, w_exp_gate[j], w_exp_up[j], w_exp_down[j], tile_expert, n_active)
            x2d = _out_moe(x1, meta, y, pos, *ple_args)
    return x2d.reshape(batch, seq, D_MODEL)
```

```python
import functools

import jax
import jax.numpy as jnp
import numpy as np
from jax import lax
from jax.experimental import pallas as pl
from jax.experimental.pallas import tpu as pltpu

D_MODEL = 1024
N_HEADS = 16
N_KV_HEADS = 4
HEAD_DIM = 64
GROUP = N_HEADS // N_KV_HEADS
QKV_DIM = (N_HEADS + 2 * N_KV_HEADS) * HEAD_DIM
ROT_DIM = 16
ROPE_THETA = 500000.0
MOBA_BLOCK = 256
MOBA_TOPK = 3
SWA_WINDOW = 128
D_FF = 3584
N_EXPERTS = 8
PLE_DIM = 256
LN_EPS = 1e-5
DEPTH = 2
ALPHA = (2.0 * DEPTH) ** 0.25
LOG2E = 1.4426950408889634
Q_SCALE = HEAD_DIM ** -0.5 * LOG2E

LANES = 128
ROW_TILE = D_MODEL // LANES
DMA_UNROLL = 8
NEG = -1e30
VMEM_LIMIT = 48 * 1024 * 1024

TM_QKV = 512
TM_MIX = 512
TM_DENSE = 1024
TM_FFN = 1024
FC_FFN = 512
TM_DISPATCH = 512
TM_OUT = 256
TQ_SWA = 512
SWA_KV_PER_STEP = 1
MOBA_SUB = 512
MOBA_KEYS = 2 * MOBA_BLOCK
MOBA_KV_PER_STEP = 2
MOBA_VMEM_LIMIT = 56 * 1024 * 1024

f32 = jnp.float32
bf16 = jnp.bfloat16


def _cparams(*sem):
    return pltpu.CompilerParams(dimension_semantics=sem, vmem_limit_bytes=VMEM_LIMIT)


def _rope_kernel(pos_ref, inv_ref, cos_ref, sin_ref):
    ang = pos_ref[...].astype(f32) * inv_ref[...]
    cos_ref[...] = jnp.cos(ang)
    sin_ref[...] = jnp.sin(ang)


def _rope_tables(positions):
    n = positions.size
    half = ROT_DIM // 2
    inv = 1.0 / (ROPE_THETA ** (jnp.arange(0, ROT_DIM, 2, dtype=f32) / ROT_DIM))
    rows = n * half // LANES
    pos_rep = jnp.repeat(positions.reshape(-1), half).reshape(rows, LANES)
    inv_rep = jnp.tile(inv, LANES // half).reshape(1, LANES)
    cos, sin = pl.pallas_call(
        _rope_kernel,
        out_shape=(jax.ShapeDtypeStruct((rows, LANES), f32),) * 2,
        name="rope_tables",
    )(pos_rep, inv_rep)
    cos = jnp.tile(cos.reshape(n, half), (1, LANES // half))
    sin = jnp.tile(sin.reshape(n, half), (1, LANES // half))
    d = jnp.arange(LANES) % HEAD_DIM
    c_tab = jnp.where(d < ROT_DIM, cos, 1.0)
    s_lo = jnp.where(d < half, -sin, 0.0)
    s_hi = jnp.where((d >= half) & (d < ROT_DIM), sin, 0.0)
    return c_tab, s_lo, s_hi


def _qkv_kernel(x_ref, w_ref, c_ref, slo_ref, shi_ref, q_ref, k_ref, v_ref, *, tiles_per_seq):
    tm = x_ref.shape[0]
    acc = jnp.dot(x_ref[...].astype(bf16), w_ref[...], preferred_element_type=f32)
    lane = lax.broadcasted_iota(jnp.int32, (tm, LANES), 1)
    row = lax.broadcasted_iota(jnp.int32, (tm, LANES), 0)
    low = lane < HEAD_DIM
    c_tab, s_lo, s_hi = c_ref[...], slo_ref[...], shi_ref[...]
    seq0 = (pl.program_id(0) % tiles_per_seq) * tm
    blk = (seq0 + row) // MOBA_BLOCK
    k_pad = jnp.where(lane == HEAD_DIM + blk, 1.0, 0.0)
    v_pad = jnp.where(lane == HEAD_DIM, 1.0, 0.0)
    n_q = N_HEADS // 2
    n_kv = N_KV_HEADS // 2
    for c in range(n_q + n_kv):
        xc = acc[:, c * LANES:(c + 1) * LANES]
        r = xc * c_tab + pltpu.roll(xc, LANES - ROT_DIM // 2, 1) * s_lo + pltpu.roll(xc, ROT_DIM // 2, 1) * s_hi
        r_odd = pltpu.roll(r, HEAD_DIM, 1)
        if c < n_q:
            q_ref[0, 2 * c] = jnp.where(low, r * Q_SCALE, 0.0).astype(bf16)
            q_ref[0, 2 * c + 1] = jnp.where(low, r_odd * Q_SCALE, 0.0).astype(bf16)
        else:
            k_ref[0, 2 * (c - n_q)] = jnp.where(low, r, k_pad).astype(bf16)
            k_ref[0, 2 * (c - n_q) + 1] = jnp.where(low, r_odd, k_pad).astype(bf16)
    for c in range(n_kv):
        xc = acc[:, (n_q + n_kv + c) * LANES:(n_q + n_kv + c + 1) * LANES]
        v_ref[0, 2 * c] = jnp.where(low, xc, v_pad).astype(bf16)
        v_ref[0, 2 * c + 1] = jnp.where(low, pltpu.roll(xc, HEAD_DIM, 1), v_pad).astype(bf16)


def _qkv(x2d, w_qkv, tabs, batch, seq):
    n = x2d.shape[0]
    nt = seq // TM_QKV
    tok = lambda i: (i, 0)
    head = lambda i: (i // nt, 0, i % nt, 0)
    return pl.pallas_call(
        functools.partial(_qkv_kernel, tiles_per_seq=nt),
        grid=(n // TM_QKV,),
        in_specs=[pl.BlockSpec((TM_QKV, D_MODEL), tok),
                  pl.BlockSpec((D_MODEL, QKV_DIM), lambda i: (0, 0)),
                  pl.BlockSpec((TM_QKV, LANES), tok),
                  pl.BlockSpec((TM_QKV, LANES), tok),
                  pl.BlockSpec((TM_QKV, LANES), tok)],
        out_specs=[pl.BlockSpec((1, N_HEADS, TM_QKV, LANES), head),
                   pl.BlockSpec((1, N_KV_HEADS, TM_QKV, LANES), head),
                   pl.BlockSpec((1, N_KV_HEADS, TM_QKV, LANES), head)],
        out_shape=[jax.ShapeDtypeStruct((batch, N_HEADS, seq, LANES), bf16),
                   jax.ShapeDtypeStruct((batch, N_KV_HEADS, seq, LANES), bf16),
                   jax.ShapeDtypeStruct((batch, N_KV_HEADS, seq, LANES), bf16)],
        compiler_params=_cparams("parallel"),
        name="qkv_rope",
    )(x2d, w_qkv, *tabs)


def _merge_heads(o, out_ref, rows):
    lane = lax.broadcasted_iota(jnp.int32, (rows, LANES), 1)
    low = lane < HEAD_DIM
    for c in range(o.shape[0] // rows // 2):
        even = o[(2 * c) * rows:(2 * c + 1) * rows]
        odd = o[(2 * c + 1) * rows:(2 * c + 2) * rows]
        out_ref[0, :, c * LANES:(c + 1) * LANES] = jnp.where(low, even, pltpu.roll(odd, HEAD_DIM, 1)).astype(out_ref.dtype)


def _moba_kernel(q_ref, k_ref, v_ref, o_ref, kmean_ref, vt_ref, qa_ref, s0_ref, s1_ref, m_ref, acc_ref):
    u = pl.program_id(2)
    blk = MOBA_BLOCK
    tq = q_ref.shape[2]
    nb = k_ref.shape[2] // blk
    n_grp = k_ref.shape[1]
    grp_rows = GROUP * tq
    rows = n_grp * grp_rows
    sub = MOBA_SUB
    n_sub = rows // sub
    nt_dims = (((1,), (1,)), ((), ()))

    @pl.when(u == 0)
    def _():
        for kv in range(n_grp):
            for j in range(nb):
                kj = k_ref[0, kv, j * blk:(j + 1) * blk, :].astype(f32)
                kmean_ref[kv, j:j + 1, :] = jnp.sum(kj, axis=0, keepdims=True) * (1.0 / blk)
                vj = v_ref[0, kv, j * blk:(j + 1) * blk, :].astype(f32)
                vt_ref[kv, :, j * blk:(j + 1) * blk] = vj.T.astype(bf16)

    for kv in range(n_grp):
        q = q_ref[0, kv * GROUP:(kv + 1) * GROUP].reshape(grp_rows, LANES)
        km = kmean_ref[kv]
        km_hi = km.astype(bf16)
        km_lo = (km - km_hi.astype(f32)).astype(bf16)
        g = (lax.dot_general(km_hi, q, nt_dims, preferred_element_type=f32)
             + lax.dot_general(km_lo, q, nt_dims, preferred_element_type=f32))
        jidx = lax.broadcasted_iota(jnp.int32, (nb, grp_rows), 0)
        q_blk = u * (tq // blk) + (lax.broadcasted_iota(jnp.int32, (nb, grp_rows), 1) % tq) // blk
        past = jidx < q_blk
        g = jnp.where(past, g, -jnp.inf)
        sel = jidx == q_blk
        for _ in range(MOBA_TOPK):
            mx = jnp.max(g, axis=0, keepdims=True)
            first = jnp.min(jnp.where(g == mx, jidx, nb), axis=0, keepdims=True)
            pick = jidx == first
            sel = jnp.logical_or(sel, jnp.logical_and(pick, past))
            g = jnp.where(pick, -jnp.inf, g)
        bias = jnp.where(sel, 0.0, NEG)
        bias_t = jnp.concatenate([jnp.zeros((HEAD_DIM, grp_rows), f32), bias,
                                  jnp.zeros((LANES - HEAD_DIM - nb, grp_rows), f32)], axis=0)
        qa_ref[kv * grp_rows:(kv + 1) * grp_rows] = (q.astype(f32) + bias_t.T).astype(bf16)

    m_ref[...] = jnp.full_like(m_ref, NEG)
    acc_ref[...] = jnp.zeros_like(acc_ref)

    def scores(c, s_ref, r):
        o = pl.multiple_of(c * MOBA_KEYS, MOBA_KEYS)
        rs = slice(r * sub, (r + 1) * sub)
        kv = r * sub // grp_rows
        s_ref[r] = lax.dot_general(k_ref[0, kv, pl.ds(o, MOBA_KEYS), :], qa_ref[rs], nt_dims,
                                   preferred_element_type=f32)

    def accumulate(c, s_ref, r, causal):
        o = pl.multiple_of(c * MOBA_KEYS, MOBA_KEYS)
        kv = r * sub // grp_rows
        s = s_ref[r]
        if causal:
            key = lax.broadcasted_iota(jnp.int32, (MOBA_KEYS, sub), 0)
            tok = (lax.broadcasted_iota(jnp.int32, (MOBA_KEYS, sub), 1) + r * sub) % tq
            s = jnp.where(jnp.logical_and(key // blk == tok // blk, key > tok), NEG, s)
        m_old = m_ref[r]
        m_new = jnp.maximum(m_old, jnp.max(s, axis=0, keepdims=True))
        p = jnp.exp2(s - m_new).astype(bf16)
        pv = jnp.dot(vt_ref[kv, :, pl.ds(o, MOBA_KEYS)], p, preferred_element_type=f32)
        acc_ref[r] = acc_ref[r] * jnp.exp2(m_old - m_new) + pv
        m_ref[r] = m_new

    def step(c_next, s_next, c_cur, s_cur, causal=False):
        for r in range(n_sub):
            if c_next is not None:
                scores(c_next, s_next, r)
            accumulate(c_cur, s_cur, r, causal)

    for r in range(n_sub):
        scores(0, s0_ref, r)

    def body(t, carry):
        step(2 * t + 1, s1_ref, 2 * t, s0_ref)
        step(2 * t + 2, s0_ref, 2 * t + 1, s1_ref)
        return carry

    lax.fori_loop(0, u // 2, body, 0)

    @pl.when(u % 2 == 1)
    def _():
        step(u, s1_ref, u - 1, s0_ref)
        step(None, None, u, s1_ref, causal=True)

    @pl.when(u % 2 == 0)
    def _():
        step(None, None, u, s0_ref, causal=True)

    heads = []
    for r in range(n_sub):
        a = acc_ref[r]
        heads.append((a / a[HEAD_DIM:HEAD_DIM + 1, :]).T)
    _merge_heads(jnp.concatenate(heads, axis=0), o_ref, tq)


def _moba(q, k, v):
    batch, _, seq, _ = q.shape
    nb = seq // MOBA_BLOCK
    n_grp = MOBA_KV_PER_STEP
    rows = n_grp * GROUP * MOBA_KEYS
    return pl.pallas_call(
        _moba_kernel,
        grid=(batch, N_KV_HEADS // n_grp, seq // MOBA_KEYS),
        in_specs=[pl.BlockSpec((1, n_grp * GROUP, MOBA_KEYS, LANES), lambda b, g, u: (b, g, u, 0)),
                  pl.BlockSpec((1, n_grp, seq, LANES), lambda b, g, u: (b, g, 0, 0)),
                  pl.BlockSpec((1, n_grp, seq, LANES), lambda b, g, u: (b, g, 0, 0))],
        out_specs=pl.BlockSpec((1, MOBA_KEYS, n_grp * GROUP * HEAD_DIM), lambda b, g, u: (b, u, g)),
        out_shape=jax.ShapeDtypeStruct((batch, seq, N_HEADS * HEAD_DIM), bf16),
        scratch_shapes=[pltpu.VMEM((n_grp, nb, LANES), f32),
                        pltpu.VMEM((n_grp, LANES, seq), bf16),
                        pltpu.VMEM((rows, LANES), bf16),
                        pltpu.VMEM((rows // MOBA_SUB, MOBA_KEYS, MOBA_SUB), f32),
                        pltpu.VMEM((rows // MOBA_SUB, MOBA_KEYS, MOBA_SUB), f32),
                        pltpu.VMEM((rows // MOBA_SUB, 1, MOBA_SUB), f32),
                        pltpu.VMEM((rows // MOBA_SUB, LANES, MOBA_SUB), f32)],
        compiler_params=pltpu.CompilerParams(dimension_semantics=("parallel", "parallel", "arbitrary"),
                                             vmem_limit_bytes=MOBA_VMEM_LIMIT),
        name="moba_attention",
    )(q, k, v)


def _swa_kernel(sink_ref, q_ref, kc_ref, kp_ref, vc_ref, vp_ref, o_ref):
    g = pl.program_id(1)
    i = pl.program_id(2)
    w = SWA_WINDOW
    n_grp = kc_ref.shape[1]
    rows = GROUP * w
    nt_dims = (((1,), (1,)), ((), ()))
    n_sb = q_ref.shape[2] // w
    t_in = lax.broadcasted_iota(jnp.int32, (rows, 2 * w), 0) % w
    col = lax.broadcasted_iota(jnp.int32, (rows, 2 * w), 1)
    band = jnp.logical_and(col > t_in, col <= t_in + w)
    first = jnp.logical_and(band, jnp.logical_or(col >= w, i > 0))
    head_of_row = lax.broadcasted_iota(jnp.int32, (rows, 1), 0) // w
    lane = lax.broadcasted_iota(jnp.int32, (w, LANES), 1)
    low = lane < HEAD_DIM

    def window(ref_cur, ref_prev, kv, sb):
        if sb == 0:
            return jnp.concatenate([ref_prev[0, kv], ref_cur[0, kv, 0:w, :]], axis=0)
        return ref_cur[0, kv, (sb - 1) * w:(sb + 1) * w, :]

    work = [(kv, sb) for kv in range(n_grp) for sb in range(n_sb)]
    scores = [lax.dot_general(q_ref[0, kv * GROUP:(kv + 1) * GROUP, sb * w:(sb + 1) * w, :].reshape(rows, LANES),
                              window(kc_ref, kp_ref, kv, sb), nt_dims, preferred_element_type=f32)
              for kv, sb in work]
    for (kv, sb), sc in zip(work, scores):
        sink = jnp.zeros((rows, 1), f32)
        for hh in range(GROUP):
            sink = jnp.where(head_of_row == hh, sink_ref[(g * n_grp + kv) * GROUP + hh] * LOG2E, sink)
        s = jnp.where(first if sb == 0 else band, sc, NEG)
        m = jnp.maximum(jnp.max(s, axis=1, keepdims=True), sink)
        p = jnp.exp2(s - m).astype(bf16)
        acc = jnp.dot(p, window(vc_ref, vp_ref, kv, sb), preferred_element_type=f32)
        out = acc / (acc[:, HEAD_DIM:HEAD_DIM + 1] + jnp.exp2(sink - m))
        for c in range(GROUP // 2):
            even = out[(2 * c) * w:(2 * c + 1) * w]
            odd = out[(2 * c + 1) * w:(2 * c + 2) * w]
            chunk = kv * (GROUP // 2) + c
            o_ref[0, sb * w:(sb + 1) * w, chunk * LANES:(chunk + 1) * LANES] = jnp.where(
                low, even, pltpu.roll(odd, HEAD_DIM, 1)).astype(o_ref.dtype)


def _swa(q, k, v, sinks):
    batch, _, seq, _ = q.shape
    per = TQ_SWA // SWA_WINDOW
    n_grp = SWA_KV_PER_STEP
    cur = lambda b, g, i, s: (b, g, i, 0)
    prev = lambda b, g, i, s: (b, g, jnp.maximum(i * per - 1, 0), 0)
    grid_spec = pltpu.PrefetchScalarGridSpec(
        num_scalar_prefetch=1,
        grid=(batch, N_KV_HEADS // n_grp, seq // TQ_SWA),
        in_specs=[pl.BlockSpec((1, n_grp * GROUP, TQ_SWA, LANES), cur),
                  pl.BlockSpec((1, n_grp, TQ_SWA, LANES), cur),
                  pl.BlockSpec((1, n_grp, SWA_WINDOW, LANES), prev),
                  pl.BlockSpec((1, n_grp, TQ_SWA, LANES), cur),
                  pl.BlockSpec((1, n_grp, SWA_WINDOW, LANES), prev)],
        out_specs=pl.BlockSpec((1, TQ_SWA, n_grp * GROUP * HEAD_DIM), lambda b, g, i, s: (b, i, g)),
    )
    return pl.pallas_call(
        _swa_kernel,
        grid_spec=grid_spec,
        out_shape=jax.ShapeDtypeStruct((batch, seq, N_HEADS * HEAD_DIM), bf16),
        compiler_params=_cparams("parallel", "parallel", "arbitrary"),
        name="swa_attention",
    )(sinks, q, k, k, v, v)


def _layer_norm(x, g, b):
    mu = jnp.mean(x, axis=-1, keepdims=True)
    xc = x - mu
    var = jnp.mean(xc * xc, axis=-1, keepdims=True)
    return xc * lax.rsqrt(var + LN_EPS) * g + b


def _load_rows(ref, n_rows):
    return jnp.concatenate([ref[pl.ds(k, n_rows, stride=ROW_TILE), :] for k in range(ROW_TILE)], axis=1)


def _store_rows(ref, val):
    for k in range(ROW_TILE):
        ref[pl.ds(k, val.shape[0], stride=ROW_TILE), :] = val[:, k * LANES:(k + 1) * LANES]


def _mix_router_kernel(x_ref, a_ref, wo_ref, g_ref, b_ref, wr_ref, x1_ref, meta_ref, metat_ref, cnt_ref):
    tm = x_ref.shape[0]
    y = ALPHA * x_ref[...] + jnp.dot(a_ref[...], wo_ref[...], preferred_element_type=f32)
    x1 = _layer_norm(y, g_ref[...], b_ref[...])
    _store_rows(x1_ref, x1)

    @pl.when(pl.program_id(0) == 0)
    def _():
        cnt_ref[...] = jnp.zeros_like(cnt_ref)

    logits = jnp.dot(x1, wr_ref[...], preferred_element_type=f32)
    lane = lax.broadcasted_iota(jnp.int32, (tm, LANES), 1)
    lg = jnp.where(lane < N_EXPERTS, logits, -jnp.inf)
    m1 = jnp.max(lg, axis=1, keepdims=True)
    i1 = jnp.min(jnp.where(lg == m1, lane, LANES), axis=1, keepdims=True)
    lg2 = jnp.where(lane == i1, -jnp.inf, lg)
    m2 = jnp.max(lg2, axis=1, keepdims=True)
    i2 = jnp.min(jnp.where(lg2 == m2, lane, LANES), axis=1, keepdims=True)
    e2 = jnp.exp(m2 - m1)
    w1 = 1.0 / (1.0 + e2)
    w2 = e2 / (1.0 + e2)
    hit1 = lane == i1
    hit2 = lane == i2
    hits = jnp.where(jnp.logical_or(hit1, hit2), 1.0, 0.0)
    r_i = lax.broadcasted_iota(jnp.int32, (tm, tm), 0)
    c_i = lax.broadcasted_iota(jnp.int32, (tm, tm), 1)
    before = jnp.where(c_i < r_i, 1.0, 0.0).astype(bf16)
    cum = jnp.dot(before, hits.astype(bf16), preferred_element_type=f32) + cnt_ref[0:1, :]
    r1 = jnp.sum(jnp.where(hit1, cum, 0.0), axis=1, keepdims=True)
    r2 = jnp.sum(jnp.where(hit2, cum, 0.0), axis=1, keepdims=True)
    cnt_ref[...] = cnt_ref[...] + jnp.sum(hits, axis=0, keepdims=True)
    meta = jnp.where(lane == 0, i1.astype(f32), 0.0)
    meta = jnp.where(lane == 1, i2.astype(f32), meta)
    meta = jnp.where(lane == 2, w1, meta)
    meta = jnp.where(lane == 3, w2, meta)
    meta = jnp.where(lane == 4, r1, meta)
    meta = jnp.where(lane == 5, r2, meta)
    meta_ref[...] = meta
    metat_ref[...] = meta.T[0:8, :]


def _mix_router(x2d, a2d, w_o, g, b, w_router):
    n = x2d.shape[0]
    tok = lambda i: (i, 0)
    const = lambda i: (0, 0)
    wr = jnp.zeros((D_MODEL, LANES), f32).at[:, :N_EXPERTS].set(w_router)
    return pl.pallas_call(
        _mix_router_kernel, grid=(n // TM_MIX,),
        in_specs=[pl.BlockSpec((TM_MIX, D_MODEL), tok),
                  pl.BlockSpec((TM_MIX, D_MODEL), tok),
                  pl.BlockSpec((D_MODEL, D_MODEL), const),
                  pl.BlockSpec((1, D_MODEL), const),
                  pl.BlockSpec((1, D_MODEL), const),
                  pl.BlockSpec((D_MODEL, LANES), const)],
        out_specs=[pl.BlockSpec((TM_MIX * ROW_TILE, LANES), tok),
                   pl.BlockSpec((TM_MIX, LANES), tok), pl.BlockSpec((8, TM_MIX), lambda i: (0, i)),
                   pl.BlockSpec((8, LANES), const)],
        out_shape=[jax.ShapeDtypeStruct((n * ROW_TILE, LANES), f32),
                   jax.ShapeDtypeStruct((n, LANES), f32), jax.ShapeDtypeStruct((8, n), f32),
                   jax.ShapeDtypeStruct((8, LANES), f32)],
        compiler_params=_cparams("arbitrary"), name="mix_ln_router",
    )(x2d, a2d, w_o, g, b, wr)


def _swiglu_accumulate(acc_ref, xb, wg, wu, wd, fc):
    gate = jnp.dot(xb, wg.astype(bf16), preferred_element_type=f32)
    up = jnp.dot(xb, wu.astype(bf16), preferred_element_type=f32)
    h = (gate / (1.0 + jnp.exp(-gate)) * up).astype(bf16)
    part = jnp.dot(h, wd.astype(bf16), preferred_element_type=f32)

    @pl.when(fc == 0)
    def _():
        acc_ref[...] = part

    @pl.when(fc > 0)
    def _():
        acc_ref[...] = acc_ref[...] + part


def _moe_ffn_kernel(te_ref, nact_ref, x_ref, wg_ref, wu_ref, wd_ref, y_ref, xb_ref, acc_ref):
    n = pl.program_id(0)
    fc = pl.program_id(1)
    active = n < nact_ref[0]

    @pl.when(jnp.logical_and(jnp.logical_not(active), fc == 0))
    def _():
        y_ref[...] = jnp.zeros_like(y_ref)

    @pl.when(active)
    def _():
        @pl.when(fc == 0)
        def _():
            xb_ref[...] = _load_rows(x_ref, TM_FFN).astype(bf16)

        _swiglu_accumulate(acc_ref, xb_ref[...], wg_ref[0], wu_ref[0], wd_ref[0], fc)

        @pl.when(fc == pl.num_programs(1) - 1)
        def _():
            _store_rows(y_ref, acc_ref[...])


def _moe_ffn(xs_rows, w_gate, w_up, w_down, tile_expert, n_active):
    blk = (TM_FFN * ROW_TILE, LANES)
    n_tiles = xs_rows.shape[0] // blk[0]
    n_fc = D_FF // FC_FFN

    def row(n, f, te, na):
        return jnp.minimum(n, na[0] - 1), 0

    def w_in(n, f, te, na):
        return te[jnp.minimum(n, na[0] - 1)], 0, jnp.where(n < na[0], f, n_fc - 1)

    def w_out(n, f, te, na):
        return te[jnp.minimum(n, na[0] - 1)], jnp.where(n < na[0], f, n_fc - 1), 0

    grid_spec = pltpu.PrefetchScalarGridSpec(
        num_scalar_prefetch=2,
        grid=(n_tiles, n_fc),
        in_specs=[pl.BlockSpec(blk, row),
                  pl.BlockSpec((1, D_MODEL, FC_FFN), w_in),
                  pl.BlockSpec((1, D_MODEL, FC_FFN), w_in),
                  pl.BlockSpec((1, FC_FFN, D_MODEL), w_out)],
        out_specs=pl.BlockSpec(blk, lambda n, f, te, na: (n, 0)),
        scratch_shapes=[pltpu.VMEM((TM_FFN, D_MODEL), bf16), pltpu.VMEM((TM_FFN, D_MODEL), f32)],
    )
    return pl.pallas_call(
        _moe_ffn_kernel, grid_spec=grid_spec,
        out_shape=jax.ShapeDtypeStruct(xs_rows.shape, f32),
        compiler_params=_cparams("arbitrary", "arbitrary"), name="moe_swiglu",
    )(tile_expert, n_active, xs_rows, w_gate, w_up, w_down)


def _dense_tail_kernel(x_ref, a_ref, wo_ref, g1_ref, b1_ref, wg_ref, wu_ref, wd_ref, g2_ref, b2_ref,
                       p_ref, wp_ref, wgate_ref, o_ref, x1_ref, xb_ref):
    fc = pl.program_id(1)
    acc_ref = o_ref

    @pl.when(fc == 0)
    def _():
        y = ALPHA * x_ref[...] + jnp.dot(a_ref[...], wo_ref[...], preferred_element_type=f32)
        x1 = _layer_norm(y, g1_ref[...], b1_ref[...])
        x1_ref[...] = x1
        xb_ref[...] = x1.astype(bf16)

    _swiglu_accumulate(acc_ref, xb_ref[...], wg_ref[...], wu_ref[...], wd_ref[...], fc)

    @pl.when(fc == pl.num_programs(1) - 1)
    def _():
        x2 = _layer_norm(ALPHA * x1_ref[...] + acc_ref[...], g2_ref[...], b2_ref[...])
        _ple(x2, p_ref, wp_ref, wgate_ref, o_ref)


def _dense_tail(x2d, a2d, w_o, g1, b1, w_gate, w_up, w_down, g2, b2, p2d, w_proj, w_pgate):
    n = x2d.shape[0]
    tok = lambda i, f: (i, 0)
    const = lambda i, f: (0, 0)
    once = dict(pipeline_mode=pl.Buffered(1))
    return pl.pallas_call(
        _dense_tail_kernel, grid=(n // TM_DENSE, D_FF // FC_FFN),
        in_specs=[pl.BlockSpec((TM_DENSE, D_MODEL), tok),
                  pl.BlockSpec((TM_DENSE, D_MODEL), tok),
                  pl.BlockSpec((D_MODEL, D_MODEL), const, **once),
                  pl.BlockSpec((1, D_MODEL), const, **once),
                  pl.BlockSpec((1, D_MODEL), const, **once),
                  pl.BlockSpec((D_MODEL, FC_FFN), lambda i, f: (0, f)),
                  pl.BlockSpec((D_MODEL, FC_FFN), lambda i, f: (0, f)),
                  pl.BlockSpec((FC_FFN, D_MODEL), lambda i, f: (f, 0)),
                  pl.BlockSpec((1, D_MODEL), const, **once),
                  pl.BlockSpec((1, D_MODEL), const, **once),
                  pl.BlockSpec((TM_DENSE, PLE_DIM), tok),
                  pl.BlockSpec((PLE_DIM, D_MODEL), const, **once),
                  pl.BlockSpec((D_MODEL, D_MODEL), const, **once)],
        out_specs=pl.BlockSpec((TM_DENSE, D_MODEL), tok),
        out_shape=jax.ShapeDtypeStruct((n, D_MODEL), f32),
        scratch_shapes=[pltpu.VMEM((TM_DENSE, D_MODEL), f32), pltpu.VMEM((TM_DENSE, D_MODEL), bf16)],
        compiler_params=pltpu.CompilerParams(dimension_semantics=("parallel", "arbitrary"),
                                             vmem_limit_bytes=MOBA_VMEM_LIMIT), name="dense_tail",
    )(x2d, a2d, w_o, g1, b1, w_gate, w_up, w_down, g2, b2, p2d, w_proj, w_pgate)


def _dispatch_kernel(last_ref, nact_ref, pos_ref, x_ref, xs_ref, zero_ref, sem, zsem):
    tm = x_ref.shape[0] // ROW_TILE
    n_tiles = xs_ref.shape[0] // zero_ref.shape[0]

    @pl.when(pl.program_id(0) == 0)
    def _():
        zero_ref[...] = jnp.zeros_like(zero_ref)

        def fill(t):
            dst = pl.multiple_of(t * zero_ref.shape[0], zero_ref.shape[0])
            return pltpu.make_async_copy(zero_ref, xs_ref.at[pl.ds(dst, zero_ref.shape[0])], zsem)

        def start_tail(t, _):
            fill(t).start()
            return 0

        def wait_tail(t, _):
            fill(t).wait()
            return 0

        for e in range(N_EXPERTS):
            pl.when(last_ref[e] >= 0)(lambda e=e: fill(last_ref[e]).start())
        lax.fori_loop(nact_ref[0], n_tiles, start_tail, 0)
        for e in range(N_EXPERTS):
            pl.when(last_ref[e] >= 0)(lambda e=e: fill(last_ref[e]).wait())
        lax.fori_loop(nact_ref[0], n_tiles, wait_tail, 0)

    def start(r, _):
        src = x_ref.at[pl.ds(pl.multiple_of(r * ROW_TILE, ROW_TILE), ROW_TILE)]
        for k in range(2):
            dst = pl.multiple_of(pos_ref[2 * r + k] * ROW_TILE, ROW_TILE)
            pltpu.make_async_copy(src, xs_ref.at[pl.ds(dst, ROW_TILE)], sem).start(priority=k)
        return 0

    lax.fori_loop(0, tm, start, 0, unroll=DMA_UNROLL)
    for _ in range(2):
        pltpu.make_async_copy(x_ref, xs_ref.at[pl.ds(0, tm * ROW_TILE)], sem).wait()


def _dispatch(x1_rows, pos_flat, last_tile, n_active, n_tiles):
    n = x1_rows.shape[0] // ROW_TILE
    grid_spec = pltpu.PrefetchScalarGridSpec(
        num_scalar_prefetch=2,
        grid=(n // TM_DISPATCH,),
        in_specs=[pl.BlockSpec((2 * TM_DISPATCH,), lambda i, lt, na: (i,), memory_space=pltpu.SMEM),
                  pl.BlockSpec((TM_DISPATCH * ROW_TILE, LANES), lambda i, lt, na: (i, 0))],
        out_specs=pl.BlockSpec(memory_space=pl.ANY),
        scratch_shapes=[pltpu.VMEM((TM_FFN * ROW_TILE, LANES), f32), pltpu.SemaphoreType.DMA(()),
                        pltpu.SemaphoreType.DMA(())],
    )
    return pl.pallas_call(
        _dispatch_kernel, grid_spec=grid_spec,
        out_shape=jax.ShapeDtypeStruct((n_tiles * TM_FFN * ROW_TILE, LANES), f32),
        compiler_params=_cparams("arbitrary"), name="moe_dispatch",
    )(last_tile, n_active, pos_flat, x1_rows)


def _ple(x2, p_ref, wp_ref, wgate_ref, o_ref):
    pe = jnp.dot(p_ref[...].astype(bf16), wp_ref[...], preferred_element_type=f32)
    z = jnp.dot(x2.astype(bf16), wgate_ref[...], preferred_element_type=f32)
    o_ref[...] = x2 + pe / (1.0 + jnp.exp(-z))


def _out_moe_kernel(pos_ref, nxt_ref, x1_ref, meta_ref, y_ref, g_ref, b_ref, p_ref, wp_ref, wgate_ref, o_ref,
                    buf_ref, sems):
    i = pl.program_id(0)
    tm = x1_ref.shape[0] // ROW_TILE
    slot = i % 2

    def gather(idx_ref, to_slot):
        def start(r, _):
            dst = pl.ds(pl.multiple_of(r * ROW_TILE, ROW_TILE), ROW_TILE)
            for k in range(2):
                src = pl.multiple_of(idx_ref[2 * r + k] * ROW_TILE, ROW_TILE)
                pltpu.make_async_copy(y_ref.at[pl.ds(src, ROW_TILE)], buf_ref.at[to_slot, k, dst],
                                      sems.at[to_slot]).start(priority=k)
            return 0

        lax.fori_loop(0, tm, start, 0, unroll=DMA_UNROLL)

    @pl.when(i == 0)
    def _():
        gather(pos_ref, slot)

    @pl.when(i + 1 < pl.num_programs(0))
    def _():
        gather(nxt_ref, 1 - slot)

    for k in range(2):
        pltpu.make_async_copy(y_ref.at[pl.ds(0, tm * ROW_TILE)], buf_ref.at[slot, k], sems.at[slot]).wait()
    meta = meta_ref[...]
    f = meta[:, 2:3] * _load_rows(buf_ref.at[slot, 0], tm) + meta[:, 3:4] * _load_rows(buf_ref.at[slot, 1], tm)
    x2 = _layer_norm(ALPHA * _load_rows(x1_ref, tm) + f, g_ref[...], b_ref[...])
    _ple(x2, p_ref, wp_ref, wgate_ref, o_ref)


def _out_moe(x1_rows, meta, y_rows, pos_flat, g, b, p2d, w_proj, w_gate):
    n = x1_rows.shape[0] // ROW_TILE
    steps = n // TM_OUT
    tok = lambda i: (i, 0)
    const = lambda i: (0, 0)
    return pl.pallas_call(
        _out_moe_kernel, grid=(steps,),
        in_specs=[pl.BlockSpec((2 * TM_OUT,), lambda i: (i,), memory_space=pltpu.SMEM),
                  pl.BlockSpec((2 * TM_OUT,), lambda i: (jnp.minimum(i + 1, steps - 1),), memory_space=pltpu.SMEM),
                  pl.BlockSpec((TM_OUT * ROW_TILE, LANES), tok),
                  pl.BlockSpec((TM_OUT, LANES), tok),
                  pl.BlockSpec(memory_space=pl.ANY),
                  pl.BlockSpec((1, D_MODEL), const),
                  pl.BlockSpec((1, D_MODEL), const),
                  pl.BlockSpec((TM_OUT, PLE_DIM), tok),
                  pl.BlockSpec((PLE_DIM, D_MODEL), const),
                  pl.BlockSpec((D_MODEL, D_MODEL), const)],
        out_specs=pl.BlockSpec((TM_OUT, D_MODEL), tok),
        out_shape=jax.ShapeDtypeStruct((n, D_MODEL), f32),
        scratch_shapes=[pltpu.VMEM((2, 2, TM_OUT * ROW_TILE, LANES), f32), pltpu.SemaphoreType.DMA((2,))],
        compiler_params=_cparams("arbitrary"), name="moe_combine_ln_ple",
    )(pos_flat, pos_flat, x1_rows, meta, y_rows, g, b, p2d, w_proj, w_gate)


def _routing_plan(meta_t, counts_row, n_tiles):
    i1 = meta_t[0].astype(jnp.int32)
    i2 = meta_t[1].astype(jnp.int32)
    r1 = meta_t[4].astype(jnp.int32)
    r2 = meta_t[5].astype(jnp.int32)
    counts = counts_row[0, :N_EXPERTS].astype(jnp.int32)
    tiles = (counts + TM_FFN - 1) // TM_FFN
    tile_end = jnp.cumsum(tiles)
    offset = (tile_end - tiles) * TM_FFN
    pos = jnp.stack([offset[i1] + r1, offset[i2] + r2], axis=1).reshape(-1)
    tile_expert = jnp.sum(jnp.arange(n_tiles)[:, None] >= tile_end[None, :], axis=1)
    tile_expert = jnp.minimum(tile_expert, N_EXPERTS - 1)
    last_tile = jnp.where(tiles > 0, tile_end - 1, -1)
    return (pos.astype(jnp.int32), tile_expert.astype(jnp.int32), tile_end[-1:].astype(jnp.int32),
            last_tile.astype(jnp.int32))


def kernel(x, p, positions, w_qkv, w_o, ln_mix_g, ln_mix_b, ln_ffn_g, ln_ffn_b, sinks, w_ffn_gate, w_ffn_up,
           w_ffn_down, w_router, w_exp_gate, w_exp_up, w_exp_down, w_ple_proj, w_ple_gate):
    batch, seq, _ = x.shape
    n = batch * seq
    assert seq % TQ_SWA == 0 and (seq // MOBA_BLOCK) % 8 == 0 and seq // MOBA_BLOCK <= LANES - HEAD_DIM
    tabs = _rope_tables(positions)
    x2d = x.reshape(n, D_MODEL)
    row = lambda v: v.reshape(1, D_MODEL)
    for i in range(DEPTH):
        j = i // 2
        q, k, v = _qkv(x2d, w_qkv[i].astype(bf16), tabs, batch, seq)
        if i % 2 == 0:
            a = _moba(q, k, v)
        else:
            a = _swa(q, k, v, sinks[j])
        a2d = a.reshape(n, D_MODEL)
        wo = w_o[i].astype(bf16)
        ple_args = (row(ln_ffn_g[i]), row(ln_ffn_b[i]), p[i].reshape(n, PLE_DIM),
                    w_ple_proj[i].astype(bf16), w_ple_gate[i].astype(bf16))
        if i % 2 == 0:
            x2d = _dense_tail(x2d, a2d, wo, row(ln_mix_g[i]), row(ln_mix_b[i]), w_ffn_gate[j].astype(bf16),
                              w_ffn_up[j].astype(bf16), w_ffn_down[j].astype(bf16), *ple_args)
        else:
            x1, meta, meta_t, counts = _mix_router(x2d, a2d, wo, row(ln_mix_g[i]), row(ln_mix_b[i]), w_router[j])
            n_tiles = 2 * n // TM_FFN + N_EXPERTS
            pos, tile_expert, n_active, last_tile = _routing_plan(meta_t, counts, n_tiles)
            xs = _dispatch(x1, pos, last_tile, n_active, n_tiles)
            y = _moe_ffn(xs, w_exp_gate[j], w_exp_up[j], w_exp_down[j], tile_expert, n_active)
            x2d = _out_moe(x1, meta, y, pos, *ple_args)
    return x2d.reshape(batch, seq, D_MODEL)
```

```python
import functools

import jax
import jax.numpy as jnp
import numpy as np
from jax import lax
from jax.experimental import pallas as pl
from jax.experimental.pallas import tpu as pltpu

D_MODEL = 1024
N_HEADS = 16
N_KV_HEADS = 4
HEAD_DIM = 64
GROUP = N_HEADS // N_KV_HEADS
QKV_DIM = (N_HEADS + 2 * N_KV_HEADS) * HEAD_DIM
ROT_DIM = 16
ROPE_THETA = 500000.0
MOBA_BLOCK = 256
MOBA_TOPK = 3
SWA_WINDOW = 128
D_FF = 3584
N_EXPERTS = 8
PLE_DIM = 256
LN_EPS = 1e-5
DEPTH = 2
ALPHA = (2.0 * DEPTH) ** 0.25
LOG2E = 1.4426950408889634
Q_SCALE = HEAD_DIM ** -0.5 * LOG2E

LANES = 128
ROW_TILE = D_MODEL // LANES
DMA_UNROLL = 8
NEG = -1e30
VMEM_LIMIT = 48 * 1024 * 1024

TM_QKV = 512
TM_MIX = 512
TM_DENSE = 1024
TM_FFN = 1024
FC_FFN = 512
TM_DISPATCH = 512
TM_OUT = 512
TQ_SWA = 1024
SWA_KV_PER_STEP = 2
MOBA_SUB = 512
MOBA_KEYS = 2 * MOBA_BLOCK
VT_ROWS = 80
MOBA_KV_PER_STEP = 2
MOBA_VMEM_LIMIT = 56 * 1024 * 1024

f32 = jnp.float32
bf16 = jnp.bfloat16


def _cparams(*sem):
    return pltpu.CompilerParams(dimension_semantics=sem, vmem_limit_bytes=VMEM_LIMIT)


def _rope_kernel(pos_ref, inv_ref, cos_ref, sin_ref):
    ang = pos_ref[...].astype(f32) * inv_ref[...]
    cos_ref[...] = jnp.cos(ang)
    sin_ref[...] = jnp.sin(ang)


def _rope_tables(positions):
    n = positions.size
    half = ROT_DIM // 2
    inv = 1.0 / (ROPE_THETA ** (jnp.arange(0, ROT_DIM, 2, dtype=f32) / ROT_DIM))
    rows = n * half // LANES
    pos_rep = jnp.repeat(positions.reshape(-1), half).reshape(rows, LANES)
    inv_rep = jnp.tile(inv, LANES // half).reshape(1, LANES)
    cos, sin = pl.pallas_call(
        _rope_kernel,
        out_shape=(jax.ShapeDtypeStruct((rows, LANES), f32),) * 2,
        name="rope_tables",
    )(pos_rep, inv_rep)
    cos = jnp.tile(cos.reshape(n, half), (1, LANES // half))
    sin = jnp.tile(sin.reshape(n, half), (1, LANES // half))
    d = jnp.arange(LANES) % HEAD_DIM
    c_tab = jnp.where(d < ROT_DIM, cos, 1.0)
    s_lo = jnp.where(d < half, -sin, 0.0)
    s_hi = jnp.where((d >= half) & (d < ROT_DIM), sin, 0.0)
    return c_tab, s_lo, s_hi


def _qkv_kernel(x_ref, w_ref, c_ref, slo_ref, shi_ref, q_ref, k_ref, v_ref, *, tiles_per_seq):
    tm = x_ref.shape[0]
    acc = jnp.dot(x_ref[...].astype(bf16), w_ref[...], preferred_element_type=f32)
    lane = lax.broadcasted_iota(jnp.int32, (tm, LANES), 1)
    row = lax.broadcasted_iota(jnp.int32, (tm, LANES), 0)
    low = lane < HEAD_DIM
    c_tab, s_lo, s_hi = c_ref[...], slo_ref[...], shi_ref[...]
    seq0 = (pl.program_id(0) % tiles_per_seq) * tm
    blk = (seq0 + row) // MOBA_BLOCK
    k_pad = jnp.where(lane == HEAD_DIM + blk, 1.0, 0.0)
    v_pad = jnp.where(lane == HEAD_DIM, 1.0, 0.0)
    n_q = N_HEADS // 2
    n_kv = N_KV_HEADS // 2
    for c in range(n_q + n_kv):
        xc = acc[:, c * LANES:(c + 1) * LANES]
        r = xc * c_tab + pltpu.roll(xc, LANES - ROT_DIM // 2, 1) * s_lo + pltpu.roll(xc, ROT_DIM // 2, 1) * s_hi
        r_odd = pltpu.roll(r, HEAD_DIM, 1)
        if c < n_q:
            q_ref[0, 2 * c] = jnp.where(low, r * Q_SCALE, 0.0).astype(bf16)
            q_ref[0, 2 * c + 1] = jnp.where(low, r_odd * Q_SCALE, 0.0).astype(bf16)
        else:
            k_ref[0, 2 * (c - n_q)] = jnp.where(low, r, k_pad).astype(bf16)
            k_ref[0, 2 * (c - n_q) + 1] = jnp.where(low, r_odd, k_pad).astype(bf16)
    for c in range(n_kv):
        xc = acc[:, (n_q + n_kv + c) * LANES:(n_q + n_kv + c + 1) * LANES]
        v_ref[0, 2 * c] = jnp.where(low, xc, v_pad).astype(bf16)
        v_ref[0, 2 * c + 1] = jnp.where(low, pltpu.roll(xc, HEAD_DIM, 1), v_pad).astype(bf16)


def _qkv(x2d, w_qkv, tabs, batch, seq):
    n = x2d.shape[0]
    nt = seq // TM_QKV
    tok = lambda i: (i, 0)
    head = lambda i: (i // nt, 0, i % nt, 0)
    return pl.pallas_call(
        functools.partial(_qkv_kernel, tiles_per_seq=nt),
        grid=(n // TM_QKV,),
        in_specs=[pl.BlockSpec((TM_QKV, D_MODEL), tok),
                  pl.BlockSpec((D_MODEL, QKV_DIM), lambda i: (0, 0)),
                  pl.BlockSpec((TM_QKV, LANES), tok),
                  pl.BlockSpec((TM_QKV, LANES), tok),
                  pl.BlockSpec((TM_QKV, LANES), tok)],
        out_specs=[pl.BlockSpec((1, N_HEADS, TM_QKV, LANES), head),
                   pl.BlockSpec((1, N_KV_HEADS, TM_QKV, LANES), head),
                   pl.BlockSpec((1, N_KV_HEADS, TM_QKV, LANES), head)],
        out_shape=[jax.ShapeDtypeStruct((batch, N_HEADS, seq, LANES), bf16),
                   jax.ShapeDtypeStruct((batch, N_KV_HEADS, seq, LANES), bf16),
                   jax.ShapeDtypeStruct((batch, N_KV_HEADS, seq, LANES), bf16)],
        compiler_params=_cparams("parallel"),
        name="qkv_rope",
    )(x2d, w_qkv, *tabs)


def _merge_heads(o, out_ref, rows):
    lane = lax.broadcasted_iota(jnp.int32, (rows, LANES), 1)
    low = lane < HEAD_DIM
    for c in range(o.shape[0] // rows // 2):
        even = o[(2 * c) * rows:(2 * c + 1) * rows]
        odd = o[(2 * c + 1) * rows:(2 * c + 2) * rows]
        out_ref[0, :, c * LANES:(c + 1) * LANES] = jnp.where(low, even, pltpu.roll(odd, HEAD_DIM, 1)).astype(out_ref.dtype)


def _moba_kernel(q_ref, k_ref, v_ref, o_ref, kmean_ref, vt_ref, qa_ref, s0_ref, s1_ref, m_ref, acc_ref):
    u = pl.program_id(2)
    blk = MOBA_BLOCK
    tq = q_ref.shape[2]
    nb = k_ref.shape[2] // blk
    n_grp = k_ref.shape[1]
    grp_rows = GROUP * tq
    rows = n_grp * grp_rows
    sub = MOBA_SUB
    n_sub = rows // sub
    nt_dims = (((1,), (1,)), ((), ()))

    @pl.when(u == 0)
    def _():
        for kv in range(n_grp):
            for j in range(nb):
                kj = k_ref[0, kv, j * blk:(j + 1) * blk, :].astype(f32)
                kmean_ref[kv, j:j + 1, :] = jnp.sum(kj, axis=0, keepdims=True) * (1.0 / blk)
                vj = v_ref[0, kv, j * blk:(j + 1) * blk, :].astype(f32)
                vt_ref[kv, :, j * blk:(j + 1) * blk] = vj.T[0:VT_ROWS].astype(bf16)

    for kv in range(n_grp):
        q = q_ref[0, kv * GROUP:(kv + 1) * GROUP].reshape(grp_rows, LANES)
        km = kmean_ref[kv]
        km_hi = km.astype(bf16)
        km_lo = (km - km_hi.astype(f32)).astype(bf16)
        g = (lax.dot_general(km_hi, q, nt_dims, preferred_element_type=f32)
             + lax.dot_general(km_lo, q, nt_dims, preferred_element_type=f32))
        jidx = lax.broadcasted_iota(jnp.int32, (nb, grp_rows), 0)
        q_blk = u * (tq // blk) + (lax.broadcasted_iota(jnp.int32, (nb, grp_rows), 1) % tq) // blk
        past = jidx < q_blk
        g = jnp.where(past, g, -jnp.inf)
        sel = jidx == q_blk
        for _ in range(MOBA_TOPK):
            mx = jnp.max(g, axis=0, keepdims=True)
            first = jnp.min(jnp.where(g == mx, jidx, nb), axis=0, keepdims=True)
            pick = jidx == first
            sel = jnp.logical_or(sel, jnp.logical_and(pick, past))
            g = jnp.where(pick, -jnp.inf, g)
        bias = jnp.where(sel, 0.0, NEG)
        bias_t = jnp.concatenate([jnp.zeros((HEAD_DIM, grp_rows), f32), bias,
                                  jnp.zeros((LANES - HEAD_DIM - nb, grp_rows), f32)], axis=0)
        qa_ref[kv * grp_rows:(kv + 1) * grp_rows] = (q.astype(f32) + bias_t.T).astype(bf16)

    m_ref[...] = jnp.full_like(m_ref, NEG)
    acc_ref[...] = jnp.zeros_like(acc_ref)

    def scores(c, s_ref, r):
        o = pl.multiple_of(c * MOBA_KEYS, MOBA_KEYS)
        rs = slice(r * sub, (r + 1) * sub)
        kv = r * sub // grp_rows
        s_ref[r] = lax.dot_general(k_ref[0, kv, pl.ds(o, MOBA_KEYS), :], qa_ref[rs], nt_dims,
                                   preferred_element_type=f32)

    def accumulate(c, s_ref, r, causal):
        o = pl.multiple_of(c * MOBA_KEYS, MOBA_KEYS)
        kv = r * sub // grp_rows
        s = s_ref[r]
        if causal:
            key = lax.broadcasted_iota(jnp.int32, (MOBA_KEYS, sub), 0)
            tok = (lax.broadcasted_iota(jnp.int32, (MOBA_KEYS, sub), 1) + r * sub) % tq
            s = jnp.where(jnp.logical_and(key // blk == tok // blk, key > tok), NEG, s)
        m_old = m_ref[r]
        m_new = jnp.maximum(m_old, jnp.max(s, axis=0, keepdims=True))
        p = jnp.exp2(s - m_new).astype(bf16)
        pv = jnp.dot(vt_ref[kv, :, pl.ds(o, MOBA_KEYS)], p, preferred_element_type=f32)
        acc_ref[r] = acc_ref[r] * jnp.exp2(m_old - m_new) + pv
        m_ref[r] = m_new

    def step(c_next, s_next, c_cur, s_cur, causal=False):
        for r in range(n_sub):
            if c_next is not None:
                scores(c_next, s_next, r)
            accumulate(c_cur, s_cur, r, causal)

    for r in range(n_sub):
        scores(0, s0_ref, r)

    def body(t, carry):
        step(2 * t + 1, s1_ref, 2 * t, s0_ref)
        step(2 * t + 2, s0_ref, 2 * t + 1, s1_ref)
        return carry

    lax.fori_loop(0, u // 2, body, 0)

    @pl.when(u % 2 == 1)
    def _():
        step(u, s1_ref, u - 1, s0_ref)
        step(None, None, u, s1_ref, causal=True)

    @pl.when(u % 2 == 0)
    def _():
        step(None, None, u, s0_ref, causal=True)

    heads = []
    for r in range(n_sub):
        a = acc_ref[r]
        o = a / a[HEAD_DIM:HEAD_DIM + 1, :]
        heads.append(jnp.concatenate([o, jnp.zeros((LANES - VT_ROWS, sub), f32)], axis=0).T)
    _merge_heads(jnp.concatenate(heads, axis=0), o_ref, tq)


def _moba(q, k, v):
    batch, _, seq, _ = q.shape
    nb = seq // MOBA_BLOCK
    n_grp = MOBA_KV_PER_STEP
    rows = n_grp * GROUP * MOBA_KEYS
    return pl.pallas_call(
        _moba_kernel,
        grid=(batch, N_KV_HEADS // n_grp, seq // MOBA_KEYS),
        in_specs=[pl.BlockSpec((1, n_grp * GROUP, MOBA_KEYS, LANES), lambda b, g, u: (b, g, u, 0)),
                  pl.BlockSpec((1, n_grp, seq, LANES), lambda b, g, u: (b, g, 0, 0)),
                  pl.BlockSpec((1, n_grp, seq, LANES), lambda b, g, u: (b, g, 0, 0))],
        out_specs=pl.BlockSpec((1, MOBA_KEYS, n_grp * GROUP * HEAD_DIM), lambda b, g, u: (b, u, g)),
        out_shape=jax.ShapeDtypeStruct((batch, seq, N_HEADS * HEAD_DIM), bf16),
        scratch_shapes=[pltpu.VMEM((n_grp, nb, LANES), f32),
                        pltpu.VMEM((n_grp, VT_ROWS, seq), bf16),
                        pltpu.VMEM((rows, LANES), bf16),
                        pltpu.VMEM((rows // MOBA_SUB, MOBA_KEYS, MOBA_SUB), f32),
                        pltpu.VMEM((rows // MOBA_SUB, MOBA_KEYS, MOBA_SUB), f32),
                        pltpu.VMEM((rows // MOBA_SUB, 1, MOBA_SUB), f32),
                        pltpu.VMEM((rows // MOBA_SUB, VT_ROWS, MOBA_SUB), f32)],
        compiler_params=pltpu.CompilerParams(dimension_semantics=("parallel", "parallel", "arbitrary"),
                                             vmem_limit_bytes=MOBA_VMEM_LIMIT),
        name="moba_attention",
    )(q, k, v)


def _swa_kernel(sink_ref, q_ref, kc_ref, kp_ref, vc_ref, vp_ref, o_ref):
    g = pl.program_id(1)
    i = pl.program_id(2)
    w = SWA_WINDOW
    n_grp = kc_ref.shape[1]
    rows = GROUP * w
    nt_dims = (((1,), (1,)), ((), ()))
    n_sb = q_ref.shape[2] // w
    key = lax.broadcasted_iota(jnp.int32, (2 * w, rows), 0)
    t_in = lax.broadcasted_iota(jnp.int32, (2 * w, rows), 1) % w
    band = jnp.logical_and(key > t_in, key <= t_in + w)
    first = jnp.logical_and(band, jnp.logical_or(key >= w, i > 0))
    head_of_row = lax.broadcasted_iota(jnp.int32, (1, rows), 1) // w
    lane = lax.broadcasted_iota(jnp.int32, (w, LANES), 1)
    low = lane < HEAD_DIM

    def window(ref_cur, ref_prev, kv, sb):
        if sb == 0:
            return jnp.concatenate([ref_prev[0, kv], ref_cur[0, kv, 0:w, :]], axis=0)
        return ref_cur[0, kv, (sb - 1) * w:(sb + 1) * w, :]

    work = [(kv, sb) for kv in range(n_grp) for sb in range(n_sb)]
    scores = [lax.dot_general(window(kc_ref, kp_ref, kv, sb),
                              q_ref[0, kv * GROUP:(kv + 1) * GROUP, sb * w:(sb + 1) * w, :].reshape(rows, LANES),
                              nt_dims, preferred_element_type=f32)
              for kv, sb in work]
    v_t = [jnp.concatenate([vp_ref[0, kv].astype(f32).T[0:VT_ROWS], vc_ref[0, kv].astype(f32).T[0:VT_ROWS]],
                           axis=1).astype(bf16) for kv in range(n_grp)]
    for (kv, sb), sc in zip(work, scores):
        sink = jnp.zeros((1, rows), f32)
        for hh in range(GROUP):
            sink = jnp.where(head_of_row == hh, sink_ref[(g * n_grp + kv) * GROUP + hh] * LOG2E, sink)
        s = jnp.where(first if sb == 0 else band, sc, NEG)
        m = jnp.maximum(jnp.max(s, axis=0, keepdims=True), sink)
        p = jnp.exp2(s - m).astype(bf16)
        acc = jnp.dot(v_t[kv][:, sb * w:(sb + 2) * w], p, preferred_element_type=f32)
        out_t = acc / (acc[HEAD_DIM:HEAD_DIM + 1, :] + jnp.exp2(sink - m))
        out = jnp.concatenate([out_t, jnp.zeros((LANES - VT_ROWS, rows), f32)], axis=0).T
        for c in range(GROUP // 2):
            even = out[(2 * c) * w:(2 * c + 1) * w]
            odd = out[(2 * c + 1) * w:(2 * c + 2) * w]
            chunk = kv * (GROUP // 2) + c
            o_ref[0, sb * w:(sb + 1) * w, chunk * LANES:(chunk + 1) * LANES] = jnp.where(
                low, even, pltpu.roll(odd, HEAD_DIM, 1)).astype(o_ref.dtype)


def _swa(q, k, v, sinks):
    batch, _, seq, _ = q.shape
    per = TQ_SWA // SWA_WINDOW
    n_grp = SWA_KV_PER_STEP
    cur = lambda b, g, i, s: (b, g, i, 0)
    prev = lambda b, g, i, s: (b, g, jnp.maximum(i * per - 1, 0), 0)
    grid_spec = pltpu.PrefetchScalarGridSpec(
        num_scalar_prefetch=1,
        grid=(batch, N_KV_HEADS // n_grp, seq // TQ_SWA),
        in_specs=[pl.BlockSpec((1, n_grp * GROUP, TQ_SWA, LANES), cur),
                  pl.BlockSpec((1, n_grp, TQ_SWA, LANES), cur),
                  pl.BlockSpec((1, n_grp, SWA_WINDOW, LANES), prev),
                  pl.BlockSpec((1, n_grp, TQ_SWA, LANES), cur),
                  pl.BlockSpec((1, n_grp, SWA_WINDOW, LANES), prev)],
        out_specs=pl.BlockSpec((1, TQ_SWA, n_grp * GROUP * HEAD_DIM), lambda b, g, i, s: (b, i, g)),
    )
    return pl.pallas_call(
        _swa_kernel,
        grid_spec=grid_spec,
        out_shape=jax.ShapeDtypeStruct((batch, seq, N_HEADS * HEAD_DIM), bf16),
        compiler_params=_cparams("parallel", "parallel", "arbitrary"),
        name="swa_attention",
    )(sinks, q, k, k, v, v)


def _layer_norm(x, g, b):
    mu = jnp.mean(x, axis=-1, keepdims=True)
    xc = x - mu
    var = jnp.mean(xc * xc, axis=-1, keepdims=True)
    return xc * lax.rsqrt(var + LN_EPS) * g + b


def _load_rows(ref, n_rows):
    return jnp.concatenate([ref[pl.ds(k, n_rows, stride=ROW_TILE), :] for k in range(ROW_TILE)], axis=1)


def _store_rows(ref, val):
    for k in range(ROW_TILE):
        ref[pl.ds(k, val.shape[0], stride=ROW_TILE), :] = val[:, k * LANES:(k + 1) * LANES]


def _mix_router_kernel(x_ref, a_ref, wo_ref, g_ref, b_ref, wr_ref, x1_ref, meta_ref, metat_ref, cnt_ref):
    tm = x_ref.shape[0]
    y = ALPHA * x_ref[...] + jnp.dot(a_ref[...], wo_ref[...], preferred_element_type=f32)
    x1 = _layer_norm(y, g_ref[...], b_ref[...])
    _store_rows(x1_ref, x1)

    @pl.when(pl.program_id(0) == 0)
    def _():
        cnt_ref[...] = jnp.zeros_like(cnt_ref)

    logits = jnp.dot(x1, wr_ref[...], preferred_element_type=f32)
    lane = lax.broadcasted_iota(jnp.int32, (tm, LANES), 1)
    lg = jnp.where(lane < N_EXPERTS, logits, -jnp.inf)
    m1 = jnp.max(lg, axis=1, keepdims=True)
    i1 = jnp.min(jnp.where(lg == m1, lane, LANES), axis=1, keepdims=True)
    lg2 = jnp.where(lane == i1, -jnp.inf, lg)
    m2 = jnp.max(lg2, axis=1, keepdims=True)
    i2 = jnp.min(jnp.where(lg2 == m2, lane, LANES), axis=1, keepdims=True)
    e2 = jnp.exp(m2 - m1)
    w1 = 1.0 / (1.0 + e2)
    w2 = e2 / (1.0 + e2)
    hit1 = lane == i1
    hit2 = lane == i2
    hits = jnp.where(jnp.logical_or(hit1, hit2), 1.0, 0.0)
    r_i = lax.broadcasted_iota(jnp.int32, (tm, tm), 0)
    c_i = lax.broadcasted_iota(jnp.int32, (tm, tm), 1)
    before = jnp.where(c_i < r_i, 1.0, 0.0).astype(bf16)
    cum = jnp.dot(before, hits.astype(bf16), preferred_element_type=f32) + cnt_ref[0:1, :]
    r1 = jnp.sum(jnp.where(hit1, cum, 0.0), axis=1, keepdims=True)
    r2 = jnp.sum(jnp.where(hit2, cum, 0.0), axis=1, keepdims=True)
    cnt_ref[...] = cnt_ref[...] + jnp.sum(hits, axis=0, keepdims=True)
    meta = jnp.where(lane == 0, i1.astype(f32), 0.0)
    meta = jnp.where(lane == 1, i2.astype(f32), meta)
    meta = jnp.where(lane == 2, w1, meta)
    meta = jnp.where(lane == 3, w2, meta)
    meta = jnp.where(lane == 4, r1, meta)
    meta = jnp.where(lane == 5, r2, meta)
    meta_ref[...] = meta
    metat_ref[...] = meta.T[0:8, :]


def _mix_router(x2d, a2d, w_o, g, b, w_router):
    n = x2d.shape[0]
    tok = lambda i: (i, 0)
    const = lambda i: (0, 0)
    wr = jnp.zeros((D_MODEL, LANES), f32).at[:, :N_EXPERTS].set(w_router)
    return pl.pallas_call(
        _mix_router_kernel, grid=(n // TM_MIX,),
        in_specs=[pl.BlockSpec((TM_MIX, D_MODEL), tok),
                  pl.BlockSpec((TM_MIX, D_MODEL), tok),
                  pl.BlockSpec((D_MODEL, D_MODEL), const),
                  pl.BlockSpec((1, D_MODEL), const),
                  pl.BlockSpec((1, D_MODEL), const),
                  pl.BlockSpec((D_MODEL, LANES), const)],
        out_specs=[pl.BlockSpec((TM_MIX * ROW_TILE, LANES), tok),
                   pl.BlockSpec((TM_MIX, LANES), tok), pl.BlockSpec((8, TM_MIX), lambda i: (0, i)),
                   pl.BlockSpec((8, LANES), const)],
        out_shape=[jax.ShapeDtypeStruct((n * ROW_TILE, LANES), f32),
                   jax.ShapeDtypeStruct((n, LANES), f32), jax.ShapeDtypeStruct((8, n), f32),
                   jax.ShapeDtypeStruct((8, LANES), f32)],
        compiler_params=_cparams("arbitrary"), name="mix_ln_router",
    )(x2d, a2d, w_o, g, b, wr)


def _swiglu_accumulate(acc_ref, xb, wg, wu, wd, fc):
    gate = jnp.dot(xb, wg.astype(bf16), preferred_element_type=f32)
    up = jnp.dot(xb, wu.astype(bf16), preferred_element_type=f32)
    h = (gate / (1.0 + jnp.exp(-gate)) * up).astype(bf16)
    part = jnp.dot(h, wd.astype(bf16), preferred_element_type=f32)

    @pl.when(fc == 0)
    def _():
        acc_ref[...] = part

    @pl.when(fc > 0)
    def _():
        acc_ref[...] = acc_ref[...] + part


def _moe_ffn_kernel(te_ref, nact_ref, x_ref, wg_ref, wu_ref, wd_ref, y_ref, xb_ref, acc_ref):
    n = pl.program_id(0)
    fc = pl.program_id(1)
    active = n < nact_ref[0]

    @pl.when(jnp.logical_and(jnp.logical_not(active), fc == 0))
    def _():
        y_ref[...] = jnp.zeros_like(y_ref)

    @pl.when(active)
    def _():
        @pl.when(fc == 0)
        def _():
            xb_ref[...] = _load_rows(x_ref, TM_FFN).astype(bf16)

        _swiglu_accumulate(acc_ref, xb_ref[...], wg_ref[0], wu_ref[0], wd_ref[0], fc)

        @pl.when(fc == pl.num_programs(1) - 1)
        def _():
            _store_rows(y_ref, acc_ref[...])


def _moe_ffn(xs_rows, w_gate, w_up, w_down, tile_expert, n_active):
    blk = (TM_FFN * ROW_TILE, LANES)
    n_tiles = xs_rows.shape[0] // blk[0]
    n_fc = D_FF // FC_FFN

    def row(n, f, te, na):
        return jnp.minimum(n, na[0] - 1), 0

    def w_in(n, f, te, na):
        return te[jnp.minimum(n, na[0] - 1)], 0, jnp.where(n < na[0], f, n_fc - 1)

    def w_out(n, f, te, na):
        return te[jnp.minimum(n, na[0] - 1)], jnp.where(n < na[0], f, n_fc - 1), 0

    grid_spec = pltpu.PrefetchScalarGridSpec(
        num_scalar_prefetch=2,
        grid=(n_tiles, n_fc),
        in_specs=[pl.BlockSpec(blk, row),
                  pl.BlockSpec((1, D_MODEL, FC_FFN), w_in),
                  pl.BlockSpec((1, D_MODEL, FC_FFN), w_in),
                  pl.BlockSpec((1, FC_FFN, D_MODEL), w_out)],
        out_specs=pl.BlockSpec(blk, lambda n, f, te, na: (n, 0)),
        scratch_shapes=[pltpu.VMEM((TM_FFN, D_MODEL), bf16), pltpu.VMEM((TM_FFN, D_MODEL), f32)],
    )
    return pl.pallas_call(
        _moe_ffn_kernel, grid_spec=grid_spec,
        out_shape=jax.ShapeDtypeStruct(xs_rows.shape, f32),
        compiler_params=_cparams("arbitrary", "arbitrary"), name="moe_swiglu",
    )(tile_expert, n_active, xs_rows, w_gate, w_up, w_down)


def _dense_tail_kernel(x_ref, a_ref, wo_ref, g1_ref, b1_ref, wg_ref, wu_ref, wd_ref, g2_ref, b2_ref,
                       p_ref, wp_ref, wgate_ref, o_ref, x1_ref, xb_ref):
    fc = pl.program_id(1)
    acc_ref = o_ref

    @pl.when(fc == 0)
    def _():
        y = ALPHA * x_ref[...] + jnp.dot(a_ref[...], wo_ref[...], preferred_element_type=f32)
        x1 = _layer_norm(y, g1_ref[...], b1_ref[...])
        x1_ref[...] = x1
        xb_ref[...] = x1.astype(bf16)

    _swiglu_accumulate(acc_ref, xb_ref[...], wg_ref[...], wu_ref[...], wd_ref[...], fc)

    @pl.when(fc == pl.num_programs(1) - 1)
    def _():
        x2 = _layer_norm(ALPHA * x1_ref[...] + acc_ref[...], g2_ref[...], b2_ref[...])
        _ple(x2, p_ref, wp_ref, wgate_ref, o_ref)


def _dense_tail(x2d, a2d, w_o, g1, b1, w_gate, w_up, w_down, g2, b2, p2d, w_proj, w_pgate):
    n = x2d.shape[0]
    tok = lambda i, f: (i, 0)
    const = lambda i, f: (0, 0)
    once = dict(pipeline_mode=pl.Buffered(1))
    return pl.pallas_call(
        _dense_tail_kernel, grid=(n // TM_DENSE, D_FF // FC_FFN),
        in_specs=[pl.BlockSpec((TM_DENSE, D_MODEL), tok),
                  pl.BlockSpec((TM_DENSE, D_MODEL), tok),
                  pl.BlockSpec((D_MODEL, D_MODEL), const, **once),
                  pl.BlockSpec((1, D_MODEL), const, **once),
                  pl.BlockSpec((1, D_MODEL), const, **once),
                  pl.BlockSpec((D_MODEL, FC_FFN), lambda i, f: (0, f)),
                  pl.BlockSpec((D_MODEL, FC_FFN), lambda i, f: (0, f)),
                  pl.BlockSpec((FC_FFN, D_MODEL), lambda i, f: (f, 0)),
                  pl.BlockSpec((1, D_MODEL), const, **once),
                  pl.BlockSpec((1, D_MODEL), const, **once),
                  pl.BlockSpec((TM_DENSE, PLE_DIM), tok),
                  pl.BlockSpec((PLE_DIM, D_MODEL), const, **once),
                  pl.BlockSpec((D_MODEL, D_MODEL), const, **once)],
        out_specs=pl.BlockSpec((TM_DENSE, D_MODEL), tok),
        out_shape=jax.ShapeDtypeStruct((n, D_MODEL), f32),
        scratch_shapes=[pltpu.VMEM((TM_DENSE, D_MODEL), f32), pltpu.VMEM((TM_DENSE, D_MODEL), bf16)],
        compiler_params=pltpu.CompilerParams(dimension_semantics=("parallel", "arbitrary"),
                                             vmem_limit_bytes=MOBA_VMEM_LIMIT), name="dense_tail",
    )(x2d, a2d, w_o, g1, b1, w_gate, w_up, w_down, g2, b2, p2d, w_proj, w_pgate)


def _dispatch_kernel(last_ref, nact_ref, pos_ref, x_ref, xs_ref, zero_ref, sem, zsem):
    tm = x_ref.shape[0] // ROW_TILE
    n_tiles = xs_ref.shape[0] // zero_ref.shape[0]

    @pl.when(pl.program_id(0) == 0)
    def _():
        zero_ref[...] = jnp.zeros_like(zero_ref)

        def fill(t):
            dst = pl.multiple_of(t * zero_ref.shape[0], zero_ref.shape[0])
            return pltpu.make_async_copy(zero_ref, xs_ref.at[pl.ds(dst, zero_ref.shape[0])], zsem)

        def start_tail(t, _):
            fill(t).start()
            return 0

        def wait_tail(t, _):
            fill(t).wait()
            return 0

        for e in range(N_EXPERTS):
            pl.when(last_ref[e] >= 0)(lambda e=e: fill(last_ref[e]).start())
        lax.fori_loop(nact_ref[0], n_tiles, start_tail, 0)
        for e in range(N_EXPERTS):
            pl.when(last_ref[e] >= 0)(lambda e=e: fill(last_ref[e]).wait())
        lax.fori_loop(nact_ref[0], n_tiles, wait_tail, 0)

    def start(r, _):
        src = x_ref.at[pl.ds(pl.multiple_of(r * ROW_TILE, ROW_TILE), ROW_TILE)]
        for k in range(2):
            dst = pl.multiple_of(pos_ref[2 * r + k] * ROW_TILE, ROW_TILE)
            pltpu.make_async_copy(src, xs_ref.at[pl.ds(dst, ROW_TILE)], sem).start(priority=k)
        return 0

    lax.fori_loop(0, tm, start, 0, unroll=DMA_UNROLL)
    for _ in range(2):
        pltpu.make_async_copy(x_ref, xs_ref.at[pl.ds(0, tm * ROW_TILE)], sem).wait()


def _dispatch(x1_rows, pos_flat, last_tile, n_active, n_tiles):
    n = x1_rows.shape[0] // ROW_TILE
    grid_spec = pltpu.PrefetchScalarGridSpec(
        num_scalar_prefetch=2,
        grid=(n // TM_DISPATCH,),
        in_specs=[pl.BlockSpec((2 * TM_DISPATCH,), lambda i, lt, na: (i,), memory_space=pltpu.SMEM),
                  pl.BlockSpec((TM_DISPATCH * ROW_TILE, LANES), lambda i, lt, na: (i, 0))],
        out_specs=pl.BlockSpec(memory_space=pl.ANY),
        scratch_shapes=[pltpu.VMEM((TM_FFN * ROW_TILE, LANES), f32), pltpu.SemaphoreType.DMA(()),
                        pltpu.SemaphoreType.DMA(())],
    )
    return pl.pallas_call(
        _dispatch_kernel, grid_spec=grid_spec,
        out_shape=jax.ShapeDtypeStruct((n_tiles * TM_FFN * ROW_TILE, LANES), f32),
        compiler_params=_cparams("arbitrary"), name="moe_dispatch",
    )(last_tile, n_active, pos_flat, x1_rows)


def _ple(x2, p_ref, wp_ref, wgate_ref, o_ref):
    pe = jnp.dot(p_ref[...].astype(bf16), wp_ref[...], preferred_element_type=f32)
    z = jnp.dot(x2.astype(bf16), wgate_ref[...], preferred_element_type=f32)
    o_ref[...] = x2 + pe / (1.0 + jnp.exp(-z))


def _out_moe_kernel(pos_ref, nxt_ref, x1_ref, meta_ref, y_ref, g_ref, b_ref, p_ref, wp_ref, wgate_ref, o_ref,
                    buf_ref, sems):
    i = pl.program_id(0)
    tm = x1_ref.shape[0] // ROW_TILE
    slot = i % 2

    def gather(idx_ref, to_slot):
        def start(r, _):
            dst = pl.ds(pl.multiple_of(r * ROW_TILE, ROW_TILE), ROW_TILE)
            for k in range(2):
                src = pl.multiple_of(idx_ref[2 * r + k] * ROW_TILE, ROW_TILE)
                pltpu.make_async_copy(y_ref.at[pl.ds(src, ROW_TILE)], buf_ref.at[to_slot, k, dst],
                                      sems.at[to_slot]).start(priority=k)
            return 0

        lax.fori_loop(0, tm, start, 0, unroll=DMA_UNROLL)

    @pl.when(i == 0)
    def _():
        gather(pos_ref, slot)

    @pl.when(i + 1 < pl.num_programs(0))
    def _():
        gather(nxt_ref, 1 - slot)

    for k in range(2):
        pltpu.make_async_copy(y_ref.at[pl.ds(0, tm * ROW_TILE)], buf_ref.at[slot, k], sems.at[slot]).wait()
    meta = meta_ref[...]
    f = meta[:, 2:3] * _load_rows(buf_ref.at[slot, 0], tm) + meta[:, 3:4] * _load_rows(buf_ref.at[slot, 1], tm)
    x2 = _layer_norm(ALPHA * _load_rows(x1_ref, tm) + f, g_ref[...], b_ref[...])
    _ple(x2, p_ref, wp_ref, wgate_ref, o_ref)


def _out_moe(x1_rows, meta, y_rows, pos_flat, g, b, p2d, w_proj, w_gate):
    n = x1_rows.shape[0] // ROW_TILE
    steps = n // TM_OUT
    tok = lambda i: (i, 0)
    const = lambda i: (0, 0)
    return pl.pallas_call(
        _out_moe_kernel, grid=(steps,),
        in_specs=[pl.BlockSpec((2 * TM_OUT,), lambda i: (i,), memory_space=pltpu.SMEM),
                  pl.BlockSpec((2 * TM_OUT,), lambda i: (jnp.minimum(i + 1, steps - 1),), memory_space=pltpu.SMEM),
                  pl.BlockSpec((TM_OUT * ROW_TILE, LANES), tok),
                  pl.BlockSpec((TM_OUT, LANES), tok),
                  pl.BlockSpec(memory_space=pl.ANY),
                  pl.BlockSpec((1, D_MODEL), const),
                  pl.BlockSpec((1, D_MODEL), const),
                  pl.BlockSpec((TM_OUT, PLE_DIM), tok),
                  pl.BlockSpec((PLE_DIM, D_MODEL), const),
                  pl.BlockSpec((D_MODEL, D_MODEL), const)],
        out_specs=pl.BlockSpec((TM_OUT, D_MODEL), tok),
        out_shape=jax.ShapeDtypeStruct((n, D_MODEL), f32),
        scratch_shapes=[pltpu.VMEM((2, 2, TM_OUT * ROW_TILE, LANES), f32), pltpu.SemaphoreType.DMA((2,))],
        compiler_params=_cparams("arbitrary"), name="moe_combine_ln_ple",
    )(pos_flat, pos_flat, x1_rows, meta, y_rows, g, b, p2d, w_proj, w_gate)


def _routing_plan(meta_t, counts_row, n_tiles):
    i1 = meta_t[0].astype(jnp.int32)
    i2 = meta_t[1].astype(jnp.int32)
    r1 = meta_t[4].astype(jnp.int32)
    r2 = meta_t[5].astype(jnp.int32)
    counts = counts_row[0, :N_EXPERTS].astype(jnp.int32)
    tiles = (counts + TM_FFN - 1) // TM_FFN
    tile_end = jnp.cumsum(tiles)
    offset = (tile_end - tiles) * TM_FFN
    pos = jnp.stack([offset[i1] + r1, offset[i2] + r2], axis=1).reshape(-1)
    tile_expert = jnp.sum(jnp.arange(n_tiles)[:, None] >= tile_end[None, :], axis=1)
    tile_expert = jnp.minimum(tile_expert, N_EXPERTS - 1)
    last_tile = jnp.where(tiles > 0, tile_end - 1, -1)
    return (pos.astype(jnp.int32), tile_expert.astype(jnp.int32), tile_end[-1:].astype(jnp.int32),
            last_tile.astype(jnp.int32))


def kernel(x, p, positions, w_qkv, w_o, ln_mix_g, ln_mix_b, ln_ffn_g, ln_ffn_b, sinks, w_ffn_gate, w_ffn_up,
           w_ffn_down, w_router, w_exp_gate, w_exp_up, w_exp_down, w_ple_proj, w_ple_gate):
    batch, seq, _ = x.shape
    n = batch * seq
    assert seq % TQ_SWA == 0 and (seq // MOBA_BLOCK) % 8 == 0 and seq // MOBA_BLOCK <= LANES - HEAD_DIM
    tabs = _rope_tables(positions)
    x2d = x.reshape(n, D_MODEL)
    row = lambda v: v.reshape(1, D_MODEL)
    for i in range(DEPTH):
        j = i // 2
        q, k, v = _qkv(x2d, w_qkv[i].astype(bf16), tabs, batch, seq)
        if i % 2 == 0:
            a = _moba(q, k, v)
        else:
            a = _swa(q, k, v, sinks[j])
        a2d = a.reshape(n, D_MODEL)
        wo = w_o[i].astype(bf16)
        ple_args = (row(ln_ffn_g[i]), row(ln_ffn_b[i]), p[i].reshape(n, PLE_DIM),
                    w_ple_proj[i].astype(bf16), w_ple_gate[i].astype(bf16))
        if i % 2 == 0:
            x2d = _dense_tail(x2d, a2d, wo, row(ln_mix_g[i]), row(ln_mix_b[i]), w_ffn_gate[j].astype(bf16),
                              w_ffn_up[j].astype(bf16), w_ffn_down[j].astype(bf16), *ple_args)
        else:
            x1, meta, meta_t, counts = _mix_router(x2d, a2d, wo, row(ln_mix_g[i]), row(ln_mix_b[i]), w_router[j])
            n_tiles = 2 * n // TM_FFN + N_EXPERTS
            pos, tile_expert, n_active, last_tile = _routing_plan(meta_t, counts, n_tiles)
            xs = _dispatch(x1, pos, last_tile, n_active, n_tiles)
            y = _moe_ffn(xs, w_exp_gate[j], w_exp_up[j], w_exp_down[j], tile_expert, n_active)
            x2d = _out_moe(x1, meta, y, pos, *ple_args)
    return x2d.reshape(batch, seq, D_MODEL)
```

```python
import functools

import jax
import jax.numpy as jnp
import numpy as np
from jax import lax
from jax.experimental import pallas as pl
from jax.experimental.pallas import tpu as pltpu

D_MODEL = 1024
N_HEADS = 16
N_KV_HEADS = 4
HEAD_DIM = 64
GROUP = N_HEADS // N_KV_HEADS
QKV_DIM = (N_HEADS + 2 * N_KV_HEADS) * HEAD_DIM
ROT_DIM = 16
ROPE_THETA = 500000.0
MOBA_BLOCK = 256
MOBA_TOPK = 3
SWA_WINDOW = 128
D_FF = 3584
N_EXPERTS = 8
PLE_DIM = 256
LN_EPS = 1e-5
DEPTH = 2
ALPHA = (2.0 * DEPTH) ** 0.25
LOG2E = 1.4426950408889634
Q_SCALE = HEAD_DIM ** -0.5 * LOG2E

LANES = 128
ROW_TILE = D_MODEL // LANES
DMA_UNROLL = 8
NEG = -1e30
VMEM_LIMIT = 48 * 1024 * 1024

TM_QKV = 512
TM_MIX = 512
TM_DENSE = 1024
TM_FFN = 1024
FC_FFN = 512
TM_DISPATCH = 512
TM_OUT = 256
TQ_SWA = 1024
SWA_KV_PER_STEP = 2
MOBA_SUB = 512
MOBA_KEYS = 2 * MOBA_BLOCK
VT_ROWS = 80
MOBA_KV_PER_STEP = 2
MOBA_VMEM_LIMIT = 56 * 1024 * 1024

f32 = jnp.float32
bf16 = jnp.bfloat16


def _cparams(*sem):
    return pltpu.CompilerParams(dimension_semantics=sem, vmem_limit_bytes=VMEM_LIMIT)


def _rope_kernel(pos_ref, inv_ref, cos_ref, sin_ref):
    ang = pos_ref[...].astype(f32) * inv_ref[...]
    cos_ref[...] = jnp.cos(ang)
    sin_ref[...] = jnp.sin(ang)


def _rope_tables(positions):
    n = positions.size
    half = ROT_DIM // 2
    inv = 1.0 / (ROPE_THETA ** (jnp.arange(0, ROT_DIM, 2, dtype=f32) / ROT_DIM))
    rows = n * half // LANES
    pos_rep = jnp.repeat(positions.reshape(-1), half).reshape(rows, LANES)
    inv_rep = jnp.tile(inv, LANES // half).reshape(1, LANES)
    cos, sin = pl.pallas_call(
        _rope_kernel,
        out_shape=(jax.ShapeDtypeStruct((rows, LANES), f32),) * 2,
        name="rope_tables",
    )(pos_rep, inv_rep)
    cos = jnp.tile(cos.reshape(n, half), (1, LANES // half))
    sin = jnp.tile(sin.reshape(n, half), (1, LANES // half))
    d = jnp.arange(LANES) % HEAD_DIM
    c_tab = jnp.where(d < ROT_DIM, cos, 1.0)
    s_lo = jnp.where(d < half, -sin, 0.0)
    s_hi = jnp.where((d >= half) & (d < ROT_DIM), sin, 0.0)
    return c_tab, s_lo, s_hi


def _qkv_kernel(x_ref, w_ref, c_ref, slo_ref, shi_ref, q_ref, k_ref, v_ref, *, tiles_per_seq):
    tm = x_ref.shape[0]
    acc = jnp.dot(x_ref[...].astype(bf16), w_ref[...], preferred_element_type=f32)
    lane = lax.broadcasted_iota(jnp.int32, (tm, LANES), 1)
    row = lax.broadcasted_iota(jnp.int32, (tm, LANES), 0)
    low = lane < HEAD_DIM
    c_tab, s_lo, s_hi = c_ref[...], slo_ref[...], shi_ref[...]
    seq0 = (pl.program_id(0) % tiles_per_seq) * tm
    blk = (seq0 + row) // MOBA_BLOCK
    k_pad = jnp.where(lane == HEAD_DIM + blk, 1.0, 0.0)
    v_pad = jnp.where(lane == HEAD_DIM, 1.0, 0.0)
    n_q = N_HEADS // 2
    n_kv = N_KV_HEADS // 2
    for c in range(n_q + n_kv):
        xc = acc[:, c * LANES:(c + 1) * LANES]
        r = xc * c_tab + pltpu.roll(xc, LANES - ROT_DIM // 2, 1) * s_lo + pltpu.roll(xc, ROT_DIM // 2, 1) * s_hi
        r_odd = pltpu.roll(r, HEAD_DIM, 1)
        if c < n_q:
            q_ref[0, 2 * c] = jnp.where(low, r * Q_SCALE, 0.0).astype(bf16)
            q_ref[0, 2 * c + 1] = jnp.where(low, r_odd * Q_SCALE, 0.0).astype(bf16)
        else:
            k_ref[0, 2 * (c - n_q)] = jnp.where(low, r, k_pad).astype(bf16)
            k_ref[0, 2 * (c - n_q) + 1] = jnp.where(low, r_odd, k_pad).astype(bf16)
    for c in range(n_kv):
        xc = acc[:, (n_q + n_kv + c) * LANES:(n_q + n_kv + c + 1) * LANES]
        v_ref[0, 2 * c] = jnp.where(low, xc, v_pad).astype(bf16)
        v_ref[0, 2 * c + 1] = jnp.where(low, pltpu.roll(xc, HEAD_DIM, 1), v_pad).astype(bf16)


def _qkv(x2d, w_qkv, tabs, batch, seq):
    n = x2d.shape[0]
    nt = seq // TM_QKV
    tok = lambda i: (i, 0)
    head = lambda i: (i // nt, 0, i % nt, 0)
    return pl.pallas_call(
        functools.partial(_qkv_kernel, tiles_per_seq=nt),
        grid=(n // TM_QKV,),
        in_specs=[pl.BlockSpec((TM_QKV, D_MODEL), tok),
                  pl.BlockSpec((D_MODEL, QKV_DIM), lambda i: (0, 0)),
                  pl.BlockSpec((TM_QKV, LANES), tok),
                  pl.BlockSpec((TM_QKV, LANES), tok),
                  pl.BlockSpec((TM_QKV, LANES), tok)],
        out_specs=[pl.BlockSpec((1, N_HEADS, TM_QKV, LANES), head),
                   pl.BlockSpec((1, N_KV_HEADS, TM_QKV, LANES), head),
                   pl.BlockSpec((1, N_KV_HEADS, TM_QKV, LANES), head)],
        out_shape=[jax.ShapeDtypeStruct((batch, N_HEADS, seq, LANES), bf16),
                   jax.ShapeDtypeStruct((batch, N_KV_HEADS, seq, LANES), bf16),
                   jax.ShapeDtypeStruct((batch, N_KV_HEADS, seq, LANES), bf16)],
        compiler_params=_cparams("parallel"),
        name="qkv_rope",
    )(x2d, w_qkv, *tabs)


def _merge_heads(o, out_ref, rows):
    lane = lax.broadcasted_iota(jnp.int32, (rows, LANES), 1)
    low = lane < HEAD_DIM
    for c in range(o.shape[0] // rows // 2):
        even = o[(2 * c) * rows:(2 * c + 1) * rows]
        odd = o[(2 * c + 1) * rows:(2 * c + 2) * rows]
        out_ref[0, :, c * LANES:(c + 1) * LANES] = jnp.where(low, even, pltpu.roll(odd, HEAD_DIM, 1)).astype(out_ref.dtype)


def _moba_kernel(q_ref, k_ref, v_ref, o_ref, kmean_ref, vt_ref, qa_ref, s0_ref, s1_ref, m_ref, acc_ref):
    u = pl.program_id(2)
    blk = MOBA_BLOCK
    tq = q_ref.shape[2]
    nb = k_ref.shape[2] // blk
    n_grp = k_ref.shape[1]
    grp_rows = GROUP * tq
    rows = n_grp * grp_rows
    sub = MOBA_SUB
    n_sub = rows // sub
    nt_dims = (((1,), (1,)), ((), ()))

    @pl.when(u == 0)
    def _():
        for kv in range(n_grp):
            for j in range(nb):
                kj = k_ref[0, kv, j * blk:(j + 1) * blk, :].astype(f32)
                kmean_ref[kv, j:j + 1, :] = jnp.sum(kj, axis=0, keepdims=True) * (1.0 / blk)
                vj = v_ref[0, kv, j * blk:(j + 1) * blk, :].astype(f32)
                vt_ref[kv, :, j * blk:(j + 1) * blk] = vj.T.astype(bf16)

    for kv in range(n_grp):
        q = q_ref[0, kv * GROUP:(kv + 1) * GROUP].reshape(grp_rows, LANES)
        km = kmean_ref[kv]
        km_hi = km.astype(bf16)
        km_lo = (km - km_hi.astype(f32)).astype(bf16)
        g = (lax.dot_general(km_hi, q, nt_dims, preferred_element_type=f32)
             + lax.dot_general(km_lo, q, nt_dims, preferred_element_type=f32))
        jidx = lax.broadcasted_iota(jnp.int32, (nb, grp_rows), 0)
        q_blk = u * (tq // blk) + (lax.broadcasted_iota(jnp.int32, (nb, grp_rows), 1) % tq) // blk
        past = jidx < q_blk
        g = jnp.where(past, g, -jnp.inf)
        sel = jidx == q_blk
        for _ in range(MOBA_TOPK):
            mx = jnp.max(g, axis=0, keepdims=True)
            first = jnp.min(jnp.where(g == mx, jidx, nb), axis=0, keepdims=True)
            pick = jidx == first
            sel = jnp.logical_or(sel, jnp.logical_and(pick, past))
            g = jnp.where(pick, -jnp.inf, g)
        bias = jnp.where(sel, 0.0, NEG)
        bias_t = jnp.concatenate([jnp.zeros((HEAD_DIM, grp_rows), f32), bias,
                                  jnp.zeros((LANES - HEAD_DIM - nb, grp_rows), f32)], axis=0)
        qa_ref[kv * grp_rows:(kv + 1) * grp_rows] = (q.astype(f32) + bias_t.T).astype(bf16)

    m_ref[...] = jnp.full_like(m_ref, NEG)
    acc_ref[...] = jnp.zeros_like(acc_ref)

    def scores(c, s_ref, r):
        o = pl.multiple_of(c * MOBA_KEYS, MOBA_KEYS)
        rs = slice(r * sub, (r + 1) * sub)
        kv = r * sub // grp_rows
        s_ref[r] = lax.dot_general(k_ref[0, kv, pl.ds(o, MOBA_KEYS), :], qa_ref[rs], nt_dims,
                                   preferred_element_type=f32)

    def accumulate(c, s_ref, r, causal):
        o = pl.multiple_of(c * MOBA_KEYS, MOBA_KEYS)
        kv = r * sub // grp_rows
        s = s_ref[r]
        if causal:
            key = lax.broadcasted_iota(jnp.int32, (MOBA_KEYS, sub), 0)
            tok = (lax.broadcasted_iota(jnp.int32, (MOBA_KEYS, sub), 1) + r * sub) % tq
            s = jnp.where(jnp.logical_and(key // blk == tok // blk, key > tok), NEG, s)
        m_old = m_ref[r]
        m_new = jnp.maximum(m_old, jnp.max(s, axis=0, keepdims=True))
        p = jnp.exp2(s - m_new).astype(bf16)
        pv = jnp.dot(vt_ref[kv, :, pl.ds(o, MOBA_KEYS)], p, preferred_element_type=f32)
        acc_ref[r] = acc_ref[r] * jnp.exp2(m_old - m_new) + pv
        m_ref[r] = m_new

    def step(c_next, s_next, c_cur, s_cur, causal=False):
        for r in range(n_sub):
            if c_next is not None:
                scores(c_next, s_next, r)
            accumulate(c_cur, s_cur, r, causal)

    for r in range(n_sub):
        scores(0, s0_ref, r)

    def body(t, carry):
        step(2 * t + 1, s1_ref, 2 * t, s0_ref)
        step(2 * t + 2, s0_ref, 2 * t + 1, s1_ref)
        return carry

    lax.fori_loop(0, u // 2, body, 0)

    @pl.when(u % 2 == 1)
    def _():
        step(u, s1_ref, u - 1, s0_ref)
        step(None, None, u, s1_ref, causal=True)

    @pl.when(u % 2 == 0)
    def _():
        step(None, None, u, s0_ref, causal=True)

    heads = []
    for r in range(n_sub):
        a = acc_ref[r]
        heads.append((a / a[HEAD_DIM:HEAD_DIM + 1, :]).T)
    _merge_heads(jnp.concatenate(heads, axis=0), o_ref, tq)


def _moba(q, k, v):
    batch, _, seq, _ = q.shape
    nb = seq // MOBA_BLOCK
    n_grp = MOBA_KV_PER_STEP
    rows = n_grp * GROUP * MOBA_KEYS
    return pl.pallas_call(
        _moba_kernel,
        grid=(batch, N_KV_HEADS // n_grp, seq // MOBA_KEYS),
        in_specs=[pl.BlockSpec((1, n_grp * GROUP, MOBA_KEYS, LANES), lambda b, g, u: (b, g, u, 0)),
                  pl.BlockSpec((1, n_grp, seq, LANES), lambda b, g, u: (b, g, 0, 0)),
                  pl.BlockSpec((1, n_grp, seq, LANES), lambda b, g, u: (b, g, 0, 0))],
        out_specs=pl.BlockSpec((1, MOBA_KEYS, n_grp * GROUP * HEAD_DIM), lambda b, g, u: (b, u, g)),
        out_shape=jax.ShapeDtypeStruct((batch, seq, N_HEADS * HEAD_DIM), bf16),
        scratch_shapes=[pltpu.VMEM((n_grp, nb, LANES), f32),
                        pltpu.VMEM((n_grp, LANES, seq), bf16),
                        pltpu.VMEM((rows, LANES), bf16),
                        pltpu.VMEM((rows // MOBA_SUB, MOBA_KEYS, MOBA_SUB), f32),
                        pltpu.VMEM((rows // MOBA_SUB, MOBA_KEYS, MOBA_SUB), f32),
                        pltpu.VMEM((rows // MOBA_SUB, 1, MOBA_SUB), f32),
                        pltpu.VMEM((rows // MOBA_SUB, LANES, MOBA_SUB), f32)],
        compiler_params=pltpu.CompilerParams(dimension_semantics=("parallel", "parallel", "arbitrary"),
                                             vmem_limit_bytes=MOBA_VMEM_LIMIT),
        name="moba_attention",
    )(q, k, v)


def _swa_kernel(sink_ref, q_ref, kc_ref, kp_ref, vc_ref, vp_ref, o_ref):
    g = pl.program_id(1)
    i = pl.program_id(2)
    w = SWA_WINDOW
    n_grp = kc_ref.shape[1]
    rows = GROUP * w
    nt_dims = (((1,), (1,)), ((), ()))
    n_sb = q_ref.shape[2] // w
    key = lax.broadcasted_iota(jnp.int32, (2 * w, rows), 0)
    t_in = lax.broadcasted_iota(jnp.int32, (2 * w, rows), 1) % w
    band = jnp.logical_and(key > t_in, key <= t_in + w)
    first = jnp.logical_and(band, jnp.logical_or(key >= w, i > 0))
    head_of_row = lax.broadcasted_iota(jnp.int32, (1, rows), 1) // w
    lane = lax.broadcasted_iota(jnp.int32, (w, LANES), 1)
    low = lane < HEAD_DIM

    def window(ref_cur, ref_prev, kv, sb):
        if sb == 0:
            return jnp.concatenate([ref_prev[0, kv], ref_cur[0, kv, 0:w, :]], axis=0)
        return ref_cur[0, kv, (sb - 1) * w:(sb + 1) * w, :]

    work = [(kv, sb) for kv in range(n_grp) for sb in range(n_sb)]
    scores = [lax.dot_general(window(kc_ref, kp_ref, kv, sb),
                              q_ref[0, kv * GROUP:(kv + 1) * GROUP, sb * w:(sb + 1) * w, :].reshape(rows, LANES),
                              nt_dims, preferred_element_type=f32)
              for kv, sb in work]
    v_t = [jnp.concatenate([vp_ref[0, kv].astype(f32).T[0:VT_ROWS], vc_ref[0, kv].astype(f32).T[0:VT_ROWS]],
                           axis=1).astype(bf16) for kv in range(n_grp)]
    for (kv, sb), sc in zip(work, scores):
        sink = jnp.zeros((1, rows), f32)
        for hh in range(GROUP):
            sink = jnp.where(head_of_row == hh, sink_ref[(g * n_grp + kv) * GROUP + hh] * LOG2E, sink)
        s = jnp.where(first if sb == 0 else band, sc, NEG)
        m = jnp.maximum(jnp.max(s, axis=0, keepdims=True), sink)
        p = jnp.exp2(s - m).astype(bf16)
        acc = jnp.dot(v_t[kv][:, sb * w:(sb + 2) * w], p, preferred_element_type=f32)
        out_t = acc / (acc[HEAD_DIM:HEAD_DIM + 1, :] + jnp.exp2(sink - m))
        out = jnp.concatenate([out_t, jnp.zeros((LANES - VT_ROWS, rows), f32)], axis=0).T
        for c in range(GROUP // 2):
            even = out[(2 * c) * w:(2 * c + 1) * w]
            odd = out[(2 * c + 1) * w:(2 * c + 2) * w]
            chunk = kv * (GROUP // 2) + c
            o_ref[0, sb * w:(sb + 1) * w, chunk * LANES:(chunk + 1) * LANES] = jnp.where(
                low, even, pltpu.roll(odd, HEAD_DIM, 1)).astype(o_ref.dtype)


def _swa(q, k, v, sinks):
    batch, _, seq, _ = q.shape
    per = TQ_SWA // SWA_WINDOW
    n_grp = SWA_KV_PER_STEP
    cur = lambda b, g, i, s: (b, g, i, 0)
    prev = lambda b, g, i, s: (b, g, jnp.maximum(i * per - 1, 0), 0)
    grid_spec = pltpu.PrefetchScalarGridSpec(
        num_scalar_prefetch=1,
        grid=(batch, N_KV_HEADS // n_grp, seq // TQ_SWA),
        in_specs=[pl.BlockSpec((1, n_grp * GROUP, TQ_SWA, LANES), cur),
                  pl.BlockSpec((1, n_grp, TQ_SWA, LANES), cur),
                  pl.BlockSpec((1, n_grp, SWA_WINDOW, LANES), prev),
                  pl.BlockSpec((1, n_grp, TQ_SWA, LANES), cur),
                  pl.BlockSpec((1, n_grp, SWA_WINDOW, LANES), prev)],
        out_specs=pl.BlockSpec((1, TQ_SWA, n_grp * GROUP * HEAD_DIM), lambda b, g, i, s: (b, i, g)),
    )
    return pl.pallas_call(
        _swa_kernel,
        grid_spec=grid_spec,
        out_shape=jax.ShapeDtypeStruct((batch, seq, N_HEADS * HEAD_DIM), bf16),
        compiler_params=_cparams("parallel", "parallel", "arbitrary"),
        name="swa_attention",
    )(sinks, q, k, k, v, v)


def _layer_norm(x, g, b):
    mu = jnp.mean(x, axis=-1, keepdims=True)
    xc = x - mu
    var = jnp.mean(xc * xc, axis=-1, keepdims=True)
    return xc * lax.rsqrt(var + LN_EPS) * g + b


def _load_rows(ref, n_rows):
    return jnp.concatenate([ref[pl.ds(k, n_rows, stride=ROW_TILE), :] for k in range(ROW_TILE)], axis=1)


def _store_rows(ref, val):
    for k in range(ROW_TILE):
        ref[pl.ds(k, val.shape[0], stride=ROW_TILE), :] = val[:, k * LANES:(k + 1) * LANES]


def _mix_router_kernel(x_ref, a_ref, wo_ref, g_ref, b_ref, wr_ref, x1_ref, meta_ref, metat_ref, cnt_ref):
    tm = x_ref.shape[0]
    y = ALPHA * x_ref[...] + jnp.dot(a_ref[...], wo_ref[...], preferred_element_type=f32)
    x1 = _layer_norm(y, g_ref[...], b_ref[...])
    _store_rows(x1_ref, x1)

    @pl.when(pl.program_id(0) == 0)
    def _():
        cnt_ref[...] = jnp.zeros_like(cnt_ref)

    logits = jnp.dot(x1, wr_ref[...], preferred_element_type=f32)
    lane = lax.broadcasted_iota(jnp.int32, (tm, LANES), 1)
    lg = jnp.where(lane < N_EXPERTS, logits, -jnp.inf)
    m1 = jnp.max(lg, axis=1, keepdims=True)
    i1 = jnp.min(jnp.where(lg == m1, lane, LANES), axis=1, keepdims=True)
    lg2 = jnp.where(lane == i1, -jnp.inf, lg)
    m2 = jnp.max(lg2, axis=1, keepdims=True)
    i2 = jnp.min(jnp.where(lg2 == m2, lane, LANES), axis=1, keepdims=True)
    e2 = jnp.exp(m2 - m1)
    w1 = 1.0 / (1.0 + e2)
    w2 = e2 / (1.0 + e2)
    hit1 = lane == i1
    hit2 = lane == i2
    hits = jnp.where(jnp.logical_or(hit1, hit2), 1.0, 0.0)
    r_i = lax.broadcasted_iota(jnp.int32, (tm, tm), 0)
    c_i = lax.broadcasted_iota(jnp.int32, (tm, tm), 1)
    before = jnp.where(c_i < r_i, 1.0, 0.0).astype(bf16)
    cum = jnp.dot(before, hits.astype(bf16), preferred_element_type=f32) + cnt_ref[0:1, :]
    r1 = jnp.sum(jnp.where(hit1, cum, 0.0), axis=1, keepdims=True)
    r2 = jnp.sum(jnp.where(hit2, cum, 0.0), axis=1, keepdims=True)
    cnt_ref[...] = cnt_ref[...] + jnp.sum(hits, axis=0, keepdims=True)
    meta = jnp.where(lane == 0, i1.astype(f32), 0.0)
    meta = jnp.where(lane == 1, i2.astype(f32), meta)
    meta = jnp.where(lane == 2, w1, meta)
    meta = jnp.where(lane == 3, w2, meta)
    meta = jnp.where(lane == 4, r1, meta)
    meta = jnp.where(lane == 5, r2, meta)
    meta_ref[...] = meta
    metat_ref[...] = meta.T[0:8, :]


def _mix_router(x2d, a2d, w_o, g, b, w_router):
    n = x2d.shape[0]
    tok = lambda i: (i, 0)
    const = lambda i: (0, 0)
    wr = jnp.zeros((D_MODEL, LANES), f32).at[:, :N_EXPERTS].set(w_router)
    return pl.pallas_call(
        _mix_router_kernel, grid=(n // TM_MIX,),
        in_specs=[pl.BlockSpec((TM_MIX, D_MODEL), tok),
                  pl.BlockSpec((TM_MIX, D_MODEL), tok),
                  pl.BlockSpec((D_MODEL, D_MODEL), const),
                  pl.BlockSpec((1, D_MODEL), const),
                  pl.BlockSpec((1, D_MODEL), const),
                  pl.BlockSpec((D_MODEL, LANES), const)],
        out_specs=[pl.BlockSpec((TM_MIX * ROW_TILE, LANES), tok),
                   pl.BlockSpec((TM_MIX, LANES), tok), pl.BlockSpec((8, TM_MIX), lambda i: (0, i)),
                   pl.BlockSpec((8, LANES), const)],
        out_shape=[jax.ShapeDtypeStruct((n * ROW_TILE, LANES), f32),
                   jax.ShapeDtypeStruct((n, LANES), f32), jax.ShapeDtypeStruct((8, n), f32),
                   jax.ShapeDtypeStruct((8, LANES), f32)],
        compiler_params=_cparams("arbitrary"), name="mix_ln_router",
    )(x2d, a2d, w_o, g, b, wr)


def _swiglu_accumulate(acc_ref, xb, wg, wu, wd, fc):
    gate = jnp.dot(xb, wg.astype(bf16), preferred_element_type=f32)
    up = jnp.dot(xb, wu.astype(bf16), preferred_element_type=f32)
    h = (gate / (1.0 + jnp.exp(-gate)) * up).astype(bf16)
    part = jnp.dot(h, wd.astype(bf16), preferred_element_type=f32)

    @pl.when(fc == 0)
    def _():
        acc_ref[...] = part

    @pl.when(fc > 0)
    def _():
        acc_ref[...] = acc_ref[...] + part


def _moe_ffn_kernel(te_ref, nact_ref, x_ref, wg_ref, wu_ref, wd_ref, y_ref, xb_ref, acc_ref):
    n = pl.program_id(0)
    fc = pl.program_id(1)
    active = n < nact_ref[0]

    @pl.when(jnp.logical_and(jnp.logical_not(active), fc == 0))
    def _():
        y_ref[...] = jnp.zeros_like(y_ref)

    @pl.when(active)
    def _():
        @pl.when(fc == 0)
        def _():
            xb_ref[...] = _load_rows(x_ref, TM_FFN).astype(bf16)

        _swiglu_accumulate(acc_ref, xb_ref[...], wg_ref[0], wu_ref[0], wd_ref[0], fc)

        @pl.when(fc == pl.num_programs(1) - 1)
        def _():
            _store_rows(y_ref, acc_ref[...])


def _moe_ffn(xs_rows, w_gate, w_up, w_down, tile_expert, n_active):
    blk = (TM_FFN * ROW_TILE, LANES)
    n_tiles = xs_rows.shape[0] // blk[0]
    n_fc = D_FF // FC_FFN

    def row(n, f, te, na):
        return jnp.minimum(n, na[0] - 1), 0

    def w_in(n, f, te, na):
        return te[jnp.minimum(n, na[0] - 1)], 0, jnp.where(n < na[0], f, n_fc - 1)

    def w_out(n, f, te, na):
        return te[jnp.minimum(n, na[0] - 1)], jnp.where(n < na[0], f, n_fc - 1), 0

    grid_spec = pltpu.PrefetchScalarGridSpec(
        num_scalar_prefetch=2,
        grid=(n_tiles, n_fc),
        in_specs=[pl.BlockSpec(blk, row),
                  pl.BlockSpec((1, D_MODEL, FC_FFN), w_in),
                  pl.BlockSpec((1, D_MODEL, FC_FFN), w_in),
                  pl.BlockSpec((1, FC_FFN, D_MODEL), w_out)],
        out_specs=pl.BlockSpec(blk, lambda n, f, te, na: (n, 0)),
        scratch_shapes=[pltpu.VMEM((TM_FFN, D_MODEL), bf16), pltpu.VMEM((TM_FFN, D_MODEL), f32)],
    )
    return pl.pallas_call(
        _moe_ffn_kernel, grid_spec=grid_spec,
        out_shape=jax.ShapeDtypeStruct(xs_rows.shape, f32),
        compiler_params=_cparams("arbitrary", "arbitrary"), name="moe_swiglu",
    )(tile_expert, n_active, xs_rows, w_gate, w_up, w_down)


def _dense_tail_kernel(x_ref, a_ref, wo_ref, g1_ref, b1_ref, wg_ref, wu_ref, wd_ref, g2_ref, b2_ref,
                       p_ref, wp_ref, wgate_ref, o_ref, x1_ref, xb_ref):
    fc = pl.program_id(1)
    acc_ref = o_ref

    @pl.when(fc == 0)
    def _():
        y = ALPHA * x_ref[...] + jnp.dot(a_ref[...], wo_ref[...], preferred_element_type=f32)
        x1 = _layer_norm(y, g1_ref[...], b1_ref[...])
        x1_ref[...] = x1
        xb_ref[...] = x1.astype(bf16)

    _swiglu_accumulate(acc_ref, xb_ref[...], wg_ref[...], wu_ref[...], wd_ref[...], fc)

    @pl.when(fc == pl.num_programs(1) - 1)
    def _():
        x2 = _layer_norm(ALPHA * x1_ref[...] + acc_ref[...], g2_ref[...], b2_ref[...])
        _ple(x2, p_ref, wp_ref, wgate_ref, o_ref)


def _dense_tail(x2d, a2d, w_o, g1, b1, w_gate, w_up, w_down, g2, b2, p2d, w_proj, w_pgate):
    n = x2d.shape[0]
    tok = lambda i, f: (i, 0)
    const = lambda i, f: (0, 0)
    once = dict(pipeline_mode=pl.Buffered(1))
    return pl.pallas_call(
        _dense_tail_kernel, grid=(n // TM_DENSE, D_FF // FC_FFN),
        in_specs=[pl.BlockSpec((TM_DENSE, D_MODEL), tok),
                  pl.BlockSpec((TM_DENSE, D_MODEL), tok),
                  pl.BlockSpec((D_MODEL, D_MODEL), const, **once),
                  pl.BlockSpec((1, D_MODEL), const, **once),
                  pl.BlockSpec((1, D_MODEL), const, **once),
                  pl.BlockSpec((D_MODEL, FC_FFN), lambda i, f: (0, f)),
                  pl.BlockSpec((D_MODEL, FC_FFN), lambda i, f: (0, f)),
                  pl.BlockSpec((FC_FFN, D_MODEL), lambda i, f: (f, 0)),
                  pl.BlockSpec((1, D_MODEL), const, **once),
                  pl.BlockSpec((1, D_MODEL), const, **once),
                  pl.BlockSpec((TM_DENSE, PLE_DIM), tok),
                  pl.BlockSpec((PLE_DIM, D_MODEL), const, **once),
                  pl.BlockSpec((D_MODEL, D_MODEL), const, **once)],
        out_specs=pl.BlockSpec((TM_DENSE, D_MODEL), tok),
        out_shape=jax.ShapeDtypeStruct((n, D_MODEL), f32),
        scratch_shapes=[pltpu.VMEM((TM_DENSE, D_MODEL), f32), pltpu.VMEM((TM_DENSE, D_MODEL), bf16)],
        compiler_params=pltpu.CompilerParams(dimension_semantics=("parallel", "arbitrary"),
                                             vmem_limit_bytes=MOBA_VMEM_LIMIT), name="dense_tail",
    )(x2d, a2d, w_o, g1, b1, w_gate, w_up, w_down, g2, b2, p2d, w_proj, w_pgate)


def _dispatch_kernel(last_ref, nact_ref, pos_ref, x_ref, xs_ref, zero_ref, sem, zsem):
    tm = x_ref.shape[0] // ROW_TILE
    n_tiles = xs_ref.shape[0] // zero_ref.shape[0]

    @pl.when(pl.program_id(0) == 0)
    def _():
        zero_ref[...] = jnp.zeros_like(zero_ref)

        def fill(t):
            dst = pl.multiple_of(t * zero_ref.shape[0], zero_ref.shape[0])
            return pltpu.make_async_copy(zero_ref, xs_ref.at[pl.ds(dst, zero_ref.shape[0])], zsem)

        def start_tail(t, _):
            fill(t).start()
            return 0

        def wait_tail(t, _):
            fill(t).wait()
            return 0

        for e in range(N_EXPERTS):
            pl.when(last_ref[e] >= 0)(lambda e=e: fill(last_ref[e]).start())
        lax.fori_loop(nact_ref[0], n_tiles, start_tail, 0)
        for e in range(N_EXPERTS):
            pl.when(last_ref[e] >= 0)(lambda e=e: fill(last_ref[e]).wait())
        lax.fori_loop(nact_ref[0], n_tiles, wait_tail, 0)

    def start(r, _):
        src = x_ref.at[pl.ds(pl.multiple_of(r * ROW_TILE, ROW_TILE), ROW_TILE)]
        for k in range(2):
            dst = pl.multiple_of(pos_ref[2 * r + k] * ROW_TILE, ROW_TILE)
            pltpu.make_async_copy(src, xs_ref.at[pl.ds(dst, ROW_TILE)], sem).start(priority=k)
        return 0

    lax.fori_loop(0, tm, start, 0, unroll=DMA_UNROLL)
    for _ in range(2):
        pltpu.make_async_copy(x_ref, xs_ref.at[pl.ds(0, tm * ROW_TILE)], sem).wait()


def _dispatch(x1_rows, pos_flat, last_tile, n_active, n_tiles):
    n = x1_rows.shape[0] // ROW_TILE
    grid_spec = pltpu.PrefetchScalarGridSpec(
        num_scalar_prefetch=2,
        grid=(n // TM_DISPATCH,),
        in_specs=[pl.BlockSpec((2 * TM_DISPATCH,), lambda i, lt, na: (i,), memory_space=pltpu.SMEM),
                  pl.BlockSpec((TM_DISPATCH * ROW_TILE, LANES), lambda i, lt, na: (i, 0))],
        out_specs=pl.BlockSpec(memory_space=pl.ANY),
        scratch_shapes=[pltpu.VMEM((TM_FFN * ROW_TILE, LANES), f32), pltpu.SemaphoreType.DMA(()),
                        pltpu.SemaphoreType.DMA(())],
    )
    return pl.pallas_call(
        _dispatch_kernel, grid_spec=grid_spec,
        out_shape=jax.ShapeDtypeStruct((n_tiles * TM_FFN * ROW_TILE, LANES), f32),
        compiler_params=_cparams("arbitrary"), name="moe_dispatch",
    )(last_tile, n_active, pos_flat, x1_rows)


def _ple(x2, p_ref, wp_ref, wgate_ref, o_ref):
    pe = jnp.dot(p_ref[...].astype(bf16), wp_ref[...], preferred_element_type=f32)
    z = jnp.dot(x2.astype(bf16), wgate_ref[...], preferred_element_type=f32)
    o_ref[...] = x2 + pe / (1.0 + jnp.exp(-z))


def _out_moe_kernel(pos_ref, nxt_ref, x1_ref, meta_ref, y_ref, g_ref, b_ref, p_ref, wp_ref, wgate_ref, o_ref,
                    buf_ref, sems):
    i = pl.program_id(0)
    tm = x1_ref.shape[0] // ROW_TILE
    slot = i % 2

    def gather(idx_ref, to_slot):
        def start(r, _):
            dst = pl.ds(pl.multiple_of(r * ROW_TILE, ROW_TILE), ROW_TILE)
            for k in range(2):
                src = pl.multiple_of(idx_ref[2 * r + k] * ROW_TILE, ROW_TILE)
                pltpu.make_async_copy(y_ref.at[pl.ds(src, ROW_TILE)], buf_ref.at[to_slot, k, dst],
                                      sems.at[to_slot]).start(priority=k)
            return 0

        lax.fori_loop(0, tm, start, 0, unroll=DMA_UNROLL)

    @pl.when(i == 0)
    def _():
        gather(pos_ref, slot)

    @pl.when(i + 1 < pl.num_programs(0))
    def _():
        gather(nxt_ref, 1 - slot)

    for k in range(2):
        pltpu.make_async_copy(y_ref.at[pl.ds(0, tm * ROW_TILE)], buf_ref.at[slot, k], sems.at[slot]).wait()
    meta = meta_ref[...]
    f = meta[:, 2:3] * _load_rows(buf_ref.at[slot, 0], tm) + meta[:, 3:4] * _load_rows(buf_ref.at[slot, 1], tm)
    x2 = _layer_norm(ALPHA * _load_rows(x1_ref, tm) + f, g_ref[...], b_ref[...])
    _ple(x2, p_ref, wp_ref, wgate_ref, o_ref)


def _out_moe(x1_rows, meta, y_rows, pos_flat, g, b, p2d, w_proj, w_gate):
    n = x1_rows.shape[0] // ROW_TILE
    steps = n // TM_OUT
    tok = lambda i: (i, 0)
    const = lambda i: (0, 0)
    return pl.pallas_call(
        _out_moe_kernel, grid=(steps,),
        in_specs=[pl.BlockSpec((2 * TM_OUT,), lambda i: (i,), memory_space=pltpu.SMEM),
                  pl.BlockSpec((2 * TM_OUT,), lambda i: (jnp.minimum(i + 1, steps - 1),), memory_space=pltpu.SMEM),
                  pl.BlockSpec((TM_OUT * ROW_TILE, LANES), tok),
                  pl.BlockSpec((TM_OUT, LANES), tok),
                  pl.BlockSpec(memory_space=pl.ANY),
                  pl.BlockSpec((1, D_MODEL), const),
                  pl.BlockSpec((1, D_MODEL), const),
                  pl.BlockSpec((TM_OUT, PLE_DIM), tok),
                  pl.BlockSpec((PLE_DIM, D_MODEL), const),
                  pl.BlockSpec((D_MODEL, D_MODEL), const)],
        out_specs=pl.BlockSpec((TM_OUT, D_MODEL), tok),
        out_shape=jax.ShapeDtypeStruct((n, D_MODEL), f32),
        scratch_shapes=[pltpu.VMEM((2, 2, TM_OUT * ROW_TILE, LANES), f32), pltpu.SemaphoreType.DMA((2,))],
        compiler_params=_cparams("arbitrary"), name="moe_combine_ln_ple",
    )(pos_flat, pos_flat, x1_rows, meta, y_rows, g, b, p2d, w_proj, w_gate)


def _routing_plan(meta_t, counts_row, n_tiles):
    i1 = meta_t[0].astype(jnp.int32)
    i2 = meta_t[1].astype(jnp.int32)
    r1 = meta_t[4].astype(jnp.int32)
    r2 = meta_t[5].astype(jnp.int32)
    counts = counts_row[0, :N_EXPERTS].astype(jnp.int32)
    tiles = (counts + TM_FFN - 1) // TM_FFN
    tile_end = jnp.cumsum(tiles)
    offset = (tile_end - tiles) * TM_FFN
    pos = jnp.stack([offset[i1] + r1, offset[i2] + r2], axis=1).reshape(-1)
    tile_expert = jnp.sum(jnp.arange(n_tiles)[:, None] >= tile_end[None, :], axis=1)
    tile_expert = jnp.minimum(tile_expert, N_EXPERTS - 1)
    last_tile = jnp.where(tiles > 0, tile_end - 1, -1)
    return (pos.astype(jnp.int32), tile_expert.astype(jnp.int32), tile_end[-1:].astype(jnp.int32),
            last_tile.astype(jnp.int32))


def kernel(x, p, positions, w_qkv, w_o, ln_mix_g, ln_mix_b, ln_ffn_g, ln_ffn_b, sinks, w_ffn_gate, w_ffn_up,
           w_ffn_down, w_router, w_exp_gate, w_exp_up, w_exp_down, w_ple_proj, w_ple_gate):
    batch, seq, _ = x.shape
    n = batch * seq
    assert seq % TQ_SWA == 0 and (seq // MOBA_BLOCK) % 8 == 0 and seq // MOBA_BLOCK <= LANES - HEAD_DIM
    tabs = _rope_tables(positions)
    x2d = x.reshape(n, D_MODEL)
    row = lambda v: v.reshape(1, D_MODEL)
    for i in range(DEPTH):
        j = i // 2
        q, k, v = _qkv(x2d, w_qkv[i].astype(bf16), tabs, batch, seq)
        if i % 2 == 0:
            a = _moba(q, k, v)
        else:
            a = _swa(q, k, v, sinks[j])
        a2d = a.reshape(n, D_MODEL)
        wo = w_o[i].astype(bf16)
        ple_args = (row(ln_ffn_g[i]), row(ln_ffn_b[i]), p[i].reshape(n, PLE_DIM),
                    w_ple_proj[i].astype(bf16), w_ple_gate[i].astype(bf16))
        if i % 2 == 0:
            x2d = _dense_tail(x2d, a2d, wo, row(ln_mix_g[i]), row(ln_mix_b[i]), w_ffn_gate[j].astype(bf16),
                              w_ffn_up[j].astype(bf16), w_ffn_down[j].astype(bf16), *ple_args)
        else:
            x1, meta, meta_t, counts = _mix_router(x2d, a2d, wo, row(ln_mix_g[i]), row(ln_mix_b[i]), w_router[j])
            n_tiles = 2 * n // TM_FFN + N_EXPERTS
            pos, tile_expert, n_active, last_tile = _routing_plan(meta_t, counts, n_tiles)
            xs = _dispatch(x1, pos, last_tile, n_active, n_tiles)
            y = _moe_ffn(xs, w_exp_gate[j], w_exp_up[j], w_exp_down[j], tile_expert, n_active)
            x2d = _out_moe(x1, meta, y, pos, *ple_args)
    return x2d.reshape(batch, seq, D_MODEL)
```

```python
import functools

import jax
import jax.numpy as jnp
import numpy as np
from jax import lax
from jax.experimental import pallas as pl
from jax.experimental.pallas import tpu as pltpu

D_MODEL = 1024
N_HEADS = 16
N_KV_HEADS = 4
HEAD_DIM = 64
GROUP = N_HEADS // N_KV_HEADS
QKV_DIM = (N_HEADS + 2 * N_KV_HEADS) * HEAD_DIM
ROT_DIM = 16
ROPE_THETA = 500000.0
MOBA_BLOCK = 256
MOBA_TOPK = 3
SWA_WINDOW = 128
D_FF = 3584
N_EXPERTS = 8
PLE_DIM = 256
LN_EPS = 1e-5
DEPTH = 2
ALPHA = (2.0 * DEPTH) ** 0.25
LOG2E = 1.4426950408889634
Q_SCALE = HEAD_DIM ** -0.5 * LOG2E

LANES = 128
ROW_TILE = D_MODEL // LANES
DMA_UNROLL = 8
NEG = -1e30
VMEM_LIMIT = 48 * 1024 * 1024

TM_QKV = 512
TM_MIX = 512
TM_DENSE = 1024
TM_FFN = 1024
FC_FFN = 512
TM_DISPATCH = 512
TM_OUT = 256
TQ_SWA = 1024
SWA_KV_PER_STEP = 2
MOBA_SUB = 512
MOBA_KEYS = 2 * MOBA_BLOCK
VT_ROWS = 80
MOBA_KV_PER_STEP = 2
MOBA_VMEM_LIMIT = 56 * 1024 * 1024

f32 = jnp.float32
bf16 = jnp.bfloat16


def _cparams(*sem):
    return pltpu.CompilerParams(dimension_semantics=sem, vmem_limit_bytes=VMEM_LIMIT)


def _rope_kernel(pos_ref, inv_ref, cos_ref, sin_ref):
    ang = pos_ref[...].astype(f32) * inv_ref[...]
    cos_ref[...] = jnp.cos(ang)
    sin_ref[...] = jnp.sin(ang)


def _rope_tables(positions):
    n = positions.size
    half = ROT_DIM // 2
    inv = 1.0 / (ROPE_THETA ** (jnp.arange(0, ROT_DIM, 2, dtype=f32) / ROT_DIM))
    rows = n * half // LANES
    pos_rep = jnp.repeat(positions.reshape(-1), half).reshape(rows, LANES)
    inv_rep = jnp.tile(inv, LANES // half).reshape(1, LANES)
    cos, sin = pl.pallas_call(
        _rope_kernel,
        out_shape=(jax.ShapeDtypeStruct((rows, LANES), f32),) * 2,
        name="rope_tables",
    )(pos_rep, inv_rep)
    cos = jnp.tile(cos.reshape(n, half), (1, LANES // half))
    sin = jnp.tile(sin.reshape(n, half), (1, LANES // half))
    d = jnp.arange(LANES) % HEAD_DIM
    c_tab = jnp.where(d < ROT_DIM, cos, 1.0)
    s_lo = jnp.where(d < half, -sin, 0.0)
    s_hi = jnp.where((d >= half) & (d < ROT_DIM), sin, 0.0)
    return c_tab, s_lo, s_hi


def _qkv_kernel(x_ref, w_ref, c_ref, slo_ref, shi_ref, q_ref, k_ref, v_ref, *, tiles_per_seq):
    tm = x_ref.shape[0]
    acc = jnp.dot(x_ref[...].astype(bf16), w_ref[...], preferred_element_type=f32)
    lane = lax.broadcasted_iota(jnp.int32, (tm, LANES), 1)
    row = lax.broadcasted_iota(jnp.int32, (tm, LANES), 0)
    low = lane < HEAD_DIM
    c_tab, s_lo, s_hi = c_ref[...], slo_ref[...], shi_ref[...]
    seq0 = (pl.program_id(0) % tiles_per_seq) * tm
    blk = (seq0 + row) // MOBA_BLOCK
    k_pad = jnp.where(lane == HEAD_DIM + blk, 1.0, 0.0)
    v_pad = jnp.where(lane == HEAD_DIM, 1.0, 0.0)
    n_q = N_HEADS // 2
    n_kv = N_KV_HEADS // 2
    for c in range(n_q + n_kv):
        xc = acc[:, c * LANES:(c + 1) * LANES]
        r = xc * c_tab + pltpu.roll(xc, LANES - ROT_DIM // 2, 1) * s_lo + pltpu.roll(xc, ROT_DIM // 2, 1) * s_hi
        r_odd = pltpu.roll(r, HEAD_DIM, 1)
        if c < n_q:
            q_ref[0, 2 * c] = jnp.where(low, r * Q_SCALE, 0.0).astype(bf16)
            q_ref[0, 2 * c + 1] = jnp.where(low, r_odd * Q_SCALE, 0.0).astype(bf16)
        else:
            k_ref[0, 2 * (c - n_q)] = jnp.where(low, r, k_pad).astype(bf16)
            k_ref[0, 2 * (c - n_q) + 1] = jnp.where(low, r_odd, k_pad).astype(bf16)
    for c in range(n_kv):
        xc = acc[:, (n_q + n_kv + c) * LANES:(n_q + n_kv + c + 1) * LANES]
        v_ref[0, 2 * c] = jnp.where(low, xc, v_pad).astype(bf16)
        v_ref[0, 2 * c + 1] = jnp.where(low, pltpu.roll(xc, HEAD_DIM, 1), v_pad).astype(bf16)


def _qkv(x2d, w_qkv, tabs, batch, seq):
    n = x2d.shape[0]
    nt = seq // TM_QKV
    tok = lambda i: (i, 0)
    head = lambda i: (i // nt, 0, i % nt, 0)
    return pl.pallas_call(
        functools.partial(_qkv_kernel, tiles_per_seq=nt),
        grid=(n // TM_QKV,),
        in_specs=[pl.BlockSpec((TM_QKV, D_MODEL), tok),
                  pl.BlockSpec((D_MODEL, QKV_DIM), lambda i: (0, 0)),
                  pl.BlockSpec((TM_QKV, LANES), tok),
                  pl.BlockSpec((TM_QKV, LANES), tok),
                  pl.BlockSpec((TM_QKV, LANES), tok)],
        out_specs=[pl.BlockSpec((1, N_HEADS, TM_QKV, LANES), head),
                   pl.BlockSpec((1, N_KV_HEADS, TM_QKV, LANES), head),
                   pl.BlockSpec((1, N_KV_HEADS, TM_QKV, LANES), head)],
        out_shape=[jax.ShapeDtypeStruct((batch, N_HEADS, seq, LANES), bf16),
                   jax.ShapeDtypeStruct((batch, N_KV_HEADS, seq, LANES), bf16),
                   jax.ShapeDtypeStruct((batch, N_KV_HEADS, seq, LANES), bf16)],
        compiler_params=_cparams("parallel"),
        name="qkv_rope",
    )(x2d, w_qkv, *tabs)


def _merge_heads(o, out_ref, rows):
    lane = lax.broadcasted_iota(jnp.int32, (rows, LANES), 1)
    low = lane < HEAD_DIM
    for c in range(o.shape[0] // rows // 2):
        even = o[(2 * c) * rows:(2 * c + 1) * rows]
        odd = o[(2 * c + 1) * rows:(2 * c + 2) * rows]
        out_ref[0, :, c * LANES:(c + 1) * LANES] = jnp.where(low, even, pltpu.roll(odd, HEAD_DIM, 1)).astype(out_ref.dtype)


def _moba_kernel(q_ref, k_ref, v_ref, o_ref, kmean_ref, vt_ref, qa_ref, s0_ref, s1_ref, m_ref, acc_ref):
    u = pl.program_id(2)
    blk = MOBA_BLOCK
    tq = q_ref.shape[2]
    nb = k_ref.shape[2] // blk
    n_grp = k_ref.shape[1]
    grp_rows = GROUP * tq
    rows = n_grp * grp_rows
    sub = MOBA_SUB
    n_sub = rows // sub
    nt_dims = (((1,), (1,)), ((), ()))

    @pl.when(u == 0)
    def _():
        for kv in range(n_grp):
            for j in range(nb):
                kj = k_ref[0, kv, j * blk:(j + 1) * blk, :].astype(f32)
                kmean_ref[kv, j:j + 1, :] = jnp.sum(kj, axis=0, keepdims=True) * (1.0 / blk)
                vj = v_ref[0, kv, j * blk:(j + 1) * blk, :].astype(f32)
                vt_ref[kv, :, j * blk:(j + 1) * blk] = vj.T.astype(bf16)

    for kv in range(n_grp):
        q = q_ref[0, kv * GROUP:(kv + 1) * GROUP].reshape(grp_rows, LANES)
        km = kmean_ref[kv]
        km_hi = km.astype(bf16)
        km_lo = (km - km_hi.astype(f32)).astype(bf16)
        g = (lax.dot_general(km_hi, q, nt_dims, preferred_element_type=f32)
             + lax.dot_general(km_lo, q, nt_dims, preferred_element_type=f32))
        jidx = lax.broadcasted_iota(jnp.int32, (nb, grp_rows), 0)
        q_blk = u * (tq // blk) + (lax.broadcasted_iota(jnp.int32, (nb, grp_rows), 1) % tq) // blk
        past = jidx < q_blk
        g = jnp.where(past, g, -jnp.inf)
        sel = jidx == q_blk
        for _ in range(MOBA_TOPK):
            mx = jnp.max(g, axis=0, keepdims=True)
            first = jnp.min(jnp.where(g == mx, jidx, nb), axis=0, keepdims=True)
            pick = jidx == first
            sel = jnp.logical_or(sel, jnp.logical_and(pick, past))
            g = jnp.where(pick, -jnp.inf, g)
        bias = jnp.where(sel, 0.0, NEG)
        bias_t = jnp.concatenate([jnp.zeros((HEAD_DIM, grp_rows), f32), bias,
                                  jnp.zeros((LANES - HEAD_DIM - nb, grp_rows), f32)], axis=0)
        qa_ref[kv * grp_rows:(kv + 1) * grp_rows] = (q.astype(f32) + bias_t.T).astype(bf16)

    m_ref[...] = jnp.full_like(m_ref, NEG)
    acc_ref[...] = jnp.zeros_like(acc_ref)

    def scores(c, s_ref, r):
        o = pl.multiple_of(c * MOBA_KEYS, MOBA_KEYS)
        rs = slice(r * sub, (r + 1) * sub)
        kv = r * sub // grp_rows
        s_ref[r] = lax.dot_general(k_ref[0, kv, pl.ds(o, MOBA_KEYS), :], qa_ref[rs], nt_dims,
                                   preferred_element_type=f32)

    def accumulate(c, s_ref, r, causal):
        o = pl.multiple_of(c * MOBA_KEYS, MOBA_KEYS)
        kv = r * sub // grp_rows
        s = s_ref[r]
        if causal:
            key = lax.broadcasted_iota(jnp.int32, (MOBA_KEYS, sub), 0)
            tok = (lax.broadcasted_iota(jnp.int32, (MOBA_KEYS, sub), 1) + r * sub) % tq
            s = jnp.where(jnp.logical_and(key // blk == tok // blk, key > tok), NEG, s)
        m_old = m_ref[r]
        m_new = jnp.maximum(m_old, jnp.max(s, axis=0, keepdims=True))
        p = jnp.exp2(s - m_new).astype(bf16)
        pv = jnp.dot(vt_ref[kv, :, pl.ds(o, MOBA_KEYS)], p, preferred_element_type=f32)
        acc_ref[r] = acc_ref[r] * jnp.exp2(m_old - m_new) + pv
        m_ref[r] = m_new

    def step(c_next, s_next, c_cur, s_cur, causal=False):
        for r in range(n_sub):
            if c_next is not None:
                scores(c_next, s_next, r)
            accumulate(c_cur, s_cur, r, causal)

    for r in range(n_sub):
        scores(0, s0_ref, r)

    def body(t, carry):
        step(2 * t + 1, s1_ref, 2 * t, s0_ref)
        step(2 * t + 2, s0_ref, 2 * t + 1, s1_ref)
        return carry

    lax.fori_loop(0, u // 2, body, 0)

    @pl.when(u % 2 == 1)
    def _():
        step(u, s1_ref, u - 1, s0_ref)
        step(None, None, u, s1_ref, causal=True)

    @pl.when(u % 2 == 0)
    def _():
        step(None, None, u, s0_ref, causal=True)

    heads = []
    for r in range(n_sub):
        a = acc_ref[r]
        heads.append((a / a[HEAD_DIM:HEAD_DIM + 1, :]).T)
    _merge_heads(jnp.concatenate(heads, axis=0), o_ref, tq)


def _moba(q, k, v):
    batch, _, seq, _ = q.shape
    nb = seq // MOBA_BLOCK
    n_grp = MOBA_KV_PER_STEP
    rows = n_grp * GROUP * MOBA_KEYS
    return pl.pallas_call(
        _moba_kernel,
        grid=(batch, N_KV_HEADS // n_grp, seq // MOBA_KEYS),
        in_specs=[pl.BlockSpec((1, n_grp * GROUP, MOBA_KEYS, LANES), lambda b, g, u: (b, g, u, 0)),
                  pl.BlockSpec((1, n_grp, seq, LANES), lambda b, g, u: (b, g, 0, 0)),
                  pl.BlockSpec((1, n_grp, seq, LANES), lambda b, g, u: (b, g, 0, 0))],
        out_specs=pl.BlockSpec((1, MOBA_KEYS, n_grp * GROUP * HEAD_DIM), lambda b, g, u: (b, u, g)),
        out_shape=jax.ShapeDtypeStruct((batch, seq, N_HEADS * HEAD_DIM), bf16),
        scratch_shapes=[pltpu.VMEM((n_grp, nb, LANES), f32),
                        pltpu.VMEM((n_grp, LANES, seq), bf16),
                        pltpu.VMEM((rows, LANES), bf16),
                        pltpu.VMEM((rows // MOBA_SUB, MOBA_KEYS, MOBA_SUB), f32),
                        pltpu.VMEM((rows // MOBA_SUB, MOBA_KEYS, MOBA_SUB), f32),
                        pltpu.VMEM((rows // MOBA_SUB, 1, MOBA_SUB), f32),
                        pltpu.VMEM((rows // MOBA_SUB, LANES, MOBA_SUB), f32)],
        compiler_params=pltpu.CompilerParams(dimension_semantics=("parallel", "parallel", "arbitrary"),
                                             vmem_limit_bytes=MOBA_VMEM_LIMIT),
        name="moba_attention",
    )(q, k, v)


def _swa_kernel(sink_ref, q_ref, kc_ref, kp_ref, vc_ref, vp_ref, o_ref):
    g = pl.program_id(1)
    i = pl.program_id(2)
    w = SWA_WINDOW
    n_grp = kc_ref.shape[1]
    rows = GROUP * w
    nt_dims = (((1,), (1,)), ((), ()))
    n_sb = q_ref.shape[2] // w
    key = lax.broadcasted_iota(jnp.int32, (2 * w, rows), 0)
    t_in = lax.broadcasted_iota(jnp.int32, (2 * w, rows), 1) % w
    band = jnp.logical_and(key > t_in, key <= t_in + w)
    first = jnp.logical_and(band, jnp.logical_or(key >= w, i > 0))
    head_of_row = lax.broadcasted_iota(jnp.int32, (1, rows), 1) // w
    lane = lax.broadcasted_iota(jnp.int32, (w, LANES), 1)
    low = lane < HEAD_DIM

    def window(ref_cur, ref_prev, kv, sb):
        if sb == 0:
            return jnp.concatenate([ref_prev[0, kv], ref_cur[0, kv, 0:w, :]], axis=0)
        return ref_cur[0, kv, (sb - 1) * w:(sb + 1) * w, :]

    work = [(kv, sb) for kv in range(n_grp) for sb in range(n_sb)]
    scores = [lax.dot_general(window(kc_ref, kp_ref, kv, sb),
                              q_ref[0, kv * GROUP:(kv + 1) * GROUP, sb * w:(sb + 1) * w, :].reshape(rows, LANES),
                              nt_dims, preferred_element_type=f32)
              for kv, sb in work]
    v_t = [jnp.concatenate([vp_ref[0, kv].astype(f32).T[0:VT_ROWS], vc_ref[0, kv].astype(f32).T[0:VT_ROWS]],
                           axis=1).astype(bf16) for kv in range(n_grp)]
    for (kv, sb), sc in zip(work, scores):
        sink = jnp.zeros((1, rows), f32)
        for hh in range(GROUP):
            sink = jnp.where(head_of_row == hh, sink_ref[(g * n_grp + kv) * GROUP + hh] * LOG2E, sink)
        s = jnp.where(first if sb == 0 else band, sc, NEG)
        m = jnp.maximum(jnp.max(s, axis=0, keepdims=True), sink)
        p = jnp.exp2(s - m).astype(bf16)
        acc = jnp.dot(v_t[kv][:, sb * w:(sb + 2) * w], p, preferred_element_type=f32)
        out_t = acc / (acc[HEAD_DIM:HEAD_DIM + 1, :] + jnp.exp2(sink - m))
        out = jnp.concatenate([out_t, jnp.zeros((LANES - VT_ROWS, rows), f32)], axis=0).T
        for c in range(GROUP // 2):
            even = out[(2 * c) * w:(2 * c + 1) * w]
            odd = out[(2 * c + 1) * w:(2 * c + 2) * w]
            chunk = kv * (GROUP // 2) + c
            o_ref[0, sb * w:(sb + 1) * w, chunk * LANES:(chunk + 1) * LANES] = jnp.where(
                low, even, pltpu.roll(odd, HEAD_DIM, 1)).astype(o_ref.dtype)


def _swa(q, k, v, sinks):
    batch, _, seq, _ = q.shape
    per = TQ_SWA // SWA_WINDOW
    n_grp = SWA_KV_PER_STEP
    cur = lambda b, g, i, s: (b, g, i, 0)
    prev = lambda b, g, i, s: (b, g, jnp.maximum(i * per - 1, 0), 0)
    grid_spec = pltpu.PrefetchScalarGridSpec(
        num_scalar_prefetch=1,
        grid=(batch, N_KV_HEADS // n_grp, seq // TQ_SWA),
        in_specs=[pl.BlockSpec((1, n_grp * GROUP, TQ_SWA, LANES), cur),
                  pl.BlockSpec((1, n_grp, TQ_SWA, LANES), cur),
                  pl.BlockSpec((1, n_grp, SWA_WINDOW, LANES), prev),
                  pl.BlockSpec((1, n_grp, TQ_SWA, LANES), cur),
                  pl.BlockSpec((1, n_grp, SWA_WINDOW, LANES), prev)],
        out_specs=pl.BlockSpec((1, TQ_SWA, n_grp * GROUP * HEAD_DIM), lambda b, g, i, s: (b, i, g)),
    )
    return pl.pallas_call(
        _swa_kernel,
        grid_spec=grid_spec,
        out_shape=jax.ShapeDtypeStruct((batch, seq, N_HEADS * HEAD_DIM), bf16),
        compiler_params=_cparams("parallel", "parallel", "arbitrary"),
        name="swa_attention",
    )(sinks, q, k, k, v, v)


def _layer_norm(x, g, b):
    mu = jnp.mean(x, axis=-1, keepdims=True)
    xc = x - mu
    var = jnp.mean(xc * xc, axis=-1, keepdims=True)
    return xc * lax.rsqrt(var + LN_EPS) * g + b


def _load_rows(ref, n_rows):
    return jnp.concatenate([ref[pl.ds(k, n_rows, stride=ROW_TILE), :] for k in range(ROW_TILE)], axis=1)


def _store_rows(ref, val):
    for k in range(ROW_TILE):
        ref[pl.ds(k, val.shape[0], stride=ROW_TILE), :] = val[:, k * LANES:(k + 1) * LANES]


def _mix_router_kernel(x_ref, a_ref, wo_ref, g_ref, b_ref, wr_ref, x1_ref, meta_ref, metat_ref, cnt_ref):
    tm = x_ref.shape[0]
    y = ALPHA * x_ref[...] + jnp.dot(a_ref[...], wo_ref[...], preferred_element_type=f32)
    x1 = _layer_norm(y, g_ref[...], b_ref[...])
    _store_rows(x1_ref, x1)

    @pl.when(pl.program_id(0) == 0)
    def _():
        cnt_ref[...] = jnp.zeros_like(cnt_ref)

    logits = jnp.dot(x1, wr_ref[...], preferred_element_type=f32)
    lane = lax.broadcasted_iota(jnp.int32, (tm, LANES), 1)
    lg = jnp.where(lane < N_EXPERTS, logits, -jnp.inf)
    m1 = jnp.max(lg, axis=1, keepdims=True)
    i1 = jnp.min(jnp.where(lg == m1, lane, LANES), axis=1, keepdims=True)
    lg2 = jnp.where(lane == i1, -jnp.inf, lg)
    m2 = jnp.max(lg2, axis=1, keepdims=True)
    i2 = jnp.min(jnp.where(lg2 == m2, lane, LANES), axis=1, keepdims=True)
    e2 = jnp.exp(m2 - m1)
    w1 = 1.0 / (1.0 + e2)
    w2 = e2 / (1.0 + e2)
    hit1 = lane == i1
    hit2 = lane == i2
    hits = jnp.where(jnp.logical_or(hit1, hit2), 1.0, 0.0)
    r_i = lax.broadcasted_iota(jnp.int32, (tm, tm), 0)
    c_i = lax.broadcasted_iota(jnp.int32, (tm, tm), 1)
    before = jnp.where(c_i < r_i, 1.0, 0.0).astype(bf16)
    cum = jnp.dot(before, hits.astype(bf16), preferred_element_type=f32) + cnt_ref[0:1, :]
    r1 = jnp.sum(jnp.where(hit1, cum, 0.0), axis=1, keepdims=True)
    r2 = jnp.sum(jnp.where(hit2, cum, 0.0), axis=1, keepdims=True)
    cnt_ref[...] = cnt_ref[...] + jnp.sum(hits, axis=0, keepdims=True)
    meta = jnp.where(lane == 0, i1.astype(f32), 0.0)
    meta = jnp.where(lane == 1, i2.astype(f32), meta)
    meta = jnp.where(lane == 2, w1, meta)
    meta = jnp.where(lane == 3, w2, meta)
    meta = jnp.where(lane == 4, r1, meta)
    meta = jnp.where(lane == 5, r2, meta)
    meta_ref[...] = meta
    metat_ref[...] = meta.T[0:8, :]


def _mix_router(x2d, a2d, w_o, g, b, w_router):
    n = x2d.shape[0]
    tok = lambda i: (i, 0)
    const = lambda i: (0, 0)
    wr = jnp.zeros((D_MODEL, LANES), f32).at[:, :N_EXPERTS].set(w_router)
    return pl.pallas_call(
        _mix_router_kernel, grid=(n // TM_MIX,),
        in_specs=[pl.BlockSpec((TM_MIX, D_MODEL), tok),
                  pl.BlockSpec((TM_MIX, D_MODEL), tok),
                  pl.BlockSpec((D_MODEL, D_MODEL), const),
                  pl.BlockSpec((1, D_MODEL), const),
                  pl.BlockSpec((1, D_MODEL), const),
                  pl.BlockSpec((D_MODEL, LANES), const)],
        out_specs=[pl.BlockSpec((TM_MIX * ROW_TILE, LANES), tok),
                   pl.BlockSpec((TM_MIX, LANES), tok), pl.BlockSpec((8, TM_MIX), lambda i: (0, i)),
                   pl.BlockSpec((8, LANES), const)],
        out_shape=[jax.ShapeDtypeStruct((n * ROW_TILE, LANES), f32),
                   jax.ShapeDtypeStruct((n, LANES), f32), jax.ShapeDtypeStruct((8, n), f32),
                   jax.ShapeDtypeStruct((8, LANES), f32)],
        compiler_params=_cparams("arbitrary"), name="mix_ln_router",
    )(x2d, a2d, w_o, g, b, wr)


def _swiglu_accumulate(acc_ref, xb, wg, wu, wd, fc):
    gate = jnp.dot(xb, wg.astype(bf16), preferred_element_type=f32)
    up = jnp.dot(xb, wu.astype(bf16), preferred_element_type=f32)
    h = (gate / (1.0 + jnp.exp(-gate)) * up).astype(bf16)
    part = jnp.dot(h, wd.astype(bf16), preferred_element_type=f32)

    @pl.when(fc == 0)
    def _():
        acc_ref[...] = part

    @pl.when(fc > 0)
    def _():
        acc_ref[...] = acc_ref[...] + part


def _moe_ffn_kernel(te_ref, nact_ref, x_ref, wg_ref, wu_ref, wd_ref, y_ref, xb_ref, acc_ref):
    n = pl.program_id(0)
    fc = pl.program_id(1)
    active = n < nact_ref[0]

    @pl.when(jnp.logical_and(jnp.logical_not(active), fc == 0))
    def _():
        y_ref[...] = jnp.zeros_like(y_ref)

    @pl.when(active)
    def _():
        @pl.when(fc == 0)
        def _():
            xb_ref[...] = _load_rows(x_ref, TM_FFN).astype(bf16)

        _swiglu_accumulate(acc_ref, xb_ref[...], wg_ref[0], wu_ref[0], wd_ref[0], fc)

        @pl.when(fc == pl.num_programs(1) - 1)
        def _():
            _store_rows(y_ref, acc_ref[...])


def _moe_ffn(xs_rows, w_gate, w_up, w_down, tile_expert, n_active):
    blk = (TM_FFN * ROW_TILE, LANES)
    n_tiles = xs_rows.shape[0] // blk[0]
    n_fc = D_FF // FC_FFN

    def row(n, f, te, na):
        return jnp.minimum(n, na[0] - 1), 0

    def w_in(n, f, te, na):
        return te[jnp.minimum(n, na[0] - 1)], 0, jnp.where(n < na[0], f, n_fc - 1)

    def w_out(n, f, te, na):
        return te[jnp.minimum(n, na[0] - 1)], jnp.where(n < na[0], f, n_fc - 1), 0

    grid_spec = pltpu.PrefetchScalarGridSpec(
        num_scalar_prefetch=2,
        grid=(n_tiles, n_fc),
        in_specs=[pl.BlockSpec(blk, row),
                  pl.BlockSpec((1, D_MODEL, FC_FFN), w_in),
                  pl.BlockSpec((1, D_MODEL, FC_FFN), w_in),
                  pl.BlockSpec((1, FC_FFN, D_MODEL), w_out)],
        out_specs=pl.BlockSpec(blk, lambda n, f, te, na: (n, 0)),
        scratch_shapes=[pltpu.VMEM((TM_FFN, D_MODEL), bf16), pltpu.VMEM((TM_FFN, D_MODEL), f32)],
    )
    return pl.pallas_call(
        _moe_ffn_kernel, grid_spec=grid_spec,
        out_shape=jax.ShapeDtypeStruct(xs_rows.shape, f32),
        compiler_params=_cparams("arbitrary", "arbitrary"), name="moe_swiglu",
    )(tile_expert, n_active, xs_rows, w_gate, w_up, w_down)


def _dense_tail_kernel(x_ref, a_ref, wo_ref, g1_ref, b1_ref, wg_ref, wu_ref, wd_ref, g2_ref, b2_ref,
                       p_ref, wp_ref, wgate_ref, o_ref, x1_ref, xb_ref):
    fc = pl.program_id(1)
    acc_ref = o_ref

    @pl.when(fc == 0)
    def _():
        y = ALPHA * x_ref[...] + jnp.dot(a_ref[...], wo_ref[...], preferred_element_type=f32)
        x1 = _layer_norm(y, g1_ref[...], b1_ref[...])
        x1_ref[...] = x1
        xb_ref[...] = x1.astype(bf16)

    _swiglu_accumulate(acc_ref, xb_ref[...], wg_ref[...], wu_ref[...], wd_ref[...], fc)

    @pl.when(fc == pl.num_programs(1) - 1)
    def _():
        x2 = _layer_norm(ALPHA * x1_ref[...] + acc_ref[...], g2_ref[...], b2_ref[...])
        _ple(x2, p_ref, wp_ref, wgate_ref, o_ref)


def _dense_tail(x2d, a2d, w_o, g1, b1, w_gate, w_up, w_down, g2, b2, p2d, w_proj, w_pgate):
    n = x2d.shape[0]
    tok = lambda i, f: (i, 0)
    const = lambda i, f: (0, 0)
    once = dict(pipeline_mode=pl.Buffered(1))
    return pl.pallas_call(
        _dense_tail_kernel, grid=(n // TM_DENSE, D_FF // FC_FFN),
        in_specs=[pl.BlockSpec((TM_DENSE, D_MODEL), tok),
                  pl.BlockSpec((TM_DENSE, D_MODEL), tok),
                  pl.BlockSpec((D_MODEL, D_MODEL), const, **once),
                  pl.BlockSpec((1, D_MODEL), const, **once),
                  pl.BlockSpec((1, D_MODEL), const, **once),
                  pl.BlockSpec((D_MODEL, FC_FFN), lambda i, f: (0, f)),
                  pl.BlockSpec((D_MODEL, FC_FFN), lambda i, f: (0, f)),
                  pl.BlockSpec((FC_FFN, D_MODEL), lambda i, f: (f, 0)),
                  pl.BlockSpec((1, D_MODEL), const, **once),
                  pl.BlockSpec((1, D_MODEL), const, **once),
                  pl.BlockSpec((TM_DENSE, PLE_DIM), tok),
                  pl.BlockSpec((PLE_DIM, D_MODEL), const, **once),
                  pl.BlockSpec((D_MODEL, D_MODEL), const, **once)],
        out_specs=pl.BlockSpec((TM_DENSE, D_MODEL), tok),
        out_shape=jax.ShapeDtypeStruct((n, D_MODEL), f32),
        scratch_shapes=[pltpu.VMEM((TM_DENSE, D_MODEL), f32), pltpu.VMEM((TM_DENSE, D_MODEL), bf16)],
        compiler_params=pltpu.CompilerParams(dimension_semantics=("parallel", "arbitrary"),
                                             vmem_limit_bytes=MOBA_VMEM_LIMIT), name="dense_tail",
    )(x2d, a2d, w_o, g1, b1, w_gate, w_up, w_down, g2, b2, p2d, w_proj, w_pgate)


def _dispatch_kernel(last_ref, nact_ref, pos_ref, x_ref, xs_ref, zero_ref, sem, zsem):
    tm = x_ref.shape[0] // ROW_TILE
    n_tiles = xs_ref.shape[0] // zero_ref.shape[0]

    @pl.when(pl.program_id(0) == 0)
    def _():
        zero_ref[...] = jnp.zeros_like(zero_ref)

        def fill(t):
            dst = pl.multiple_of(t * zero_ref.shape[0], zero_ref.shape[0])
            return pltpu.make_async_copy(zero_ref, xs_ref.at[pl.ds(dst, zero_ref.shape[0])], zsem)

        def start_tail(t, _):
            fill(t).start()
            return 0

        def wait_tail(t, _):
            fill(t).wait()
            return 0

        for e in range(N_EXPERTS):
            pl.when(last_ref[e] >= 0)(lambda e=e: fill(last_ref[e]).start())
        lax.fori_loop(nact_ref[0], n_tiles, start_tail, 0)
        for e in range(N_EXPERTS):
            pl.when(last_ref[e] >= 0)(lambda e=e: fill(last_ref[e]).wait())
        lax.fori_loop(nact_ref[0], n_tiles, wait_tail, 0)

    def start(r, _):
        src = x_ref.at[pl.ds(pl.multiple_of(r * ROW_TILE, ROW_TILE), ROW_TILE)]
        for k in range(2):
            dst = pl.multiple_of(pos_ref[2 * r + k] * ROW_TILE, ROW_TILE)
            pltpu.make_async_copy(src, xs_ref.at[pl.ds(dst, ROW_TILE)], sem).start(priority=k)
        return 0

    lax.fori_loop(0, tm, start, 0, unroll=DMA_UNROLL)
    for _ in range(2):
        pltpu.make_async_copy(x_ref, xs_ref.at[pl.ds(0, tm * ROW_TILE)], sem).wait()


def _dispatch(x1_rows, pos_flat, last_tile, n_active, n_tiles):
    n = x1_rows.shape[0] // ROW_TILE
    grid_spec = pltpu.PrefetchScalarGridSpec(
        num_scalar_prefetch=2,
        grid=(n // TM_DISPATCH,),
        in_specs=[pl.BlockSpec((2 * TM_DISPATCH,), lambda i, lt, na: (i,), memory_space=pltpu.SMEM),
                  pl.BlockSpec((TM_DISPATCH * ROW_TILE, LANES), lambda i, lt, na: (i, 0))],
        out_specs=pl.BlockSpec(memory_space=pl.ANY),
        scratch_shapes=[pltpu.VMEM((TM_FFN * ROW_TILE, LANES), f32), pltpu.SemaphoreType.DMA(()),
                        pltpu.SemaphoreType.DMA(())],
    )
    return pl.pallas_call(
        _dispatch_kernel, grid_spec=grid_spec,
        out_shape=jax.ShapeDtypeStruct((n_tiles * TM_FFN * ROW_TILE, LANES), f32),
        compiler_params=_cparams("arbitrary"), name="moe_dispatch",
    )(last_tile, n_active, pos_flat, x1_rows)


def _ple(x2, p_ref, wp_ref, wgate_ref, o_ref):
    pe = jnp.dot(p_ref[...].astype(bf16), wp_ref[...], preferred_element_type=f32)
    z = jnp.dot(x2.astype(bf16), wgate_ref[...], preferred_element_type=f32)
    o_ref[...] = x2 + pe / (1.0 + jnp.exp(-z))


def _out_moe_kernel(pos_ref, nxt_ref, x1_ref, meta_ref, y_ref, g_ref, b_ref, p_ref, wp_ref, wgate_ref, o_ref,
                    buf_ref, sems):
    i = pl.program_id(0)
    tm = x1_ref.shape[0] // ROW_TILE
    slot = i % 2

    def gather(idx_ref, to_slot):
        def start(r, _):
            dst = pl.ds(pl.multiple_of(r * ROW_TILE, ROW_TILE), ROW_TILE)
            for k in range(2):
                src = pl.multiple_of(idx_ref[2 * r + k] * ROW_TILE, ROW_TILE)
                pltpu.make_async_copy(y_ref.at[pl.ds(src, ROW_TILE)], buf_ref.at[to_slot, k, dst],
                                      sems.at[to_slot]).start(priority=k)
            return 0

        lax.fori_loop(0, tm, start, 0, unroll=DMA_UNROLL)

    def wait(s):
        for k in range(2):
            pltpu.make_async_copy(y_ref.at[pl.ds(0, tm * ROW_TILE)], buf_ref.at[s, k], sems.at[s]).wait()

    @pl.when(i == 0)
    def _():
        gather(pos_ref, slot)

    wait(slot)
    meta = meta_ref[...]
    f = meta[:, 2:3] * _load_rows(buf_ref.at[slot, 0], tm) + meta[:, 3:4] * _load_rows(buf_ref.at[slot, 1], tm)
    x2 = _layer_norm(ALPHA * _load_rows(x1_ref, tm) + f, g_ref[...], b_ref[...])
    for r in range(tm):
        for k in range(2):
            src = pl.multiple_of(nxt_ref[2 * r + k] * ROW_TILE, ROW_TILE)
            pltpu.make_async_copy(y_ref.at[pl.ds(src, ROW_TILE)], buf_ref.at[1 - slot, k, pl.ds(r * ROW_TILE, ROW_TILE)],
                                  sems.at[1 - slot]).start(priority=k)
    _ple(x2, p_ref, wp_ref, wgate_ref, o_ref)

    @pl.when(i == pl.num_programs(0) - 1)
    def _():
        wait(1 - slot)


def _out_moe(x1_rows, meta, y_rows, pos_flat, g, b, p2d, w_proj, w_gate):
    n = x1_rows.shape[0] // ROW_TILE
    steps = n // TM_OUT
    tok = lambda i: (i, 0)
    const = lambda i: (0, 0)
    return pl.pallas_call(
        _out_moe_kernel, grid=(steps,),
        in_specs=[pl.BlockSpec((2 * TM_OUT,), lambda i: (i,), memory_space=pltpu.SMEM),
                  pl.BlockSpec((2 * TM_OUT,), lambda i: (jnp.minimum(i + 1, steps - 1),), memory_space=pltpu.SMEM),
                  pl.BlockSpec((TM_OUT * ROW_TILE, LANES), tok),
                  pl.BlockSpec((TM_OUT, LANES), tok),
                  pl.BlockSpec(memory_space=pl.ANY),
                  pl.BlockSpec((1, D_MODEL), const),
                  pl.BlockSpec((1, D_MODEL), const),
                  pl.BlockSpec((TM_OUT, PLE_DIM), tok),
                  pl.BlockSpec((PLE_DIM, D_MODEL), const),
                  pl.BlockSpec((D_MODEL, D_MODEL), const)],
        out_specs=pl.BlockSpec((TM_OUT, D_MODEL), tok),
        out_shape=jax.ShapeDtypeStruct((n, D_MODEL), f32),
        scratch_shapes=[pltpu.VMEM((2, 2, TM_OUT * ROW_TILE, LANES), f32), pltpu.SemaphoreType.DMA((2,))],
        compiler_params=_cparams("arbitrary"), name="moe_combine_ln_ple",
    )(pos_flat, pos_flat, x1_rows, meta, y_rows, g, b, p2d, w_proj, w_gate)


def _routing_plan(meta_t, counts_row, n_tiles):
    i1 = meta_t[0].astype(jnp.int32)
    i2 = meta_t[1].astype(jnp.int32)
    r1 = meta_t[4].astype(jnp.int32)
    r2 = meta_t[5].astype(jnp.int32)
    counts = counts_row[0, :N_EXPERTS].astype(jnp.int32)
    tiles = (counts + TM_FFN - 1) // TM_FFN
    tile_end = jnp.cumsum(tiles)
    offset = (tile_end - tiles) * TM_FFN
    pos = jnp.stack([offset[i1] + r1, offset[i2] + r2], axis=1).reshape(-1)
    tile_expert = jnp.sum(jnp.arange(n_tiles)[:, None] >= tile_end[None, :], axis=1)
    tile_expert = jnp.minimum(tile_expert, N_EXPERTS - 1)
    last_tile = jnp.where(tiles > 0, tile_end - 1, -1)
    return (pos.astype(jnp.int32), tile_expert.astype(jnp.int32), tile_end[-1:].astype(jnp.int32),
            last_tile.astype(jnp.int32))


def kernel(x, p, positions, w_qkv, w_o, ln_mix_g, ln_mix_b, ln_ffn_g, ln_ffn_b, sinks, w_ffn_gate, w_ffn_up,
           w_ffn_down, w_router, w_exp_gate, w_exp_up, w_exp_down, w_ple_proj, w_ple_gate):
    batch, seq, _ = x.shape
    n = batch * seq
    assert seq % TQ_SWA == 0 and (seq // MOBA_BLOCK) % 8 == 0 and seq // MOBA_BLOCK <= LANES - HEAD_DIM
    tabs = _rope_tables(positions)
    x2d = x.reshape(n, D_MODEL)
    row = lambda v: v.reshape(1, D_MODEL)
    for i in range(DEPTH):
        j = i // 2
        q, k, v = _qkv(x2d, w_qkv[i].astype(bf16), tabs, batch, seq)
        if i % 2 == 0:
            a = _moba(q, k, v)
        else:
            a = _swa(q, k, v, sinks[j])
        a2d = a.reshape(n, D_MODEL)
        wo = w_o[i].astype(bf16)
        ple_args = (row(ln_ffn_g[i]), row(ln_ffn_b[i]), p[i].reshape(n, PLE_DIM),
                    w_ple_proj[i].astype(bf16), w_ple_gate[i].astype(bf16))
        if i % 2 == 0:
            x2d = _dense_tail(x2d, a2d, wo, row(ln_mix_g[i]), row(ln_mix_b[i]), w_ffn_gate[j].astype(bf16),
                              w_ffn_up[j].astype(bf16), w_ffn_down[j].astype(bf16), *ple_args)
        else:
            x1, meta, meta_t, counts = _mix_router(x2d, a2d, wo, row(ln_mix_g[i]), row(ln_mix_b[i]), w_router[j])
            n_tiles = 2 * n // TM_FFN + N_EXPERTS
            pos, tile_expert, n_active, last_tile = _routing_plan(meta_t, counts, n_tiles)
            xs = _dispatch(x1, pos, last_tile, n_active, n_tiles)
            y = _moe_ffn(xs, w_exp_gate[j], w_exp_up[j], w_exp_down[j], tile_expert, n_active)
            x2d = _out_moe(x1, meta, y, pos, *ple_args)
    return x2d.reshape(batch, seq, D_MODEL)
```

```python
import functools

import jax
import jax.numpy as jnp
import numpy as np
from jax import lax
from jax.experimental import pallas as pl
from jax.experimental.pallas import tpu as pltpu

D_MODEL = 1024
N_HEADS = 16
N_KV_HEADS = 4
HEAD_DIM = 64
GROUP = N_HEADS // N_KV_HEADS
QKV_DIM = (N_HEADS + 2 * N_KV_HEADS) * HEAD_DIM
ROT_DIM = 16
ROPE_THETA = 500000.0
MOBA_BLOCK = 256
MOBA_TOPK = 3
SWA_WINDOW = 128
D_FF = 3584
N_EXPERTS = 8
PLE_DIM = 256
LN_EPS = 1e-5
DEPTH = 2
ALPHA = (2.0 * DEPTH) ** 0.25
LOG2E = 1.4426950408889634
Q_SCALE = HEAD_DIM ** -0.5 * LOG2E

LANES = 128
ROW_TILE = D_MODEL // LANES
DMA_UNROLL = 8
NEG = -1e30
VMEM_LIMIT = 48 * 1024 * 1024

TM_QKV = 1024
TM_MIX = 1024
TM_DENSE = 1024
TM_FFN = 1024
FC_FFN = 512
TM_DISPATCH = 1024
TM_OUT = 256
TQ_SWA = 1024
SWA_KV_PER_STEP = 2
MOBA_SUB = 512
MOBA_KEYS = 2 * MOBA_BLOCK
VT_ROWS = 80
MOBA_KV_PER_STEP = 2
MOBA_VMEM_LIMIT = 56 * 1024 * 1024

f32 = jnp.float32
bf16 = jnp.bfloat16


def _cparams(*sem):
    return pltpu.CompilerParams(dimension_semantics=sem, vmem_limit_bytes=VMEM_LIMIT)


def _rope_kernel(pos_ref, inv_ref, cos_ref, sin_ref):
    ang = pos_ref[...].astype(f32) * inv_ref[...]
    cos_ref[...] = jnp.cos(ang)
    sin_ref[...] = jnp.sin(ang)


def _rope_tables(positions):
    n = positions.size
    half = ROT_DIM // 2
    inv = 1.0 / (ROPE_THETA ** (jnp.arange(0, ROT_DIM, 2, dtype=f32) / ROT_DIM))
    rows = n * half // LANES
    pos_rep = jnp.repeat(positions.reshape(-1), half).reshape(rows, LANES)
    inv_rep = jnp.tile(inv, LANES // half).reshape(1, LANES)
    cos, sin = pl.pallas_call(
        _rope_kernel,
        out_shape=(jax.ShapeDtypeStruct((rows, LANES), f32),) * 2,
        name="rope_tables",
    )(pos_rep, inv_rep)
    cos = jnp.tile(cos.reshape(n, half), (1, LANES // half))
    sin = jnp.tile(sin.reshape(n, half), (1, LANES // half))
    d = jnp.arange(LANES) % HEAD_DIM
    c_tab = jnp.where(d < ROT_DIM, cos, 1.0)
    s_lo = jnp.where(d < half, -sin, 0.0)
    s_hi = jnp.where((d >= half) & (d < ROT_DIM), sin, 0.0)
    return c_tab, s_lo, s_hi


def _qkv_kernel(x_ref, w_ref, c_ref, slo_ref, shi_ref, q_ref, k_ref, v_ref, *, tiles_per_seq):
    tm = x_ref.shape[0]
    acc = jnp.dot(x_ref[...].astype(bf16), w_ref[...], preferred_element_type=f32)
    lane = lax.broadcasted_iota(jnp.int32, (tm, LANES), 1)
    row = lax.broadcasted_iota(jnp.int32, (tm, LANES), 0)
    low = lane < HEAD_DIM
    c_tab, s_lo, s_hi = c_ref[...], slo_ref[...], shi_ref[...]
    seq0 = (pl.program_id(0) % tiles_per_seq) * tm
    blk = (seq0 + row) // MOBA_BLOCK
    k_pad = jnp.where(lane == HEAD_DIM + blk, 1.0, 0.0)
    v_pad = jnp.where(lane == HEAD_DIM, 1.0, 0.0)
    n_q = N_HEADS // 2
    n_kv = N_KV_HEADS // 2
    for c in range(n_q + n_kv):
        xc = acc[:, c * LANES:(c + 1) * LANES]
        r = xc * c_tab + pltpu.roll(xc, LANES - ROT_DIM // 2, 1) * s_lo + pltpu.roll(xc, ROT_DIM // 2, 1) * s_hi
        r_odd = pltpu.roll(r, HEAD_DIM, 1)
        if c < n_q:
            q_ref[0, 2 * c] = jnp.where(low, r * Q_SCALE, 0.0).astype(bf16)
            q_ref[0, 2 * c + 1] = jnp.where(low, r_odd * Q_SCALE, 0.0).astype(bf16)
        else:
            k_ref[0, 2 * (c - n_q)] = jnp.where(low, r, k_pad).astype(bf16)
            k_ref[0, 2 * (c - n_q) + 1] = jnp.where(low, r_odd, k_pad).astype(bf16)
    for c in range(n_kv):
        xc = acc[:, (n_q + n_kv + c) * LANES:(n_q + n_kv + c + 1) * LANES]
        v_ref[0, 2 * c] = jnp.where(low, xc, v_pad).astype(bf16)
        v_ref[0, 2 * c + 1] = jnp.where(low, pltpu.roll(xc, HEAD_DIM, 1), v_pad).astype(bf16)


def _qkv(x2d, w_qkv, tabs, batch, seq):
    n = x2d.shape[0]
    nt = seq // TM_QKV
    tok = lambda i: (i, 0)
    head = lambda i: (i // nt, 0, i % nt, 0)
    return pl.pallas_call(
        functools.partial(_qkv_kernel, tiles_per_seq=nt),
        grid=(n // TM_QKV,),
        in_specs=[pl.BlockSpec((TM_QKV, D_MODEL), tok),
                  pl.BlockSpec((D_MODEL, QKV_DIM), lambda i: (0, 0)),
                  pl.BlockSpec((TM_QKV, LANES), tok),
                  pl.BlockSpec((TM_QKV, LANES), tok),
                  pl.BlockSpec((TM_QKV, LANES), tok)],
        out_specs=[pl.BlockSpec((1, N_HEADS, TM_QKV, LANES), head),
                   pl.BlockSpec((1, N_KV_HEADS, TM_QKV, LANES), head),
                   pl.BlockSpec((1, N_KV_HEADS, TM_QKV, LANES), head)],
        out_shape=[jax.ShapeDtypeStruct((batch, N_HEADS, seq, LANES), bf16),
                   jax.ShapeDtypeStruct((batch, N_KV_HEADS, seq, LANES), bf16),
                   jax.ShapeDtypeStruct((batch, N_KV_HEADS, seq, LANES), bf16)],
        compiler_params=_cparams("parallel"),
        name="qkv_rope",
    )(x2d, w_qkv, *tabs)


def _merge_heads(o, out_ref, rows):
    lane = lax.broadcasted_iota(jnp.int32, (rows, LANES), 1)
    low = lane < HEAD_DIM
    for c in range(o.shape[0] // rows // 2):
        even = o[(2 * c) * rows:(2 * c + 1) * rows]
        odd = o[(2 * c + 1) * rows:(2 * c + 2) * rows]
        out_ref[0, :, c * LANES:(c + 1) * LANES] = jnp.where(low, even, pltpu.roll(odd, HEAD_DIM, 1)).astype(out_ref.dtype)


def _moba_kernel(q_ref, k_ref, v_ref, o_ref, kmean_ref, vt_ref, qa_ref, s0_ref, s1_ref, m_ref, acc_ref):
    u = pl.program_id(2)
    blk = MOBA_BLOCK
    tq = q_ref.shape[2]
    nb = k_ref.shape[2] // blk
    n_grp = k_ref.shape[1]
    grp_rows = GROUP * tq
    rows = n_grp * grp_rows
    sub = MOBA_SUB
    n_sub = rows // sub
    nt_dims = (((1,), (1,)), ((), ()))

    @pl.when(u == 0)
    def _():
        for kv in range(n_grp):
            for j in range(nb):
                kj = k_ref[0, kv, j * blk:(j + 1) * blk, :].astype(f32)
                kmean_ref[kv, j:j + 1, :] = jnp.sum(kj, axis=0, keepdims=True) * (1.0 / blk)
                vj = v_ref[0, kv, j * blk:(j + 1) * blk, :].astype(f32)
                vt_ref[kv, :, j * blk:(j + 1) * blk] = vj.T.astype(bf16)

    for kv in range(n_grp):
        q = q_ref[0, kv * GROUP:(kv + 1) * GROUP].reshape(grp_rows, LANES)
        km = kmean_ref[kv]
        km_hi = km.astype(bf16)
        km_lo = (km - km_hi.astype(f32)).astype(bf16)
        g = (lax.dot_general(km_hi, q, nt_dims, preferred_element_type=f32)
             + lax.dot_general(km_lo, q, nt_dims, preferred_element_type=f32))
        jidx = lax.broadcasted_iota(jnp.int32, (nb, grp_rows), 0)
        q_blk = u * (tq // blk) + (lax.broadcasted_iota(jnp.int32, (nb, grp_rows), 1) % tq) // blk
        past = jidx < q_blk
        g = jnp.where(past, g, -jnp.inf)
        sel = jidx == q_blk
        for _ in range(MOBA_TOPK):
            mx = jnp.max(g, axis=0, keepdims=True)
            first = jnp.min(jnp.where(g == mx, jidx, nb), axis=0, keepdims=True)
            pick = jidx == first
            sel = jnp.logical_or(sel, jnp.logical_and(pick, past))
            g = jnp.where(pick, -jnp.inf, g)
        bias = jnp.where(sel, 0.0, NEG)
        bias_t = jnp.concatenate([jnp.zeros((HEAD_DIM, grp_rows), f32), bias,
                                  jnp.zeros((LANES - HEAD_DIM - nb, grp_rows), f32)], axis=0)
        qa_ref[kv * grp_rows:(kv + 1) * grp_rows] = (q.astype(f32) + bias_t.T).astype(bf16)

    m_ref[...] = jnp.full_like(m_ref, NEG)
    acc_ref[...] = jnp.zeros_like(acc_ref)

    def scores(c, s_ref, r):
        o = pl.multiple_of(c * MOBA_KEYS, MOBA_KEYS)
        rs = slice(r * sub, (r + 1) * sub)
        kv = r * sub // grp_rows
        s_ref[r] = lax.dot_general(k_ref[0, kv, pl.ds(o, MOBA_KEYS), :], qa_ref[rs], nt_dims,
                                   preferred_element_type=f32)

    def accumulate(c, s_ref, r, causal):
        o = pl.multiple_of(c * MOBA_KEYS, MOBA_KEYS)
        kv = r * sub // grp_rows
        s = s_ref[r]
        if causal:
            key = lax.broadcasted_iota(jnp.int32, (MOBA_KEYS, sub), 0)
            tok = (lax.broadcasted_iota(jnp.int32, (MOBA_KEYS, sub), 1) + r * sub) % tq
            s = jnp.where(jnp.logical_and(key // blk == tok // blk, key > tok), NEG, s)
        m_old = m_ref[r]
        m_new = jnp.maximum(m_old, jnp.max(s, axis=0, keepdims=True))
        p = jnp.exp2(s - m_new).astype(bf16)
        pv = jnp.dot(vt_ref[kv, :, pl.ds(o, MOBA_KEYS)], p, preferred_element_type=f32)
        acc_ref[r] = acc_ref[r] * jnp.exp2(m_old - m_new) + pv
        m_ref[r] = m_new

    def step(c_next, s_next, c_cur, s_cur, causal=False):
        for r in range(n_sub):
            if c_next is not None:
                scores(c_next, s_next, r)
            accumulate(c_cur, s_cur, r, causal)

    for r in range(n_sub):
        scores(0, s0_ref, r)

    def body(t, carry):
        step(2 * t + 1, s1_ref, 2 * t, s0_ref)
        step(2 * t + 2, s0_ref, 2 * t + 1, s1_ref)
        return carry

    lax.fori_loop(0, u // 2, body, 0)

    @pl.when(u % 2 == 1)
    def _():
        step(u, s1_ref, u - 1, s0_ref)
        step(None, None, u, s1_ref, causal=True)

    @pl.when(u % 2 == 0)
    def _():
        step(None, None, u, s0_ref, causal=True)

    heads = []
    for r in range(n_sub):
        a = acc_ref[r]
        heads.append((a / a[HEAD_DIM:HEAD_DIM + 1, :]).T)
    _merge_heads(jnp.concatenate(heads, axis=0), o_ref, tq)


def _moba(q, k, v):
    batch, _, seq, _ = q.shape
    nb = seq // MOBA_BLOCK
    n_grp = MOBA_KV_PER_STEP
    rows = n_grp * GROUP * MOBA_KEYS
    return pl.pallas_call(
        _moba_kernel,
        grid=(batch, N_KV_HEADS // n_grp, seq // MOBA_KEYS),
        in_specs=[pl.BlockSpec((1, n_grp * GROUP, MOBA_KEYS, LANES), lambda b, g, u: (b, g, u, 0)),
                  pl.BlockSpec((1, n_grp, seq, LANES), lambda b, g, u: (b, g, 0, 0)),
                  pl.BlockSpec((1, n_grp, seq, LANES), lambda b, g, u: (b, g, 0, 0))],
        out_specs=pl.BlockSpec((1, MOBA_KEYS, n_grp * GROUP * HEAD_DIM), lambda b, g, u: (b, u, g)),
        out_shape=jax.ShapeDtypeStruct((batch, seq, N_HEADS * HEAD_DIM), bf16),
        scratch_shapes=[pltpu.VMEM((n_grp, nb, LANES), f32),
                        pltpu.VMEM((n_grp, LANES, seq), bf16),
                        pltpu.VMEM((rows, LANES), bf16),
                        pltpu.VMEM((rows // MOBA_SUB, MOBA_KEYS, MOBA_SUB), f32),
                        pltpu.VMEM((rows // MOBA_SUB, MOBA_KEYS, MOBA_SUB), f32),
                        pltpu.VMEM((rows // MOBA_SUB, 1, MOBA_SUB), f32),
                        pltpu.VMEM((rows // MOBA_SUB, LANES, MOBA_SUB), f32)],
        compiler_params=pltpu.CompilerParams(dimension_semantics=("parallel", "parallel", "arbitrary"),
                                             vmem_limit_bytes=MOBA_VMEM_LIMIT),
        name="moba_attention",
    )(q, k, v)


def _swa_kernel(sink_ref, q_ref, kc_ref, kp_ref, vc_ref, vp_ref, o_ref):
    g = pl.program_id(1)
    i = pl.program_id(2)
    w = SWA_WINDOW
    n_grp = kc_ref.shape[1]
    rows = GROUP * w
    nt_dims = (((1,), (1,)), ((), ()))
    n_sb = q_ref.shape[2] // w
    key = lax.broadcasted_iota(jnp.int32, (2 * w, rows), 0)
    t_in = lax.broadcasted_iota(jnp.int32, (2 * w, rows), 1) % w
    band = jnp.logical_and(key > t_in, key <= t_in + w)
    first = jnp.logical_and(band, jnp.logical_or(key >= w, i > 0))
    head_of_row = lax.broadcasted_iota(jnp.int32, (1, rows), 1) // w
    lane = lax.broadcasted_iota(jnp.int32, (w, LANES), 1)
    low = lane < HEAD_DIM

    def window(ref_cur, ref_prev, kv, sb):
        if sb == 0:
            return jnp.concatenate([ref_prev[0, kv], ref_cur[0, kv, 0:w, :]], axis=0)
        return ref_cur[0, kv, (sb - 1) * w:(sb + 1) * w, :]

    work = [(kv, sb) for kv in range(n_grp) for sb in range(n_sb)]
    scores = [lax.dot_general(window(kc_ref, kp_ref, kv, sb),
                              q_ref[0, kv * GROUP:(kv + 1) * GROUP, sb * w:(sb + 1) * w, :].reshape(rows, LANES),
                              nt_dims, preferred_element_type=f32)
              for kv, sb in work]
    v_t = [jnp.concatenate([vp_ref[0, kv].astype(f32).T[0:VT_ROWS], vc_ref[0, kv].astype(f32).T[0:VT_ROWS]],
                           axis=1).astype(bf16) for kv in range(n_grp)]
    for (kv, sb), sc in zip(work, scores):
        sink = jnp.zeros((1, rows), f32)
        for hh in range(GROUP):
            sink = jnp.where(head_of_row == hh, sink_ref[(g * n_grp + kv) * GROUP + hh] * LOG2E, sink)
        s = jnp.where(first if sb == 0 else band, sc, NEG)
        m = jnp.maximum(jnp.max(s, axis=0, keepdims=True), sink)
        p = jnp.exp2(s - m).astype(bf16)
        acc = jnp.dot(v_t[kv][:, sb * w:(sb + 2) * w], p, preferred_element_type=f32)
        out_t = acc / (acc[HEAD_DIM:HEAD_DIM + 1, :] + jnp.exp2(sink - m))
        out = jnp.concatenate([out_t, jnp.zeros((LANES - VT_ROWS, rows), f32)], axis=0).T
        for c in range(GROUP // 2):
            even = out[(2 * c) * w:(2 * c + 1) * w]
            odd = out[(2 * c + 1) * w:(2 * c + 2) * w]
            chunk = kv * (GROUP // 2) + c
            o_ref[0, sb * w:(sb + 1) * w, chunk * LANES:(chunk + 1) * LANES] = jnp.where(
                low, even, pltpu.roll(odd, HEAD_DIM, 1)).astype(o_ref.dtype)


def _swa(q, k, v, sinks):
    batch, _, seq, _ = q.shape
    per = TQ_SWA // SWA_WINDOW
    n_grp = SWA_KV_PER_STEP
    cur = lambda b, g, i, s: (b, g, i, 0)
    prev = lambda b, g, i, s: (b, g, jnp.maximum(i * per - 1, 0), 0)
    grid_spec = pltpu.PrefetchScalarGridSpec(
        num_scalar_prefetch=1,
        grid=(batch, N_KV_HEADS // n_grp, seq // TQ_SWA),
        in_specs=[pl.BlockSpec((1, n_grp * GROUP, TQ_SWA, LANES), cur),
                  pl.BlockSpec((1, n_grp, TQ_SWA, LANES), cur),
                  pl.BlockSpec((1, n_grp, SWA_WINDOW, LANES), prev),
                  pl.BlockSpec((1, n_grp, TQ_SWA, LANES), cur),
                  pl.BlockSpec((1, n_grp, SWA_WINDOW, LANES), prev)],
        out_specs=pl.BlockSpec((1, TQ_SWA, n_grp * GROUP * HEAD_DIM), lambda b, g, i, s: (b, i, g)),
    )
    return pl.pallas_call(
        _swa_kernel,
        grid_spec=grid_spec,
        out_shape=jax.ShapeDtypeStruct((batch, seq, N_HEADS * HEAD_DIM), bf16),
        compiler_params=_cparams("parallel", "parallel", "arbitrary"),
        name="swa_attention",
    )(sinks, q, k, k, v, v)


def _layer_norm(x, g, b):
    mu = jnp.mean(x, axis=-1, keepdims=True)
    xc = x - mu
    var = jnp.mean(xc * xc, axis=-1, keepdims=True)
    return xc * lax.rsqrt(var + LN_EPS) * g + b


def _load_rows(ref, n_rows):
    return jnp.concatenate([ref[pl.ds(k, n_rows, stride=ROW_TILE), :] for k in range(ROW_TILE)], axis=1)


def _store_rows(ref, val):
    for k in range(ROW_TILE):
        ref[pl.ds(k, val.shape[0], stride=ROW_TILE), :] = val[:, k * LANES:(k + 1) * LANES]


def _mix_router_kernel(x_ref, a_ref, wo_ref, g_ref, b_ref, wr_ref, x1_ref, meta_ref, metat_ref, cnt_ref):
    tm = x_ref.shape[0]
    y = ALPHA * x_ref[...] + jnp.dot(a_ref[...], wo_ref[...], preferred_element_type=f32)
    x1 = _layer_norm(y, g_ref[...], b_ref[...])
    _store_rows(x1_ref, x1)

    @pl.when(pl.program_id(0) == 0)
    def _():
        cnt_ref[...] = jnp.zeros_like(cnt_ref)

    logits = jnp.dot(x1, wr_ref[...], preferred_element_type=f32)
    lane = lax.broadcasted_iota(jnp.int32, (tm, LANES), 1)
    lg = jnp.where(lane < N_EXPERTS, logits, -jnp.inf)
    m1 = jnp.max(lg, axis=1, keepdims=True)
    i1 = jnp.min(jnp.where(lg == m1, lane, LANES), axis=1, keepdims=True)
    lg2 = jnp.where(lane == i1, -jnp.inf, lg)
    m2 = jnp.max(lg2, axis=1, keepdims=True)
    i2 = jnp.min(jnp.where(lg2 == m2, lane, LANES), axis=1, keepdims=True)
    e2 = jnp.exp(m2 - m1)
    w1 = 1.0 / (1.0 + e2)
    w2 = e2 / (1.0 + e2)
    hit1 = lane == i1
    hit2 = lane == i2
    hits = jnp.where(jnp.logical_or(hit1, hit2), 1.0, 0.0)
    r_i = lax.broadcasted_iota(jnp.int32, (tm, tm), 0)
    c_i = lax.broadcasted_iota(jnp.int32, (tm, tm), 1)
    before = jnp.where(c_i < r_i, 1.0, 0.0).astype(bf16)
    cum = jnp.dot(before, hits.astype(bf16), preferred_element_type=f32) + cnt_ref[0:1, :]
    r1 = jnp.sum(jnp.where(hit1, cum, 0.0), axis=1, keepdims=True)
    r2 = jnp.sum(jnp.where(hit2, cum, 0.0), axis=1, keepdims=True)
    cnt_ref[...] = cnt_ref[...] + jnp.sum(hits, axis=0, keepdims=True)
    meta = jnp.where(lane == 0, i1.astype(f32), 0.0)
    meta = jnp.where(lane == 1, i2.astype(f32), meta)
    meta = jnp.where(lane == 2, w1, meta)
    meta = jnp.where(lane == 3, w2, meta)
    meta = jnp.where(lane == 4, r1, meta)
    meta = jnp.where(lane == 5, r2, meta)
    meta_ref[...] = meta
    metat_ref[...] = meta.T[0:8, :]


def _mix_router(x2d, a2d, w_o, g, b, w_router):
    n = x2d.shape[0]
    tok = lambda i: (i, 0)
    const = lambda i: (0, 0)
    wr = jnp.zeros((D_MODEL, LANES), f32).at[:, :N_EXPERTS].set(w_router)
    return pl.pallas_call(
        _mix_router_kernel, grid=(n // TM_MIX,),
        in_specs=[pl.BlockSpec((TM_MIX, D_MODEL), tok),
                  pl.BlockSpec((TM_MIX, D_MODEL), tok),
                  pl.BlockSpec((D_MODEL, D_MODEL), const),
                  pl.BlockSpec((1, D_MODEL), const),
                  pl.BlockSpec((1, D_MODEL), const),
                  pl.BlockSpec((D_MODEL, LANES), const)],
        out_specs=[pl.BlockSpec((TM_MIX * ROW_TILE, LANES), tok),
                   pl.BlockSpec((TM_MIX, LANES), tok), pl.BlockSpec((8, TM_MIX), lambda i: (0, i)),
                   pl.BlockSpec((8, LANES), const)],
        out_shape=[jax.ShapeDtypeStruct((n * ROW_TILE, LANES), f32),
                   jax.ShapeDtypeStruct((n, LANES), f32), jax.ShapeDtypeStruct((8, n), f32),
                   jax.ShapeDtypeStruct((8, LANES), f32)],
        compiler_params=_cparams("arbitrary"), name="mix_ln_router",
    )(x2d, a2d, w_o, g, b, wr)


def _swiglu_accumulate(acc_ref, xb, wg, wu, wd, fc):
    gate = jnp.dot(xb, wg.astype(bf16), preferred_element_type=f32)
    up = jnp.dot(xb, wu.astype(bf16), preferred_element_type=f32)
    h = (gate / (1.0 + jnp.exp(-gate)) * up).astype(bf16)
    part = jnp.dot(h, wd.astype(bf16), preferred_element_type=f32)

    @pl.when(fc == 0)
    def _():
        acc_ref[...] = part

    @pl.when(fc > 0)
    def _():
        acc_ref[...] = acc_ref[...] + part


def _moe_ffn_kernel(te_ref, nact_ref, x_ref, wg_ref, wu_ref, wd_ref, y_ref, xb_ref, acc_ref):
    n = pl.program_id(0)
    fc = pl.program_id(1)
    active = n < nact_ref[0]

    @pl.when(jnp.logical_and(jnp.logical_not(active), fc == 0))
    def _():
        y_ref[...] = jnp.zeros_like(y_ref)

    @pl.when(active)
    def _():
        @pl.when(fc == 0)
        def _():
            xb_ref[...] = _load_rows(x_ref, TM_FFN).astype(bf16)

        _swiglu_accumulate(acc_ref, xb_ref[...], wg_ref[0], wu_ref[0], wd_ref[0], fc)

        @pl.when(fc == pl.num_programs(1) - 1)
        def _():
            _store_rows(y_ref, acc_ref[...])


def _moe_ffn(xs_rows, w_gate, w_up, w_down, tile_expert, n_active):
    blk = (TM_FFN * ROW_TILE, LANES)
    n_tiles = xs_rows.shape[0] // blk[0]
    n_fc = D_FF // FC_FFN

    def row(n, f, te, na):
        return jnp.minimum(n, na[0] - 1), 0

    def w_in(n, f, te, na):
        return te[jnp.minimum(n, na[0] - 1)], 0, jnp.where(n < na[0], f, n_fc - 1)

    def w_out(n, f, te, na):
        return te[jnp.minimum(n, na[0] - 1)], jnp.where(n < na[0], f, n_fc - 1), 0

    grid_spec = pltpu.PrefetchScalarGridSpec(
        num_scalar_prefetch=2,
        grid=(n_tiles, n_fc),
        in_specs=[pl.BlockSpec(blk, row),
                  pl.BlockSpec((1, D_MODEL, FC_FFN), w_in),
                  pl.BlockSpec((1, D_MODEL, FC_FFN), w_in),
                  pl.BlockSpec((1, FC_FFN, D_MODEL), w_out)],
        out_specs=pl.BlockSpec(blk, lambda n, f, te, na: (n, 0)),
        scratch_shapes=[pltpu.VMEM((TM_FFN, D_MODEL), bf16), pltpu.VMEM((TM_FFN, D_MODEL), f32)],
    )
    return pl.pallas_call(
        _moe_ffn_kernel, grid_spec=grid_spec,
        out_shape=jax.ShapeDtypeStruct(xs_rows.shape, f32),
        compiler_params=_cparams("arbitrary", "arbitrary"), name="moe_swiglu",
    )(tile_expert, n_active, xs_rows, w_gate, w_up, w_down)


def _dense_tail_kernel(x_ref, a_ref, wo_ref, g1_ref, b1_ref, wg_ref, wu_ref, wd_ref, g2_ref, b2_ref,
                       p_ref, wp_ref, wgate_ref, o_ref, x1_ref, xb_ref):
    fc = pl.program_id(1)
    acc_ref = o_ref

    @pl.when(fc == 0)
    def _():
        y = ALPHA * x_ref[...] + jnp.dot(a_ref[...], wo_ref[...], preferred_element_type=f32)
        x1 = _layer_norm(y, g1_ref[...], b1_ref[...])
        x1_ref[...] = x1
        xb_ref[...] = x1.astype(bf16)

    _swiglu_accumulate(acc_ref, xb_ref[...], wg_ref[...], wu_ref[...], wd_ref[...], fc)

    @pl.when(fc == pl.num_programs(1) - 1)
    def _():
        x2 = _layer_norm(ALPHA * x1_ref[...] + acc_ref[...], g2_ref[...], b2_ref[...])
        _ple(x2, p_ref, wp_ref, wgate_ref, o_ref)


def _dense_tail(x2d, a2d, w_o, g1, b1, w_gate, w_up, w_down, g2, b2, p2d, w_proj, w_pgate):
    n = x2d.shape[0]
    tok = lambda i, f: (i, 0)
    const = lambda i, f: (0, 0)
    once = dict(pipeline_mode=pl.Buffered(1))
    return pl.pallas_call(
        _dense_tail_kernel, grid=(n // TM_DENSE, D_FF // FC_FFN),
        in_specs=[pl.BlockSpec((TM_DENSE, D_MODEL), tok),
                  pl.BlockSpec((TM_DENSE, D_MODEL), tok),
                  pl.BlockSpec((D_MODEL, D_MODEL), const, **once),
                  pl.BlockSpec((1, D_MODEL), const, **once),
                  pl.BlockSpec((1, D_MODEL), const, **once),
                  pl.BlockSpec((D_MODEL, FC_FFN), lambda i, f: (0, f)),
                  pl.BlockSpec((D_MODEL, FC_FFN), lambda i, f: (0, f)),
                  pl.BlockSpec((FC_FFN, D_MODEL), lambda i, f: (f, 0)),
                  pl.BlockSpec((1, D_MODEL), const, **once),
                  pl.BlockSpec((1, D_MODEL), const, **once),
                  pl.BlockSpec((TM_DENSE, PLE_DIM), tok),
                  pl.BlockSpec((PLE_DIM, D_MODEL), const, **once),
                  pl.BlockSpec((D_MODEL, D_MODEL), const, **once)],
        out_specs=pl.BlockSpec((TM_DENSE, D_MODEL), tok),
        out_shape=jax.ShapeDtypeStruct((n, D_MODEL), f32),
        scratch_shapes=[pltpu.VMEM((TM_DENSE, D_MODEL), f32), pltpu.VMEM((TM_DENSE, D_MODEL), bf16)],
        compiler_params=pltpu.CompilerParams(dimension_semantics=("parallel", "arbitrary"),
                                             vmem_limit_bytes=MOBA_VMEM_LIMIT), name="dense_tail",
    )(x2d, a2d, w_o, g1, b1, w_gate, w_up, w_down, g2, b2, p2d, w_proj, w_pgate)


def _dispatch_kernel(last_ref, nact_ref, pos_ref, x_ref, xs_ref, zero_ref, sem, zsem):
    tm = x_ref.shape[0] // ROW_TILE
    n_tiles = xs_ref.shape[0] // zero_ref.shape[0]

    @pl.when(pl.program_id(0) == 0)
    def _():
        zero_ref[...] = jnp.zeros_like(zero_ref)

        def fill(t):
            dst = pl.multiple_of(t * zero_ref.shape[0], zero_ref.shape[0])
            return pltpu.make_async_copy(zero_ref, xs_ref.at[pl.ds(dst, zero_ref.shape[0])], zsem)

        def start_tail(t, _):
            fill(t).start()
            return 0

        def wait_tail(t, _):
            fill(t).wait()
            return 0

        for e in range(N_EXPERTS):
            pl.when(last_ref[e] >= 0)(lambda e=e: fill(last_ref[e]).start())
        lax.fori_loop(nact_ref[0], n_tiles, start_tail, 0)
        for e in range(N_EXPERTS):
            pl.when(last_ref[e] >= 0)(lambda e=e: fill(last_ref[e]).wait())
        lax.fori_loop(nact_ref[0], n_tiles, wait_tail, 0)

    def start(r, _):
        src = x_ref.at[pl.ds(pl.multiple_of(r * ROW_TILE, ROW_TILE), ROW_TILE)]
        for k in range(2):
            dst = pl.multiple_of(pos_ref[2 * r + k] * ROW_TILE, ROW_TILE)
            pltpu.make_async_copy(src, xs_ref.at[pl.ds(dst, ROW_TILE)], sem).start(priority=k)
        return 0

    lax.fori_loop(0, tm, start, 0, unroll=DMA_UNROLL)
    for _ in range(2):
        pltpu.make_async_copy(x_ref, xs_ref.at[pl.ds(0, tm * ROW_TILE)], sem).wait()


def _dispatch(x1_rows, pos_flat, last_tile, n_active, n_tiles):
    n = x1_rows.shape[0] // ROW_TILE
    grid_spec = pltpu.PrefetchScalarGridSpec(
        num_scalar_prefetch=2,
        grid=(n // TM_DISPATCH,),
        in_specs=[pl.BlockSpec((2 * TM_DISPATCH,), lambda i, lt, na: (i,), memory_space=pltpu.SMEM),
                  pl.BlockSpec((TM_DISPATCH * ROW_TILE, LANES), lambda i, lt, na: (i, 0))],
        out_specs=pl.BlockSpec(memory_space=pl.ANY),
        scratch_shapes=[pltpu.VMEM((TM_FFN * ROW_TILE, LANES), f32), pltpu.SemaphoreType.DMA(()),
                        pltpu.SemaphoreType.DMA(())],
    )
    return pl.pallas_call(
        _dispatch_kernel, grid_spec=grid_spec,
        out_shape=jax.ShapeDtypeStruct((n_tiles * TM_FFN * ROW_TILE, LANES), f32),
        compiler_params=_cparams("arbitrary"), name="moe_dispatch",
    )(last_tile, n_active, pos_flat, x1_rows)


def _ple(x2, p_ref, wp_ref, wgate_ref, o_ref):
    pe = jnp.dot(p_ref[...].astype(bf16), wp_ref[...], preferred_element_type=f32)
    z = jnp.dot(x2.astype(bf16), wgate_ref[...], preferred_element_type=f32)
    o_ref[...] = x2 + pe / (1.0 + jnp.exp(-z))


def _out_moe_kernel(pos_ref, nxt_ref, x1_ref, meta_ref, y_ref, g_ref, b_ref, p_ref, wp_ref, wgate_ref, o_ref,
                    buf_ref, sems):
    i = pl.program_id(0)
    tm = x1_ref.shape[0] // ROW_TILE
    slot = i % 2

    def gather(idx_ref, to_slot):
        def start(r, _):
            dst = pl.ds(pl.multiple_of(r * ROW_TILE, ROW_TILE), ROW_TILE)
            for k in range(2):
                src = pl.multiple_of(idx_ref[2 * r + k] * ROW_TILE, ROW_TILE)
                pltpu.make_async_copy(y_ref.at[pl.ds(src, ROW_TILE)], buf_ref.at[to_slot, k, dst],
                                      sems.at[to_slot]).start(priority=k)
            return 0

        lax.fori_loop(0, tm, start, 0, unroll=DMA_UNROLL)

    def wait(s):
        for k in range(2):
            pltpu.make_async_copy(y_ref.at[pl.ds(0, tm * ROW_TILE)], buf_ref.at[s, k], sems.at[s]).wait()

    @pl.when(i == 0)
    def _():
        gather(pos_ref, slot)

    wait(slot)
    meta = meta_ref[...]
    f = meta[:, 2:3] * _load_rows(buf_ref.at[slot, 0], tm) + meta[:, 3:4] * _load_rows(buf_ref.at[slot, 1], tm)
    x2 = _layer_norm(ALPHA * _load_rows(x1_ref, tm) + f, g_ref[...], b_ref[...])
    for r in range(tm):
        for k in range(2):
            src = pl.multiple_of(nxt_ref[2 * r + k] * ROW_TILE, ROW_TILE)
            pltpu.make_async_copy(y_ref.at[pl.ds(src, ROW_TILE)], buf_ref.at[1 - slot, k, pl.ds(r * ROW_TILE, ROW_TILE)],
                                  sems.at[1 - slot]).start(priority=k)
    _ple(x2, p_ref, wp_ref, wgate_ref, o_ref)

    @pl.when(i == pl.num_programs(0) - 1)
    def _():
        wait(1 - slot)


def _out_moe(x1_rows, meta, y_rows, pos_flat, g, b, p2d, w_proj, w_gate):
    n = x1_rows.shape[0] // ROW_TILE
    steps = n // TM_OUT
    tok = lambda i: (i, 0)
    const = lambda i: (0, 0)
    return pl.pallas_call(
        _out_moe_kernel, grid=(steps,),
        in_specs=[pl.BlockSpec((2 * TM_OUT,), lambda i: (i,), memory_space=pltpu.SMEM),
                  pl.BlockSpec((2 * TM_OUT,), lambda i: (jnp.minimum(i + 1, steps - 1),), memory_space=pltpu.SMEM),
                  pl.BlockSpec((TM_OUT * ROW_TILE, LANES), tok),
                  pl.BlockSpec((TM_OUT, LANES), tok),
                  pl.BlockSpec(memory_space=pl.ANY),
                  pl.BlockSpec((1, D_MODEL), const),
                  pl.BlockSpec((1, D_MODEL), const),
                  pl.BlockSpec((TM_OUT, PLE_DIM), tok),
                  pl.BlockSpec((PLE_DIM, D_MODEL), const),
                  pl.BlockSpec((D_MODEL, D_MODEL), const)],
        out_specs=pl.BlockSpec((TM_OUT, D_MODEL), tok),
        out_shape=jax.ShapeDtypeStruct((n, D_MODEL), f32),
        scratch_shapes=[pltpu.VMEM((2, 2, TM_OUT * ROW_TILE, LANES), f32), pltpu.SemaphoreType.DMA((2,))],
        compiler_params=_cparams("arbitrary"), name="moe_combine_ln_ple",
    )(pos_flat, pos_flat, x1_rows, meta, y_rows, g, b, p2d, w_proj, w_gate)


def _routing_plan(meta_t, counts_row, n_tiles):
    i1 = meta_t[0].astype(jnp.int32)
    i2 = meta_t[1].astype(jnp.int32)
    r1 = meta_t[4].astype(jnp.int32)
    r2 = meta_t[5].astype(jnp.int32)
    counts = counts_row[0, :N_EXPERTS].astype(jnp.int32)
    tiles = (counts + TM_FFN - 1) // TM_FFN
    tile_end = jnp.cumsum(tiles)
    offset = (tile_end - tiles) * TM_FFN
    pos = jnp.stack([offset[i1] + r1, offset[i2] + r2], axis=1).reshape(-1)
    tile_expert = jnp.sum(jnp.arange(n_tiles)[:, None] >= tile_end[None, :], axis=1)
    tile_expert = jnp.minimum(tile_expert, N_EXPERTS - 1)
    last_tile = jnp.where(tiles > 0, tile_end - 1, -1)
    return (pos.astype(jnp.int32), tile_expert.astype(jnp.int32), tile_end[-1:].astype(jnp.int32),
            last_tile.astype(jnp.int32))


def kernel(x, p, positions, w_qkv, w_o, ln_mix_g, ln_mix_b, ln_ffn_g, ln_ffn_b, sinks, w_ffn_gate, w_ffn_up,
           w_ffn_down, w_router, w_exp_gate, w_exp_up, w_exp_down, w_ple_proj, w_ple_gate):
    batch, seq, _ = x.shape
    n = batch * seq
    assert seq % TQ_SWA == 0 and (seq // MOBA_BLOCK) % 8 == 0 and seq // MOBA_BLOCK <= LANES - HEAD_DIM
    tabs = _rope_tables(positions)
    x2d = x.reshape(n, D_MODEL)
    row = lambda v: v.reshape(1, D_MODEL)
    for i in range(DEPTH):
        j = i // 2
        q, k, v = _qkv(x2d, w_qkv[i].astype(bf16), tabs, batch, seq)
        if i % 2 == 0:
            a = _moba(q, k, v)
        else:
            a = _swa(q, k, v, sinks[j])
        a2d = a.reshape(n, D_MODEL)
        wo = w_o[i].astype(bf16)
        ple_args = (row(ln_ffn_g[i]), row(ln_ffn_b[i]), p[i].reshape(n, PLE_DIM),
                    w_ple_proj[i].astype(bf16), w_ple_gate[i].astype(bf16))
        if i % 2 == 0:
            x2d = _dense_tail(x2d, a2d, wo, row(ln_mix_g[i]), row(ln_mix_b[i]), w_ffn_gate[j].astype(bf16),
                              w_ffn_up[j].astype(bf16), w_ffn_down[j].astype(bf16), *ple_args)
        else:
            x1, meta, meta_t, counts = _mix_router(x2d, a2d, wo, row(ln_mix_g[i]), row(ln_mix_b[i]), w_router[j])
            n_tiles = 2 * n // TM_FFN + N_EXPERTS
            pos, tile_expert, n_active, last_tile = _routing_plan(meta_t, counts, n_tiles)
            xs = _dispatch(x1, pos, last_tile, n_active, n_tiles)
            y = _moe_ffn(xs, w_exp_gate[j], w_exp_up[j], w_exp_down[j], tile_expert, n_active)
            x2d = _out_moe(x1, meta, y, pos, *ple_args)
    return x2d.reshape(batch, seq, D_MODEL)
```
